```python
import jax, jax.numpy as jnp
from jax import lax
import numpy as np

D_MODEL = 2048
BATCH = 4
SEQ = 4096
DEPTH = 1

CTX_LEN = 256
GRID_W = 64

MIX_A = 1024
MIX_B = 1024
SGU_GROUPS = 4
SGU_CHUNK = 128
SGU_GW = MIX_A // SGU_GROUPS
GLA_HEADS = 4
GLA_DK = MIX_B // 2 // GLA_HEADS
GLA_DV = MIX_B // GLA_HEADS
GLA_KDIM = GLA_HEADS * GLA_DK
GLA_LOWRANK = 16
GLA_NORMALIZER = 16.0
GLA_CHUNK = 64
D_FF = 5632
CONV_W = 3
EPS = 1e-6
N_IN = 2 * MIX_A + 2 * GLA_KDIM + 2 * MIX_B + 2 * GLA_LOWRANK
N_MOD = 6

kernel_name = "hybrid_sgu_gla_convglu_prefix_dit_layer"


def rmsnorm(x, g):
    xf = x.astype(jnp.float32)
    y = xf * lax.rsqrt(jnp.mean(xf * xf, axis=-1, keepdims=True) + EPS)
    return (y * g.astype(jnp.float32)).astype(x.dtype)


def layernorm(x, g, b):
    xf = x.astype(jnp.float32)
    mu = jnp.mean(xf, axis=-1, keepdims=True)
    var = jnp.mean(jnp.square(xf - mu), axis=-1, keepdims=True)
    y = (xf - mu) * lax.rsqrt(var + EPS)
    return (y * g.astype(jnp.float32) + b.astype(jnp.float32)).astype(x.dtype)


def flip_t(t):
    return jnp.flip(t, axis=1)


def spatial_gating(u, v, ln_g, ln_b, w_s, b_s):
    B, L, _ = v.shape
    n = L // SGU_CHUNK
    v = v.reshape(B, n, SGU_CHUNK, SGU_GROUPS, SGU_GW)
    v = layernorm(v, ln_g.reshape(SGU_GROUPS, SGU_GW), ln_b.reshape(SGU_GROUPS, SGU_GW))
    s = jnp.einsum('gpq,bnqgc->bnpgc', w_s, v) + b_s.T[:, :, None]
    return u * s.reshape(B, L, MIX_A)


def gla_log_decay(lr, w, b):
    B, L, _ = lr.shape
    z = (lr @ w + b).astype(jnp.float32)
    return (jax.nn.log_sigmoid(z) / GLA_NORMALIZER).reshape(B, L, GLA_HEADS, GLA_DK)


def gla_chunked(q, k, v, g, s0):
    B, L, H, DK = q.shape
    n = L // GLA_CHUNK
    f = lambda t: t.astype(jnp.float32).reshape(B, n, GLA_CHUNK, H, t.shape[-1])
    q, k, v, g = f(q), f(k), f(v), f(g)
    b = jnp.cumsum(g, axis=2)
    b_last = b[:, :, -1:]
    q_e = q * jnp.exp(b)
    k_e = k * jnp.exp(-b)
    k_d = k * jnp.exp(b_last - b)
    mask = jnp.tril(jnp.ones((GLA_CHUNK, GLA_CHUNK), dtype=bool))
    att = jnp.where(mask, jnp.einsum('bnthd,bnshd->bnhts', q_e, k_e), 0.0)
    o_intra = jnp.einsum('bnhts,bnshv->bnthv', att, v)

    def step(S, inp):
        qe_c, kd_c, v_c, dec_c = inp
        o_c = jnp.einsum('bthd,bhdv->bthv', qe_c, S)
        S = dec_c[..., None] * S + jnp.einsum('bthd,bthv->bhdv', kd_c, v_c)
        return S, o_c

    xs = (jnp.moveaxis(q_e, 1, 0), jnp.moveaxis(k_d, 1, 0), jnp.moveaxis(v, 1, 0),
          jnp.moveaxis(jnp.exp(b_last[:, :, 0]), 1, 0))
    s_fin, o_inter = lax.scan(step, s0.astype(jnp.float32), xs)
    o = o_intra + jnp.moveaxis(o_inter, 0, 1)
    return o.reshape(B, L, H, v.shape[-1]), s_fin


def mix_stream(a, s0_f, s0_b, w_in, sgu_ln_g, sgu_ln_b, sgu_w, sgu_b,
               gla_gate_w_f, gla_gate_b_f, gla_gate_w_b, gla_gate_b_b, gla_norm_g, w_out, with_output):
    B, L, _ = a.shape
    p = a @ w_in
    cuts = np.cumsum([MIX_A, MIX_A, GLA_KDIM, GLA_KDIM, MIX_B, MIX_B, GLA_LOWRANK]).tolist()
    u, v_s, q, k, v, r, lr_f, lr_b = jnp.split(p, cuts, axis=-1)
    q = q.reshape(B, L, GLA_HEADS, GLA_DK) * (GLA_DK ** -0.5)
    k = k.reshape(B, L, GLA_HEADS, GLA_DK)
    v = v.reshape(B, L, GLA_HEADS, GLA_DV)
    g_f = gla_log_decay(lr_f, gla_gate_w_f, gla_gate_b_f)
    g_b = gla_log_decay(lr_b, gla_gate_w_b, gla_gate_b_b)
    o_f, s_f = gla_chunked(q, k, v, g_f, s0_f)
    o_b_rev, s_b = gla_chunked(flip_t(q), flip_t(k), flip_t(v), flip_t(g_b), s0_b)
    if not with_output:
        return None, s_f, s_b
    o = o_f + flip_t(o_b_rev)
    o = o * lax.rsqrt(jnp.mean(o * o, axis=-1, keepdims=True) + EPS) * gla_norm_g.astype(jnp.float32)
    o = o * jax.nn.silu(r.astype(jnp.float32)).reshape(B, L, GLA_HEADS, GLA_DV)
    y_b = o.reshape(B, L, MIX_B).astype(a.dtype)
    y_a = spatial_gating(jax.nn.gelu(u, approximate=True), jax.nn.gelu(v_s, approximate=True),
                         sgu_ln_g, sgu_ln_b, sgu_w, sgu_b)
    y = jnp.concatenate([y_a, y_b], axis=-1) @ w_out
    return y, s_f, s_b


def conv_glu_ffn(h, w_up, conv_w, conv_b, w_down, rows):
    B, L, _ = h.shape
    a, val = jnp.split(h @ w_up, 2, axis=-1)
    a = a.reshape(B, rows, L // rows, D_FF)
    a = lax.conv_general_dilated(a, conv_w[:, :, None, :].astype(a.dtype), (1, 1), 'SAME',
                                 dimension_numbers=('NHWC', 'HWIO', 'NHWC'),
                                 feature_group_count=D_FF) + conv_b
    a = a.reshape(B, L, D_FF)
    return (jax.nn.gelu(a, approximate=True) * val) @ w_down


def setup_inputs(seed: int = 0) -> dict:
    key = jax.random.key(seed)
    ks = jax.random.split(key, 32)
    nrm = lambda k, shape, s: jax.random.normal(k, shape, jnp.float32) * s
    gain = lambda k, shape: 1.0 + 0.05 * jax.random.normal(k, shape, jnp.float32)
    D = D_MODEL
    return {
        "x": nrm(ks[0], (BATCH, SEQ, D), 1.0),
        "c": nrm(ks[1], (BATCH, D), 1.0),
        "ctx": nrm(ks[2], (BATCH, CTX_LEN, D), 1.0),
        "c_ctx": nrm(ks[3], (D,), 1.0),
        "ada_w": nrm(ks[4], (DEPTH, D, N_MOD * D), D ** -0.5),
        "ada_b": nrm(ks[5], (DEPTH, N_MOD * D), 0.02),
        "pre_mix_g": gain(ks[6], (DEPTH, D)),
        "post_mix_g": gain(ks[7], (DEPTH, D)),
        "pre_ffn_g": gain(ks[8], (DEPTH, D)),
        "post_ffn_g": gain(ks[9], (DEPTH, D)),
        "w_in": nrm(ks[10], (DEPTH, D, N_IN), D ** -0.5),
        "sgu_ln_g": gain(ks[11], (DEPTH, MIX_A)),
        "sgu_ln_b": nrm(ks[12], (DEPTH, MIX_A), 0.02),
        "sgu_w": nrm(ks[13], (DEPTH, SGU_GROUPS, SGU_CHUNK, SGU_CHUNK), SGU_CHUNK ** -0.5),
        "sgu_b": gain(ks[14], (DEPTH, SGU_GROUPS, SGU_CHUNK)),
        "gla_gate_w_f": nrm(ks[15], (DEPTH, GLA_LOWRANK, GLA_KDIM), GLA_LOWRANK ** -0.5),
        "gla_gate_b_f": nrm(ks[16], (DEPTH, GLA_KDIM), 0.1),
        "gla_gate_w_b": nrm(ks[17], (DEPTH, GLA_LOWRANK, GLA_KDIM), GLA_LOWRANK ** -0.5),
        "gla_gate_b_b": nrm(ks[18], (DEPTH, GLA_KDIM), 0.1),
        "gla_norm_g": gain(ks[19], (DEPTH, GLA_DV)),
        "w_out": nrm(ks[20], (DEPTH, MIX_A + MIX_B, D), (MIX_A + MIX_B) ** -0.5),
        "ffn_w_up": nrm(ks[21], (DEPTH, D, 2 * D_FF), D ** -0.5),
        "ffn_conv_w": nrm(ks[22], (DEPTH, CONV_W, CONV_W, D_FF), 1.0 / CONV_W),
        "ffn_conv_b": nrm(ks[23], (DEPTH, D_FF), 0.02),
        "ffn_w_down": nrm(ks[24], (DEPTH, D_FF, D), D_FF ** -0.5),
    }


def reference(x, c, ctx, c_ctx, ada_w, ada_b, pre_mix_g, post_mix_g, pre_ffn_g, post_ffn_g,
              w_in, sgu_ln_g, sgu_ln_b, sgu_w, sgu_b, gla_gate_w_f, gla_gate_b_f,
              gla_gate_w_b, gla_gate_b_b, gla_norm_g, w_out, ffn_w_up, ffn_conv_w, ffn_conv_b, ffn_w_down):
    B, L, _ = x.shape
    rows = L // GRID_W
    s_zero = jnp.zeros((B, GLA_HEADS, GLA_DK, GLA_DV), jnp.float32)
    h, hc = x, ctx
    for i in range(DEPTH):
        ctx_out = i < DEPTH - 1
        mod = jax.nn.silu(c) @ ada_w[i] + ada_b[i]
        sh_m, sc_m, gt_m, sh_f, sc_f, gt_f = jnp.split(mod[:, None, :], N_MOD, axis=-1)
        mod_c = jax.nn.silu(c_ctx) @ ada_w[i] + ada_b[i]
        csh_m, csc_m, cgt_m, csh_f, csc_f, cgt_f = jnp.split(mod_c, N_MOD, axis=-1)
        mix_w = (w_in[i], sgu_ln_g[i], sgu_ln_b[i], sgu_w[i], sgu_b[i], gla_gate_w_f[i], gla_gate_b_f[i],
                 gla_gate_w_b[i], gla_gate_b_b[i], gla_norm_g[i], w_out[i])
        a_ctx = rmsnorm(hc, pre_mix_g[i]) * (1.0 + csc_m) + csh_m
        y_ctx, s_f, s_b = mix_stream(a_ctx, s_zero, s_zero, *mix_w, ctx_out)
        a_lat = rmsnorm(h, pre_mix_g[i]) * (1.0 + sc_m) + sh_m
        y_lat, _, _ = mix_stream(a_lat, s_f, s_b, *mix_w, True)
        h = h + gt_m * rmsnorm(y_lat, post_mix_g[i])
        f = rmsnorm(h, pre_ffn_g[i]) * (1.0 + sc_f) + sh_f
        h = h + gt_f * rmsnorm(conv_glu_ffn(f, ffn_w_up[i], ffn_conv_w[i], ffn_conv_b[i], ffn_w_down[i], rows),
                               post_ffn_g[i])
        if ctx_out:
            hc = hc + cgt_m * rmsnorm(y_ctx, post_mix_g[i])
            fc = rmsnorm(hc, pre_ffn_g[i]) * (1.0 + csc_f) + csh_f
            hc = hc + cgt_f * rmsnorm(conv_glu_ffn(fc, ffn_w_up[i], ffn_conv_w[i], ffn_conv_b[i], ffn_w_down[i], 1),
                                      post_ffn_g[i])
    return h
```

```python
import functools

import jax
import jax.numpy as jnp
from jax import lax
from jax.experimental import pallas as pl
from jax.experimental.pallas import tpu as pltpu

F32 = jnp.float32
BF16 = jnp.bfloat16

EPS = 1e-6
GRID_W = 64
MIX_A = 1024
MIX_B = 1024
SGU_GROUPS = 4
SGU_CHUNK = 128
SGU_GW = MIX_A // SGU_GROUPS
GLA_HEADS = 4
GLA_DK = 128
GLA_DV = 256
GLA_KDIM = GLA_HEADS * GLA_DK
GLA_LOWRANK = 16
GLA_NORMALIZER = 16.0
GLA_CHUNK = 64
N_MOD = 6
LANES = 128
VMEM_LIMIT = 56 * 1024 * 1024

COL_U, COL_VS, COL_Q, COL_K, COL_V, COL_R, COL_LR = 0, 1024, 2048, 2560, 3072, 4096, 5120
N_MAIN = COL_LR


def _dot(a, b):
    return jnp.dot(a, b, preferred_element_type=F32)


def _silu(x):
    return x / (1.0 + jnp.exp(-x))


def _gelu(x):
    return jax.nn.gelu(x, approximate=True)


def _params(sem):
    return pltpu.CompilerParams(dimension_semantics=sem, vmem_limit_bytes=VMEM_LIMIT)


def _mod_kernel(c_ref, w_ref, b_ref, o_ref):
    s = _silu(c_ref[...]).astype(BF16)
    o_ref[...] = _dot(s, w_ref[...].astype(BF16)) + b_ref[...]


def _modulation(c_rows, ada_w, ada_b, tn=1024):
    rows, d = c_rows.shape
    n = ada_w.shape[1]
    return pl.pallas_call(
        _mod_kernel,
        grid=(n // tn,),
        in_specs=[pl.BlockSpec((rows, d), lambda j: (0, 0)),
                  pl.BlockSpec((d, tn), lambda j: (0, j)),
                  pl.BlockSpec((1, tn), lambda j: (0, j))],
        out_specs=pl.BlockSpec((rows, tn), lambda j: (0, j)),
        out_shape=jax.ShapeDtypeStruct((rows, n), F32),
        compiler_params=_params(("arbitrary",)),
        name="mod",
    )(c_rows, ada_w, ada_b)


def _inproj_kernel(x_ref, sh_ref, sc_ref, g_ref, w_ref, wlr_ref, p_ref, lr_ref, a_scr, *, acts):
    j = pl.program_id(1)

    @pl.when(j == 0)
    def _():
        x = x_ref[...]
        y = x * lax.rsqrt(jnp.mean(x * x, axis=-1, keepdims=True) + EPS) * g_ref[...]
        a = (y * (1.0 + sc_ref[0]) + sh_ref[0]).astype(BF16)
        a_scr[...] = a
        lr_ref[...] = _dot(a, wlr_ref[...])

    acc = _dot(a_scr[...], w_ref[...])
    fns = {"gelu": _gelu, "silu": _silu, "none": lambda t: t}
    for name in sorted(set(acts)):
        idx = [t for t, a in enumerate(acts) if a == name]
        cond = functools.reduce(jnp.logical_or, [j == t for t in idx])

        @pl.when(cond)
        def _(name=name):
            p_ref[...] = fns[name](acc).astype(p_ref.dtype)


def _inproj(x2d, shift, scale, gain, w, w_lr, acts, tm, tn, rows_per_mod):
    m, d = x2d.shape
    n = w.shape[1]
    assert m % tm == 0 and n % tn == 0 and len(acts) == n // tn and rows_per_mod % tm == 0
    tiles_per_mod = rows_per_mod // tm
    return pl.pallas_call(
        functools.partial(_inproj_kernel, acts=acts),
        grid=(m // tm, n // tn),
        in_specs=[pl.BlockSpec((tm, d), lambda i, j: (i, 0)),
                  pl.BlockSpec((1, 1, d), lambda i, j: (i // tiles_per_mod, 0, 0)),
                  pl.BlockSpec((1, 1, d), lambda i, j: (i // tiles_per_mod, 0, 0)),
                  pl.BlockSpec((1, d), lambda i, j: (0, 0)),
                  pl.BlockSpec((d, tn), lambda i, j: (0, j)),
                  pl.BlockSpec((d, LANES), lambda i, j: (0, 0))],
        out_specs=[pl.BlockSpec((tm, tn), lambda i, j: (i, j)),
                   pl.BlockSpec((tm, LANES), lambda i, j: (i, 0))],
        out_shape=[jax.ShapeDtypeStruct((m, n), BF16),
                   jax.ShapeDtypeStruct((m, LANES), F32)],
        scratch_shapes=[pltpu.VMEM((tm, d), BF16)],
        compiler_params=_params(("parallel", "arbitrary")),
        name="inproj",
    )(x2d, shift, scale, gain, w, w_lr)


def _log_decay(lr_c, w_ref, b_ref):
    z = _dot(lr_c.astype(BF16), w_ref[...]) + b_ref[...]
    log_sig = -(jnp.maximum(-z, 0.0) + jnp.log(1.0 + jnp.exp(-jnp.abs(z))))
    return log_sig * (1.0 / GLA_NORMALIZER)


def _chunk_cumsum(tri, g):
    hi = g.astype(BF16)
    r1 = g - hi.astype(F32)
    mid = r1.astype(BF16)
    lo = (r1 - mid.astype(F32)).astype(BF16)
    return _dot(tri, hi) + _dot(tri, mid) + _dot(tri, lo)


def _col_of_row(row):
    n = row.shape[1]
    return jnp.transpose(jnp.broadcast_to(row, (n, n)))[:, :1]


def _gla_unit(q, k, v, lr_c, s_ref, w_ref, b_ref, tri, last_row, with_out):
    g = _log_decay(lr_c, w_ref, b_ref)
    b = _chunk_cumsum(tri, g)
    b_last = b[last_row:last_row + 1, :]
    k32 = k.astype(F32)
    s_old = s_ref[...]
    k_d = (k32 * jnp.exp(b_last - b)).astype(BF16)
    kv = lax.dot_general(k_d, v, (((0,), (0,)), ((), ())), preferred_element_type=F32)
    s_ref[...] = _col_of_row(jnp.exp(b_last)) * s_old + kv
    if not with_out:
        return None
    q_e = (q.astype(F32) * (jnp.exp(b) * (GLA_DK ** -0.5))).astype(BF16)
    k_e = (k32 * jnp.exp(-b)).astype(BF16)
    att = lax.dot_general(q_e, k_e, (((1,), (1,)), ((), ())), preferred_element_type=F32)
    att = jnp.where(tri > 0, att, 0.0).astype(BF16)
    return _dot(att, v) + _dot(q_e, s_old.astype(BF16))


def _gla_kernel(q_ref, k_ref, v_ref, r_ref, lr_ref, kc_ref, vc_ref, lrc_ref,
                wf_ref, bf_ref, wb_ref, bb_ref, ng_ref, y_ref, sf_scr, sb_scr, o_scr):
    c = GLA_CHUNK
    n_lat = q_ref.shape[0] // c
    n_ctx = kc_ref.shape[0] // c
    rows = lax.broadcasted_iota(jnp.int32, (c, c), 0)
    cols = lax.broadcasted_iota(jnp.int32, (c, c), 1)
    tri_f = (cols <= rows).astype(BF16)
    tri_b = (cols >= rows).astype(BF16)

    sf_scr[...] = jnp.zeros_like(sf_scr)
    sb_scr[...] = jnp.zeros_like(sb_scr)

    for n in range(n_ctx):
        rf = pl.ds(n * c, c)
        rb = pl.ds((n_ctx - 1 - n) * c, c)
        _gla_unit(None, kc_ref[rf, :], vc_ref[rf, :], lrc_ref[rf, :], sf_scr, wf_ref, bf_ref,
                  tri_f, c - 1, False)
        _gla_unit(None, kc_ref[rb, :], vc_ref[rb, :], lrc_ref[rb, :], sb_scr, wb_ref, bb_ref,
                  tri_b, 0, False)

    def lat_step(n, accumulate):
        rf = pl.ds(pl.multiple_of(n * c, c), c)
        rb = pl.ds(pl.multiple_of((n_lat - 1 - n) * c, c), c)
        o_f = _gla_unit(q_ref[rf, :], k_ref[rf, :], v_ref[rf, :], lr_ref[rf, :], sf_scr,
                        wf_ref, bf_ref, tri_f, c - 1, True)
        o_b = _gla_unit(q_ref[rb, :], k_ref[rb, :], v_ref[rb, :], lr_ref[rb, :], sb_scr,
                        wb_ref, bb_ref, tri_b, 0, True)
        if accumulate:
            o_scr[rf, :] += o_f
            o_scr[rb, :] += o_b
        else:
            o_scr[rf, :] = o_f
            o_scr[rb, :] = o_b

    def first_half(n, carry):
        lat_step(n, False)
        return carry

    def second_half(n, carry):
        lat_step(n, True)
        return carry

    lax.fori_loop(0, n_lat // 2, first_half, 0)
    lax.fori_loop(n_lat // 2, n_lat, second_half, 0)

    blk = 256

    def finalize(t, carry):
        rs = pl.ds(pl.multiple_of(t * blk, blk), blk)
        o = o_scr[rs, :]
        o = o * lax.rsqrt(jnp.mean(o * o, axis=-1, keepdims=True) + EPS) * ng_ref[...]
        y_ref[rs, :] = (o * r_ref[rs, :].astype(F32)).astype(y_ref.dtype)
        return carry

    lax.fori_loop(0, q_ref.shape[0] // blk, finalize, 0)


def _gla(p_lat, lr_lat, p_ctx, lr_ctx, wf, bf, wb, bb, norm_g):
    bsz, l, _ = p_lat.shape
    lc = p_ctx.shape[1]
    assert l % (2 * GLA_CHUNK) == 0 and lc % GLA_CHUNK == 0
    qb, kb = COL_Q // GLA_DK, COL_K // GLA_DK
    vb, rb = COL_V // GLA_DV, COL_R // GLA_DV
    return pl.pallas_call(
        _gla_kernel,
        grid=(bsz, GLA_HEADS),
        in_specs=[pl.BlockSpec((None, l, GLA_DK), lambda b, h: (b, 0, qb + h)),
                  pl.BlockSpec((None, l, GLA_DK), lambda b, h: (b, 0, kb + h)),
                  pl.BlockSpec((None, l, GLA_DV), lambda b, h: (b, 0, vb + h)),
                  pl.BlockSpec((None, l, GLA_DV), lambda b, h: (b, 0, rb + h)),
                  pl.BlockSpec((None, l, LANES), lambda b, h: (b, 0, 0)),
                  pl.BlockSpec((None, lc, GLA_DK), lambda b, h: (b, 0, h)),
                  pl.BlockSpec((None, lc, GLA_DV), lambda b, h: (b, 0, GLA_KDIM // GLA_DV + h)),
                  pl.BlockSpec((None, lc, LANES), lambda b, h: (b, 0, 0)),
                  pl.BlockSpec((LANES, GLA_DK), lambda b, h: (0, h)),
                  pl.BlockSpec((1, GLA_DK), lambda b, h: (0, h)),
                  pl.BlockSpec((LANES, GLA_DK), lambda b, h: (0, h)),
                  pl.BlockSpec((1, GLA_DK), lambda b, h: (0, h)),
                  pl.BlockSpec((1, GLA_DV), lambda b, h: (0, 0))],
        out_specs=pl.BlockSpec((None, l, GLA_DV), lambda b, h: (b, 0, h)),
        out_shape=jax.ShapeDtypeStruct((bsz, l, MIX_B), BF16),
        scratch_shapes=[pltpu.VMEM((GLA_DK, GLA_DV), F32),
                        pltpu.VMEM((GLA_DK, GLA_DV), F32),
                        pltpu.VMEM((l, GLA_DV), F32)],
        compiler_params=_params(("parallel", "arbitrary")),
        name="gla",
    )(p_lat, p_lat, p_lat, p_lat, lr_lat, p_ctx, p_ctx, lr_ctx, wf, bf, wb, bb, norm_g)


def _sgu_kernel(u_ref, v_ref, lg_ref, lb_ref, ws_ref, bs_ref, y_ref):
    tm = u_ref.shape[0]
    for cidx in range(tm // SGU_CHUNK):
        rs = pl.ds(cidx * SGU_CHUNK, SGU_CHUNK)
        for g in range(SGU_GROUPS):
            cs = pl.ds(g * SGU_GW, SGU_GW)
            v = v_ref[rs, cs].astype(F32)
            mu = jnp.mean(v, axis=-1, keepdims=True)
            vc = v - mu
            var = jnp.mean(vc * vc, axis=-1, keepdims=True)
            vn = vc * lax.rsqrt(var + EPS) * lg_ref[:, cs] + lb_ref[:, cs]
            s = _dot(ws_ref[g], vn.astype(BF16)) + bs_ref[g]
            y_ref[rs, cs] = (u_ref[rs, cs].astype(F32) * s).astype(y_ref.dtype)


def _sgu(p_lat2d, ln_g, ln_b, w_s, b_s, tm=512):
    m = p_lat2d.shape[0]
    return pl.pallas_call(
        _sgu_kernel,
        grid=(m // tm,),
        in_specs=[pl.BlockSpec((tm, MIX_A), lambda i: (i, COL_U // MIX_A)),
                  pl.BlockSpec((tm, MIX_A), lambda i: (i, COL_VS // MIX_A)),
                  pl.BlockSpec((1, MIX_A), lambda i: (0, 0)),
                  pl.BlockSpec((1, MIX_A), lambda i: (0, 0)),
                  pl.BlockSpec((SGU_GROUPS, SGU_CHUNK, SGU_CHUNK), lambda i: (0, 0, 0)),
                  pl.BlockSpec((SGU_GROUPS, SGU_CHUNK, 1), lambda i: (0, 0, 0))],
        out_specs=pl.BlockSpec((tm, MIX_A), lambda i: (i, 0)),
        out_shape=jax.ShapeDtypeStruct((m, MIX_A), BF16),
        compiler_params=_params(("parallel",)),
        name="sgu",
    )(p_lat2d, p_lat2d, ln_g, ln_b, w_s, b_s)


def _rms(x, g):
    return x * lax.rsqrt(jnp.mean(x * x, axis=-1, keepdims=True) + EPS) * g


def _outproj_kernel(ya_ref, yb_ref, wa_ref, wb_ref, x_ref, gt_ref, pg_ref, fg_ref, sc_ref, sh_ref,
                    h_ref, f_ref):
    y = _dot(ya_ref[...], wa_ref[...]) + _dot(yb_ref[...], wb_ref[...])
    h = x_ref[...] + gt_ref[0] * _rms(y, pg_ref[...])
    h_ref[...] = h
    f_ref[...] = (_rms(h, fg_ref[...]) * (1.0 + sc_ref[0]) + sh_ref[0]).astype(f_ref.dtype)


def _outproj(y_a, y_b, w_out, x2d, gate, post_g, ffn_g, scale_f, shift_f, rows_per_mod, tm=512):
    m, d = x2d.shape
    tiles_per_mod = rows_per_mod // tm
    mod_spec = pl.BlockSpec((1, 1, d), lambda i: (i // tiles_per_mod, 0, 0))
    vec_spec = pl.BlockSpec((1, d), lambda i: (0, 0))
    return pl.pallas_call(
        _outproj_kernel,
        grid=(m // tm,),
        in_specs=[pl.BlockSpec((tm, MIX_A), lambda i: (i, 0)),
                  pl.BlockSpec((tm, MIX_B), lambda i: (i, 0)),
                  pl.BlockSpec((MIX_A, d), lambda i: (0, 0)),
                  pl.BlockSpec((MIX_B, d), lambda i: (MIX_A // MIX_B, 0)),
                  pl.BlockSpec((tm, d), lambda i: (i, 0)),
                  mod_spec, vec_spec, vec_spec, mod_spec, mod_spec],
        out_specs=[pl.BlockSpec((tm, d), lambda i: (i, 0)),
                   pl.BlockSpec((tm, d), lambda i: (i, 0))],
        out_shape=[jax.ShapeDtypeStruct((m, d), F32),
                   jax.ShapeDtypeStruct((m, d), BF16)],
        compiler_params=_params(("parallel",)),
        name="outproj",
    )(y_a, y_b, w_out, w_out, x2d, gate, post_g, ffn_g, scale_f, shift_f)


def _ffn_up_kernel(f_ref, fp_ref, fn_ref, wa_ref, wv_ref, cw_ref, cb_ref, g_ref, fext_scr, *,
                   tiles_per_image):
    i = pl.program_id(0)
    j = pl.program_id(1)
    tm = f_ref.shape[0]
    hw = GRID_W

    @pl.when(j == 0)
    def _():
        top = (i % tiles_per_image) == 0
        bottom = (i % tiles_per_image) == tiles_per_image - 1
        fext_scr[pl.ds(0, hw), :] = jnp.where(top, jnp.zeros_like(fp_ref), fp_ref[...])
        fext_scr[pl.ds(hw, tm), :] = f_ref[...]
        fext_scr[pl.ds(hw + tm, hw), :] = jnp.where(bottom, jnp.zeros_like(fn_ref), fn_ref[...])

    a = _dot(fext_scr[...], wa_ref[...])
    val = _dot(f_ref[...], wv_ref[...])
    ext = tm + 2 * hw
    col = lax.broadcasted_iota(jnp.int32, (ext, 1), 0) % hw
    left = jnp.where(col > 0, pltpu.roll(a, 1, 0), 0.0)
    right = jnp.where(col < hw - 1, pltpu.roll(a, ext - 1, 0), 0.0)
    cw = cw_ref[...]
    acc = jnp.broadcast_to(cb_ref[...], val.shape)
    for dr in range(3):
        lo = dr * hw
        acc = acc + cw[3 * dr + 0:3 * dr + 1, :] * left[lo:lo + tm, :]
        acc = acc + cw[3 * dr + 1:3 * dr + 2, :] * a[lo:lo + tm, :]
        acc = acc + cw[3 * dr + 2:3 * dr + 3, :] * right[lo:lo + tm, :]
    g_ref[...] = (_gelu(acc) * val).astype(g_ref.dtype)


def _ffn_up(f2d, w_up, conv_w, conv_b, rows_per_image, tm=1024, tf=512):
    m, d = f2d.shape
    d_ff = w_up.shape[1] // 2
    assert rows_per_image % tm == 0 and tm % GRID_W == 0 and d_ff % tf == 0
    hb = tm // GRID_W
    n_halo = m // GRID_W
    return pl.pallas_call(
        functools.partial(_ffn_up_kernel, tiles_per_image=rows_per_image // tm),
        grid=(m // tm, d_ff // tf),
        in_specs=[pl.BlockSpec((tm, d), lambda i, j: (i, 0)),
                  pl.BlockSpec((GRID_W, d), lambda i, j: (jnp.maximum(i * hb - 1, 0), 0)),
                  pl.BlockSpec((GRID_W, d), lambda i, j: (jnp.minimum((i + 1) * hb, n_halo - 1), 0)),
                  pl.BlockSpec((d, tf), lambda i, j: (0, j)),
                  pl.BlockSpec((d, tf), lambda i, j: (0, d_ff // tf + j)),
                  pl.BlockSpec((9, tf), lambda i, j: (0, j)),
                  pl.BlockSpec((1, tf), lambda i, j: (0, j))],
        out_specs=pl.BlockSpec((tm, tf), lambda i, j: (i, j)),
        out_shape=jax.ShapeDtypeStruct((m, d_ff), BF16),
        scratch_shapes=[pltpu.VMEM((tm + 2 * GRID_W, d), BF16)],
        compiler_params=_params(("parallel", "arbitrary")),
        name="ffn_up",
    )(f2d, f2d, f2d, w_up, w_up, conv_w, conv_b)


def _ffn_down_kernel(g_ref, w_ref, h_ref, gt_ref, pg_ref, o_ref, acc_scr):
    k = pl.program_id(1)
    part = _dot(g_ref[...], w_ref[...])

    @pl.when(k == 0)
    def _():
        acc_scr[...] = part

    @pl.when(k > 0)
    def _():
        acc_scr[...] += part

    @pl.when(k == pl.num_programs(1) - 1)
    def _():
        o_ref[...] = h_ref[...] + gt_ref[0] * _rms(acc_scr[...], pg_ref[...])


def _ffn_down(g2d, w_down, h2d, gate, post_g, rows_per_mod, tm=512, tk=1408):
    m, d = h2d.shape
    d_ff = g2d.shape[1]
    assert d_ff % tk == 0 and m % tm == 0
    tiles_per_mod = rows_per_mod // tm
    return pl.pallas_call(
        _ffn_down_kernel,
        grid=(m // tm, d_ff // tk),
        in_specs=[pl.BlockSpec((tm, tk), lambda i, k: (i, k)),
                  pl.BlockSpec((tk, d), lambda i, k: (k, 0)),
                  pl.BlockSpec((tm, d), lambda i, k: (i, 0)),
                  pl.BlockSpec((1, 1, d), lambda i, k: (i // tiles_per_mod, 0, 0)),
                  pl.BlockSpec((1, d), lambda i, k: (0, 0))],
        out_specs=pl.BlockSpec((tm, d), lambda i, k: (i, 0)),
        out_shape=jax.ShapeDtypeStruct((m, d), F32),
        scratch_shapes=[pltpu.VMEM((tm, d), F32)],
        compiler_params=_params(("parallel", "arbitrary")),
        name="ffn_down",
    )(g2d, w_down, h2d, gate, post_g)


def _pad_gate_w(w, row0):
    return jnp.zeros((LANES, w.shape[1]), BF16).at[row0:row0 + w.shape[0]].set(w.astype(BF16))


def kernel(x, c, ctx, c_ctx, ada_w, ada_b, pre_mix_g, post_mix_g, pre_ffn_g, post_ffn_g, w_in, sgu_ln_g, sgu_ln_b, sgu_w, sgu_b, gla_gate_w_f, gla_gate_b_f, gla_gate_w_b, gla_gate_b_b, gla_norm_g, w_out, ffn_w_up, ffn_conv_w, ffn_conv_b, ffn_w_down):
    bsz, l, d = x.shape
    lc = ctx.shape[1]
    assert ada_w.shape[0] == 1, "single-layer kernel"
    d_ff = ffn_w_down.shape[1]

    c_rows = jnp.zeros((8, d), F32).at[:bsz].set(c).at[bsz].set(c_ctx)
    mod = _modulation(c_rows, ada_w[0], ada_b)
    sh_m, sc_m, gt_m, sh_f, sc_f, gt_f = [mod[:bsz, t * d:(t + 1) * d].reshape(bsz, 1, d) for t in range(N_MOD)]
    csh_m = mod[bsz:bsz + 1, 0:d].reshape(1, 1, d)
    csc_m = mod[bsz:bsz + 1, d:2 * d].reshape(1, 1, d)

    w_in_bf = w_in[0].astype(BF16)
    w_main = w_in_bf[:, :N_MAIN]
    w_lr = jnp.pad(w_in_bf[:, COL_LR:], ((0, 0), (0, LANES - 2 * GLA_LOWRANK)))
    w_ctx = w_in_bf[:, COL_K:COL_R]

    p_ctx, lr_ctx = _inproj(ctx.reshape(bsz * lc, d), csh_m, csc_m, pre_mix_g, w_ctx, w_lr,
                            ("none",) * 3, tm=lc, tn=512, rows_per_mod=bsz * lc)
    x2d = x.reshape(bsz * l, d)
    p_lat, lr_lat = _inproj(x2d, sh_m, sc_m, pre_mix_g, w_main, w_lr,
                            ("gelu", "gelu", "none", "none", "silu"), tm=1024, tn=1024, rows_per_mod=l)

    y_b = _gla(p_lat.reshape(bsz, l, N_MAIN), lr_lat.reshape(bsz, l, LANES),
               p_ctx.reshape(bsz, lc, -1), lr_ctx.reshape(bsz, lc, LANES),
               _pad_gate_w(gla_gate_w_f[0], 0), gla_gate_b_f,
               _pad_gate_w(gla_gate_w_b[0], GLA_LOWRANK), gla_gate_b_b, gla_norm_g)
    y_a = _sgu(p_lat, sgu_ln_g, sgu_ln_b, sgu_w[0].astype(BF16), sgu_b[0][:, :, None])

    h, f = _outproj(y_a, y_b.reshape(bsz * l, MIX_B), w_out[0].astype(BF16), x2d, gt_m,
                    post_mix_g, pre_ffn_g, sc_f, sh_f, rows_per_mod=l)
    g = _ffn_up(f, ffn_w_up[0].astype(BF16), ffn_conv_w[0].reshape(9, d_ff), ffn_conv_b, rows_per_image=l)
    out = _ffn_down(g, ffn_w_down[0].astype(BF16), h, gt_f, post_ffn_g, rows_per_mod=l)
    return out.reshape(bsz, l, d)
```

```python
import functools

import jax
import jax.numpy as jnp
from jax import lax
from jax.experimental import pallas as pl
from jax.experimental.pallas import tpu as pltpu

F32 = jnp.float32
BF16 = jnp.bfloat16

EPS = 1e-6
GRID_W = 64
MIX_A = 1024
MIX_B = 1024
SGU_GROUPS = 4
SGU_CHUNK = 128
SGU_GW = MIX_A // SGU_GROUPS
GLA_HEADS = 4
GLA_DK = 128
GLA_DV = 256
GLA_KDIM = GLA_HEADS * GLA_DK
GLA_LOWRANK = 16
GLA_NORMALIZER = 16.0
GLA_CHUNK = 64
N_MOD = 6
LANES = 128
VMEM_LIMIT = 56 * 1024 * 1024

COL_U, COL_VS, COL_Q, COL_K, COL_V, COL_R, COL_LR = 0, 1024, 2048, 2560, 3072, 4096, 5120
N_MAIN = COL_LR


def _dot(a, b):
    return jnp.dot(a, b, preferred_element_type=F32)


def _silu(x):
    return x / (1.0 + jnp.exp(-x))


def _gelu(x):
    return jax.nn.gelu(x, approximate=True)


def _params(sem):
    return pltpu.CompilerParams(dimension_semantics=sem, vmem_limit_bytes=VMEM_LIMIT)


def _mod_kernel(c_ref, w_ref, b_ref, o_ref):
    s = _silu(c_ref[...]).astype(BF16)
    o_ref[...] = _dot(s, w_ref[...].astype(BF16)) + b_ref[...]


def _modulation(c_rows, ada_w, ada_b, tn=1024):
    rows, d = c_rows.shape
    n = ada_w.shape[1]
    return pl.pallas_call(
        _mod_kernel,
        grid=(n // tn,),
        in_specs=[pl.BlockSpec((rows, d), lambda j: (0, 0)),
                  pl.BlockSpec((d, tn), lambda j: (0, j)),
                  pl.BlockSpec((1, tn), lambda j: (0, j))],
        out_specs=pl.BlockSpec((rows, tn), lambda j: (0, j)),
        out_shape=jax.ShapeDtypeStruct((rows, n), F32),
        compiler_params=_params(("arbitrary",)),
        name="mod",
    )(c_rows, ada_w, ada_b)


def _inproj_kernel(x_ref, sh_ref, sc_ref, g_ref, w_ref, wlr_ref, p_ref, lr_ref, a_scr, *, acts):
    j = pl.program_id(1)

    @pl.when(j == 0)
    def _():
        x = x_ref[...]
        y = x * lax.rsqrt(jnp.mean(x * x, axis=-1, keepdims=True) + EPS) * g_ref[...]
        a = (y * (1.0 + sc_ref[0]) + sh_ref[0]).astype(BF16)
        a_scr[...] = a
        lr_ref[...] = _dot(a, wlr_ref[...])

    acc = _dot(a_scr[...], w_ref[...])
    fns = {"gelu": _gelu, "silu": _silu, "none": lambda t: t}
    for name in sorted(set(acts)):
        idx = [t for t, a in enumerate(acts) if a == name]
        cond = functools.reduce(jnp.logical_or, [j == t for t in idx])

        @pl.when(cond)
        def _(name=name):
            p_ref[...] = fns[name](acc).astype(p_ref.dtype)


def _inproj(x2d, shift, scale, gain, w, w_lr, acts, tm, tn, rows_per_mod):
    m, d = x2d.shape
    n = w.shape[1]
    assert m % tm == 0 and n % tn == 0 and len(acts) == n // tn and rows_per_mod % tm == 0
    tiles_per_mod = rows_per_mod // tm
    return pl.pallas_call(
        functools.partial(_inproj_kernel, acts=acts),
        grid=(m // tm, n // tn),
        in_specs=[pl.BlockSpec((tm, d), lambda i, j: (i, 0)),
                  pl.BlockSpec((1, 1, d), lambda i, j: (i // tiles_per_mod, 0, 0)),
                  pl.BlockSpec((1, 1, d), lambda i, j: (i // tiles_per_mod, 0, 0)),
                  pl.BlockSpec((1, d), lambda i, j: (0, 0)),
                  pl.BlockSpec((d, tn), lambda i, j: (0, j)),
                  pl.BlockSpec((d, LANES), lambda i, j: (0, 0))],
        out_specs=[pl.BlockSpec((tm, tn), lambda i, j: (i, j)),
                   pl.BlockSpec((tm, LANES), lambda i, j: (i, 0))],
        out_shape=[jax.ShapeDtypeStruct((m, n), BF16),
                   jax.ShapeDtypeStruct((m, LANES), F32)],
        scratch_shapes=[pltpu.VMEM((tm, d), BF16)],
        compiler_params=_params(("parallel", "arbitrary")),
        name="inproj",
    )(x2d, shift, scale, gain, w, w_lr)


GLA_BLOCK = 4 * GLA_CHUNK
GLA_PAIR = 2 * GLA_CHUNK


def _log_decay(z):
    log_sig = -(jnp.maximum(-z, 0.0) + jnp.log(1.0 + jnp.exp(-jnp.abs(z))))
    return log_sig * (1.0 / GLA_NORMALIZER)


def _split(x):
    hi = x.astype(BF16)
    return hi, (x - hi.astype(F32)).astype(BF16)


def _nt_dot(a, b):
    return lax.dot_general(a, b, (((1,), (1,)), ((), ())), preferred_element_type=F32)


def _gla_masks():
    n, c = GLA_BLOCK, GLA_CHUNK
    r = jnp.arange(n)[:, None]
    s = jnp.arange(n)[None, :]
    same = (r // c) == (s // c)
    return ((same & (s <= r)).astype(BF16), (same & (s >= r)).astype(BF16), jnp.eye(GLA_DK, dtype=BF16))


def _chunk_rows(blk, cidx):
    return pl.ds(pl.multiple_of((blk * (GLA_BLOCK // GLA_CHUNK) + cidx) * GLA_DK, GLA_DK), GLA_DK)


def _gla_gates(rs, blk, q_ref, k_ref, lr_ref, w_ref, b_ref, lower_ref, eye_ref, fwd, bwd):
    c, nb, dk = GLA_CHUNK, GLA_BLOCK, GLA_DK
    lr = lr_ref[rs, :].astype(BF16)
    g = _log_decay(_dot(lr, w_ref[...]) + b_ref[...])
    g_hi, g_lo = _split(g)
    p = _dot(lower_ref[...], g_hi) + _dot(lower_ref[...], g_lo)
    last = [p[i * c + c - 1:(i + 1) * c, :] for i in range(nb // c)]
    tot = jnp.concatenate([jnp.broadcast_to(t, (c, 2 * dk)) for t in last], axis=0)
    k32 = k_ref[rs, :].astype(F32)
    q32 = None if q_ref is None else q_ref[rs, :].astype(F32) * (GLA_DK ** -0.5)
    plans = ((fwd, slice(0, dk), p, tot - p), (bwd, slice(dk, 2 * dk), tot - p + g, p - g))
    for d, lanes, b_cum, to_end in plans:
        k_d = (k32 * jnp.exp(to_end[:, lanes])).astype(BF16)
        d["kdt"][:, rs] = _nt_dot(eye_ref[...], k_d).astype(BF16)
        t_hi, t_lo = zip(*[_split(t[:, lanes]) for t in last])
        rows_hi = jnp.concatenate([jnp.broadcast_to(t, (dk, dk)) for t in t_hi], axis=0)
        rows_lo = jnp.concatenate([jnp.broadcast_to(t, (dk, dk)) for t in t_lo], axis=0)
        bl = _nt_dot(eye_ref[...], rows_hi) + _nt_dot(eye_ref[...], rows_lo)
        for cidx in range(nb // c):
            d["dec"][_chunk_rows(blk, cidx), :] = jnp.exp(bl[:, cidx * dk:(cidx + 1) * dk])
        if q32 is not None:
            d["qe"][rs, :] = (q32 * jnp.exp(b_cum[:, lanes])).astype(BF16)
            d["ke"][rs, :] = (k32 * jnp.exp(-b_cum[:, lanes])).astype(BF16)


def _gla_scan_block(rs, blk, order, d, v_ref, s_scr, record):
    c = GLA_CHUNK
    v = v_ref[rs, :]
    kd_t = d["kdt"][:, rs]
    zeros = jnp.zeros((c, v.shape[1]), v.dtype)
    s = s_scr[...]
    for cidx in order:
        pair, half = divmod(cidx, 2)
        slab = kd_t[:, pair * GLA_PAIR:(pair + 1) * GLA_PAIR]
        vc = v[cidx * c:(cidx + 1) * c]
        v_only = jnp.concatenate([vc, zeros] if half == 0 else [zeros, vc], axis=0)
        kv = _dot(slab, v_only)
        if record:
            d["snap"][_chunk_rows(blk, cidx), :] = s.astype(BF16)
        dec = d["dec"][_chunk_rows(blk, cidx), :]
        s = jnp.concatenate([dec, dec], axis=1) * s + kv
    s_scr[...] = s


def _gla_output(rs, blk, v_ref, r_ref, dirs, ng_ref, y_ref):
    c = GLA_CHUNK
    v = v_ref[rs, :]
    o = None
    for d in dirs:
        q_e = d["qe"][rs, :]
        att = jnp.where(d["tri"][...] > 0, _nt_dot(q_e, d["ke"][rs, :]), 0.0).astype(BF16)
        inter = [_dot(q_e[cidx * c:(cidx + 1) * c], d["snap"][_chunk_rows(blk, cidx), :])
                 for cidx in range(GLA_BLOCK // c)]
        od = _dot(att, v) + jnp.concatenate(inter, axis=0)
        o = od if o is None else o + od
    o = o * lax.rsqrt(jnp.mean(o * o, axis=-1, keepdims=True) + EPS) * ng_ref[...]
    y_ref[rs, :] = (o * r_ref[rs, :].astype(F32)).astype(y_ref.dtype)


def _gla_kernel(q_ref, k_ref, v_ref, r_ref, lr_ref, kc_ref, vc_ref, lrc_ref, w_ref, b_ref, ng_ref,
                lower_ref, upper_ref, eye_ref,
                y_ref, sf_scr, sb_scr, qef_scr, qeb_scr, kef_scr, keb_scr, kdtf_scr, kdtb_scr,
                decf_scr, decb_scr, snapf_scr, snapb_scr):
    fwd = dict(tri=lower_ref, qe=qef_scr, ke=kef_scr, kdt=kdtf_scr, dec=decf_scr, snap=snapf_scr)
    bwd = dict(tri=upper_ref, qe=qeb_scr, ke=keb_scr, kdt=kdtb_scr, dec=decb_scr, snap=snapb_scr)
    dirs = (fwd, bwd)
    l, lc = q_ref.shape[0], kc_ref.shape[0]
    n_blk, n_cblk = l // GLA_BLOCK, lc // GLA_BLOCK
    asc = tuple(range(GLA_BLOCK // GLA_CHUNK))
    desc = asc[::-1]

    def block_rows(j):
        return pl.ds(pl.multiple_of(j * GLA_BLOCK, GLA_BLOCK), GLA_BLOCK)

    sf_scr[...] = jnp.zeros_like(sf_scr)
    sb_scr[...] = jnp.zeros_like(sb_scr)

    for j in range(n_cblk):
        _gla_gates(block_rows(j), j, None, kc_ref, lrc_ref, w_ref, b_ref, lower_ref, eye_ref, fwd, bwd)
    for j in range(n_cblk):
        _gla_scan_block(block_rows(j), j, asc, fwd, vc_ref, sf_scr, False)
        jb = n_cblk - 1 - j
        _gla_scan_block(block_rows(jb), jb, desc, bwd, vc_ref, sb_scr, False)

    def gates(j, carry):
        _gla_gates(block_rows(j), j, q_ref, k_ref, lr_ref, w_ref, b_ref, lower_ref, eye_ref, fwd, bwd)
        return carry

    def scan(j, carry):
        _gla_scan_block(block_rows(j), j, asc, fwd, v_ref, sf_scr, True)
        jb = n_blk - 1 - j
        _gla_scan_block(block_rows(jb), jb, desc, bwd, v_ref, sb_scr, True)
        return carry

    def output(j, carry):
        _gla_output(block_rows(j), j, v_ref, r_ref, dirs, ng_ref, y_ref)
        return carry

    lax.fori_loop(0, n_blk, gates, 0)
    lax.fori_loop(0, n_blk, scan, 0)
    lax.fori_loop(0, n_blk, output, 0)


def _gla(p_lat, lr_lat, p_ctx, lr_ctx, gate_w, gate_b, norm_g):
    bsz, l, _ = p_lat.shape
    lc = p_ctx.shape[1]
    assert l % GLA_BLOCK == 0 and lc % GLA_BLOCK == 0 and lc <= l
    qb, kb = COL_Q // GLA_DK, COL_K // GLA_DK
    vb, rb = COL_V // GLA_DV, COL_R // GLA_DV
    gate_specs = [pl.BlockSpec((LANES, 2 * GLA_DK), lambda b, h: (0, h)),
                  pl.BlockSpec((1, 2 * GLA_DK), lambda b, h: (0, h))]
    masks = _gla_masks()
    mask_specs = [pl.BlockSpec(m.shape, lambda b, h: (0, 0)) for m in masks]
    n_chunks = l // GLA_CHUNK
    return pl.pallas_call(
        _gla_kernel,
        grid=(bsz, GLA_HEADS),
        in_specs=[pl.BlockSpec((None, l, GLA_DK), lambda b, h: (b, 0, qb + h)),
                  pl.BlockSpec((None, l, GLA_DK), lambda b, h: (b, 0, kb + h)),
                  pl.BlockSpec((None, l, GLA_DV), lambda b, h: (b, 0, vb + h)),
                  pl.BlockSpec((None, l, GLA_DV), lambda b, h: (b, 0, rb + h)),
                  pl.BlockSpec((None, l, LANES), lambda b, h: (b, 0, 0)),
                  pl.BlockSpec((None, lc, GLA_DK), lambda b, h: (b, 0, h)),
                  pl.BlockSpec((None, lc, GLA_DV), lambda b, h: (b, 0, GLA_KDIM // GLA_DV + h)),
                  pl.BlockSpec((None, lc, LANES), lambda b, h: (b, 0, 0))]
                 + gate_specs
                 + [pl.BlockSpec((1, GLA_DV), lambda b, h: (0, 0))] + mask_specs,
        out_specs=pl.BlockSpec((None, l, GLA_DV), lambda b, h: (b, 0, h)),
        out_shape=jax.ShapeDtypeStruct((bsz, l, MIX_B), BF16),
        scratch_shapes=[pltpu.VMEM((GLA_DK, GLA_DV), F32)] * 2
                       + [pltpu.VMEM((l, GLA_DK), BF16)] * 4
                       + [pltpu.VMEM((GLA_DK, l), BF16)] * 2
                       + [pltpu.VMEM((n_chunks * GLA_DK, GLA_DK), F32)] * 2
                       + [pltpu.VMEM((n_chunks * GLA_DK, GLA_DV), BF16)] * 2,
        compiler_params=_params(("parallel", "arbitrary")),
        name="gla",
    )(p_lat, p_lat, p_lat, p_lat, lr_lat, p_ctx, p_ctx, lr_ctx, gate_w, gate_b, norm_g, *masks)


def _sgu_kernel(u_ref, v_ref, lg_ref, lb_ref, ws_ref, bs_ref, y_ref):
    tm = u_ref.shape[0]
    for cidx in range(tm // SGU_CHUNK):
        rs = pl.ds(cidx * SGU_CHUNK, SGU_CHUNK)
        for g in range(SGU_GROUPS):
            cs = pl.ds(g * SGU_GW, SGU_GW)
            v = v_ref[rs, cs].astype(F32)
            mu = jnp.mean(v, axis=-1, keepdims=True)
            vc = v - mu
            var = jnp.mean(vc * vc, axis=-1, keepdims=True)
            vn = vc * lax.rsqrt(var + EPS) * lg_ref[:, cs] + lb_ref[:, cs]
            s = _dot(ws_ref[g], vn.astype(BF16)) + bs_ref[g]
            y_ref[rs, cs] = (u_ref[rs, cs].astype(F32) * s).astype(y_ref.dtype)


def _sgu(p_lat2d, ln_g, ln_b, w_s, b_s, tm=512):
    m = p_lat2d.shape[0]
    return pl.pallas_call(
        _sgu_kernel,
        grid=(m // tm,),
        in_specs=[pl.BlockSpec((tm, MIX_A), lambda i: (i, COL_U // MIX_A)),
                  pl.BlockSpec((tm, MIX_A), lambda i: (i, COL_VS // MIX_A)),
                  pl.BlockSpec((1, MIX_A), lambda i: (0, 0)),
                  pl.BlockSpec((1, MIX_A), lambda i: (0, 0)),
                  pl.BlockSpec((SGU_GROUPS, SGU_CHUNK, SGU_CHUNK), lambda i: (0, 0, 0)),
                  pl.BlockSpec((SGU_GROUPS, SGU_CHUNK, 1), lambda i: (0, 0, 0))],
        out_specs=pl.BlockSpec((tm, MIX_A), lambda i: (i, 0)),
        out_shape=jax.ShapeDtypeStruct((m, MIX_A), BF16),
        compiler_params=_params(("parallel",)),
        name="sgu",
    )(p_lat2d, p_lat2d, ln_g, ln_b, w_s, b_s)


def _rms(x, g):
    return x * lax.rsqrt(jnp.mean(x * x, axis=-1, keepdims=True) + EPS) * g


def _outproj_kernel(ya_ref, yb_ref, wa_ref, wb_ref, x_ref, gt_ref, pg_ref, fg_ref, sc_ref, sh_ref,
                    h_ref, f_ref):
    y = _dot(ya_ref[...], wa_ref[...]) + _dot(yb_ref[...], wb_ref[...])
    h = x_ref[...] + gt_ref[0] * _rms(y, pg_ref[...])
    h_ref[...] = h
    f_ref[...] = (_rms(h, fg_ref[...]) * (1.0 + sc_ref[0]) + sh_ref[0]).astype(f_ref.dtype)


def _outproj(y_a, y_b, w_out, x2d, gate, post_g, ffn_g, scale_f, shift_f, rows_per_mod, tm=512):
    m, d = x2d.shape
    tiles_per_mod = rows_per_mod // tm
    mod_spec = pl.BlockSpec((1, 1, d), lambda i: (i // tiles_per_mod, 0, 0))
    vec_spec = pl.BlockSpec((1, d), lambda i: (0, 0))
    return pl.pallas_call(
        _outproj_kernel,
        grid=(m // tm,),
        in_specs=[pl.BlockSpec((tm, MIX_A), lambda i: (i, 0)),
                  pl.BlockSpec((tm, MIX_B), lambda i: (i, 0)),
                  pl.BlockSpec((MIX_A, d), lambda i: (0, 0)),
                  pl.BlockSpec((MIX_B, d), lambda i: (MIX_A // MIX_B, 0)),
                  pl.BlockSpec((tm, d), lambda i: (i, 0)),
                  mod_spec, vec_spec, vec_spec, mod_spec, mod_spec],
        out_specs=[pl.BlockSpec((tm, d), lambda i: (i, 0)),
                   pl.BlockSpec((tm, d), lambda i: (i, 0))],
        out_shape=[jax.ShapeDtypeStruct((m, d), F32),
                   jax.ShapeDtypeStruct((m, d), BF16)],
        compiler_params=_params(("parallel",)),
        name="outproj",
    )(y_a, y_b, w_out, w_out, x2d, gate, post_g, ffn_g, scale_f, shift_f)


def _ffn_up_kernel(f_ref, fp_ref, fn_ref, wa_ref, wv_ref, cw_ref, cb_ref, g_ref, fext_scr, *,
                   tiles_per_image):
    i = pl.program_id(0)
    j = pl.program_id(1)
    tm = f_ref.shape[0]
    hw = GRID_W

    @pl.when(j == 0)
    def _():
        top = (i % tiles_per_image) == 0
        bottom = (i % tiles_per_image) == tiles_per_image - 1
        fext_scr[pl.ds(0, hw), :] = jnp.where(top, jnp.zeros_like(fp_ref), fp_ref[...])
        fext_scr[pl.ds(hw, tm), :] = f_ref[...]
        fext_scr[pl.ds(hw + tm, hw), :] = jnp.where(bottom, jnp.zeros_like(fn_ref), fn_ref[...])

    a = _dot(fext_scr[...], wa_ref[...])
    val = _dot(f_ref[...], wv_ref[...])
    ext = tm + 2 * hw
    col = lax.broadcasted_iota(jnp.int32, (ext, 1), 0) % hw
    left = jnp.where(col > 0, pltpu.roll(a, 1, 0), 0.0)
    right = jnp.where(col < hw - 1, pltpu.roll(a, ext - 1, 0), 0.0)
    cw = cw_ref[...]
    acc = jnp.broadcast_to(cb_ref[...], val.shape)
    for dr in range(3):
        lo = dr * hw
        acc = acc + cw[3 * dr + 0:3 * dr + 1, :] * left[lo:lo + tm, :]
        acc = acc + cw[3 * dr + 1:3 * dr + 2, :] * a[lo:lo + tm, :]
        acc = acc + cw[3 * dr + 2:3 * dr + 3, :] * right[lo:lo + tm, :]
    g_ref[...] = (_gelu(acc) * val).astype(g_ref.dtype)


def _ffn_up(f2d, w_up, conv_w, conv_b, rows_per_image, tm=1024, tf=512):
    m, d = f2d.shape
    d_ff = w_up.shape[1] // 2
    assert rows_per_image % tm == 0 and tm % GRID_W == 0 and d_ff % tf == 0
    hb = tm // GRID_W
    n_halo = m // GRID_W
    return pl.pallas_call(
        functools.partial(_ffn_up_kernel, tiles_per_image=rows_per_image // tm),
        grid=(m // tm, d_ff // tf),
        in_specs=[pl.BlockSpec((tm, d), lambda i, j: (i, 0)),
                  pl.BlockSpec((GRID_W, d), lambda i, j: (jnp.maximum(i * hb - 1, 0), 0)),
                  pl.BlockSpec((GRID_W, d), lambda i, j: (jnp.minimum((i + 1) * hb, n_halo - 1), 0)),
                  pl.BlockSpec((d, tf), lambda i, j: (0, j)),
                  pl.BlockSpec((d, tf), lambda i, j: (0, d_ff // tf + j)),
                  pl.BlockSpec((9, tf), lambda i, j: (0, j)),
                  pl.BlockSpec((1, tf), lambda i, j: (0, j))],
        out_specs=pl.BlockSpec((tm, tf), lambda i, j: (i, j)),
        out_shape=jax.ShapeDtypeStruct((m, d_ff), BF16),
        scratch_shapes=[pltpu.VMEM((tm + 2 * GRID_W, d), BF16)],
        compiler_params=_params(("parallel", "arbitrary")),
        name="ffn_up",
    )(f2d, f2d, f2d, w_up, w_up, conv_w, conv_b)


def _ffn_down_kernel(g_ref, w_ref, h_ref, gt_ref, pg_ref, o_ref, acc_scr):
    k = pl.program_id(1)
    part = _dot(g_ref[...], w_ref[...])

    @pl.when(k == 0)
    def _():
        acc_scr[...] = part

    @pl.when(k > 0)
    def _():
        acc_scr[...] += part

    @pl.when(k == pl.num_programs(1) - 1)
    def _():
        o_ref[...] = h_ref[...] + gt_ref[0] * _rms(acc_scr[...], pg_ref[...])


def _ffn_down(g2d, w_down, h2d, gate, post_g, rows_per_mod, tm=512, tk=1408):
    m, d = h2d.shape
    d_ff = g2d.shape[1]
    assert d_ff % tk == 0 and m % tm == 0
    tiles_per_mod = rows_per_mod // tm
    return pl.pallas_call(
        _ffn_down_kernel,
        grid=(m // tm, d_ff // tk),
        in_specs=[pl.BlockSpec((tm, tk), lambda i, k: (i, k)),
                  pl.BlockSpec((tk, d), lambda i, k: (k, 0)),
                  pl.BlockSpec((tm, d), lambda i, k: (i, 0)),
                  pl.BlockSpec((1, 1, d), lambda i, k: (i // tiles_per_mod, 0, 0)),
                  pl.BlockSpec((1, d), lambda i, k: (0, 0))],
        out_specs=pl.BlockSpec((tm, d), lambda i, k: (i, 0)),
        out_shape=jax.ShapeDtypeStruct((m, d), F32),
        scratch_shapes=[pltpu.VMEM((tm, d), F32)],
        compiler_params=_params(("parallel", "arbitrary")),
        name="ffn_down",
    )(g2d, w_down, h2d, gate, post_g)


def _gate_operands(w_f, b_f, w_b, b_b):
    r, kdim = w_f.shape
    wp = jnp.zeros((LANES, GLA_HEADS, 2, GLA_DK), BF16)
    wp = wp.at[:r, :, 0].set(w_f.astype(BF16).reshape(r, GLA_HEADS, GLA_DK))
    wp = wp.at[r:2 * r, :, 1].set(w_b.astype(BF16).reshape(r, GLA_HEADS, GLA_DK))
    bias = jnp.stack([b_f.reshape(GLA_HEADS, GLA_DK), b_b.reshape(GLA_HEADS, GLA_DK)], axis=1)
    return wp.reshape(LANES, 2 * kdim), bias.reshape(1, 2 * kdim)


def kernel(x, c, ctx, c_ctx, ada_w, ada_b, pre_mix_g, post_mix_g, pre_ffn_g, post_ffn_g, w_in, sgu_ln_g, sgu_ln_b, sgu_w, sgu_b, gla_gate_w_f, gla_gate_b_f, gla_gate_w_b, gla_gate_b_b, gla_norm_g, w_out, ffn_w_up, ffn_conv_w, ffn_conv_b, ffn_w_down):
    bsz, l, d = x.shape
    lc = ctx.shape[1]
    assert ada_w.shape[0] == 1, "single-layer kernel"
    d_ff = ffn_w_down.shape[1]

    c_rows = jnp.zeros((8, d), F32).at[:bsz].set(c).at[bsz].set(c_ctx)
    mod = _modulation(c_rows, ada_w[0], ada_b)
    sh_m, sc_m, gt_m, sh_f, sc_f, gt_f = [mod[:bsz, t * d:(t + 1) * d].reshape(bsz, 1, d) for t in range(N_MOD)]
    csh_m = mod[bsz:bsz + 1, 0:d].reshape(1, 1, d)
    csc_m = mod[bsz:bsz + 1, d:2 * d].reshape(1, 1, d)

    w_in_bf = w_in[0].astype(BF16)
    w_main = w_in_bf[:, :N_MAIN]
    w_lr = jnp.pad(w_in_bf[:, COL_LR:], ((0, 0), (0, LANES - 2 * GLA_LOWRANK)))
    w_ctx = w_in_bf[:, COL_K:COL_R]

    p_ctx, lr_ctx = _inproj(ctx.reshape(bsz * lc, d), csh_m, csc_m, pre_mix_g, w_ctx, w_lr,
                            ("none",) * 3, tm=lc, tn=512, rows_per_mod=bsz * lc)
    x2d = x.reshape(bsz * l, d)
    p_lat, lr_lat = _inproj(x2d, sh_m, sc_m, pre_mix_g, w_main, w_lr,
                            ("gelu", "gelu", "none", "none", "silu"), tm=1024, tn=1024, rows_per_mod=l)

    y_b = _gla(p_lat.reshape(bsz, l, N_MAIN), lr_lat.reshape(bsz, l, LANES),
               p_ctx.reshape(bsz, lc, -1), lr_ctx.reshape(bsz, lc, LANES),
               *_gate_operands(gla_gate_w_f[0], gla_gate_b_f[0], gla_gate_w_b[0], gla_gate_b_b[0]), gla_norm_g)
    y_a = _sgu(p_lat, sgu_ln_g, sgu_ln_b, sgu_w[0].astype(BF16), sgu_b[0][:, :, None])

    h, f = _outproj(y_a, y_b.reshape(bsz * l, MIX_B), w_out[0].astype(BF16), x2d, gt_m,
                    post_mix_g, pre_ffn_g, sc_f, sh_f, rows_per_mod=l)
    g = _ffn_up(f, ffn_w_up[0].astype(BF16), ffn_conv_w[0].reshape(9, d_ff), ffn_conv_b, rows_per_image=l)
    out = _ffn_down(g, ffn_w_down[0].astype(BF16), h, gt_f, post_ffn_g, rows_per_mod=l)
    return out.reshape(bsz, l, d)
```

```python
import functools

import jax
import jax.numpy as jnp
from jax import lax
from jax.experimental import pallas as pl
from jax.experimental.pallas import tpu as pltpu

F32 = jnp.float32
BF16 = jnp.bfloat16

EPS = 1e-6
GRID_W = 64
MIX_A = 1024
MIX_B = 1024
SGU_GROUPS = 4
SGU_CHUNK = 128
SGU_GW = MIX_A // SGU_GROUPS
GLA_HEADS = 4
GLA_DK = 128
GLA_DV = 256
GLA_KDIM = GLA_HEADS * GLA_DK
GLA_LOWRANK = 16
GLA_NORMALIZER = 16.0
GLA_CHUNK = 64
N_MOD = 6
LANES = 128
VMEM_LIMIT = 56 * 1024 * 1024
FFN_DOWN_VMEM_LIMIT = 60 * 1024 * 1024

COL_U, COL_VS, COL_Q, COL_K, COL_V, COL_R, COL_LR = 0, 1024, 2048, 2560, 3072, 4096, 5120
N_MAIN = COL_LR


def _dot(a, b):
    return jnp.dot(a, b, preferred_element_type=F32)


def _silu(x):
    return x / (1.0 + jnp.exp(-x))


def _gelu(x):
    return jax.nn.gelu(x, approximate=True)


_GELU_C = 0.7978845608028654


def _params(sem):
    return pltpu.CompilerParams(dimension_semantics=sem, vmem_limit_bytes=VMEM_LIMIT)


def _mod_kernel(c_ref, w_ref, b_ref, o_ref):
    s = _silu(c_ref[...]).astype(BF16)
    o_ref[...] = _dot(s, w_ref[...].astype(BF16)) + b_ref[...]


def _modulation(c_rows, ada_w, ada_b, tn=1024):
    rows, d = c_rows.shape
    n = ada_w.shape[1]
    return pl.pallas_call(
        _mod_kernel,
        grid=(n // tn,),
        in_specs=[pl.BlockSpec((rows, d), lambda j: (0, 0)),
                  pl.BlockSpec((d, tn), lambda j: (0, j)),
                  pl.BlockSpec((1, tn), lambda j: (0, j))],
        out_specs=pl.BlockSpec((rows, tn), lambda j: (0, j)),
        out_shape=jax.ShapeDtypeStruct((rows, n), F32),
        compiler_params=_params(("arbitrary",)),
        name="mod",
    )(c_rows, ada_w, ada_b)


def _inproj_kernel(x_ref, sh_ref, sc_ref, g_ref, w_ref, wlr_ref, p_ref, lr_ref, a_scr, *, acts):
    j = pl.program_id(1)

    @pl.when(j == 0)
    def _():
        x = x_ref[...]
        y = x * lax.rsqrt(jnp.mean(x * x, axis=-1, keepdims=True) + EPS) * g_ref[...]
        a = (y * (1.0 + sc_ref[0]) + sh_ref[0]).astype(BF16)
        a_scr[...] = a
        lr_ref[...] = _dot(a, wlr_ref[...])

    acc = _dot(a_scr[...], w_ref[...])
    if any(a != "none" for a in acts):
        coef = {"gelu": (2.0 * _GELU_C, 2.0 * _GELU_C * 0.044715), "silu": (1.0, 0.0), "none": (0.0, 0.0)}
        alpha = sum(jnp.where(j == t, coef[a][0], 0.0) for t, a in enumerate(acts))
        beta = sum(jnp.where(j == t, coef[a][1], 0.0) for t, a in enumerate(acts))
        gate = 1.0 / (1.0 + jnp.exp(-((alpha + beta * (acc * acc)) * acc)))
        plain = [j == t for t, a in enumerate(acts) if a == "none"]
        if plain:
            gate = jnp.where(functools.reduce(jnp.logical_or, plain), 1.0, gate)
        acc = acc * gate
    p_ref[...] = acc.astype(p_ref.dtype)


def _inproj(x2d, shift, scale, gain, w, w_lr, acts, tm, tn, rows_per_mod):
    m, d = x2d.shape
    n = w.shape[1]
    assert m % tm == 0 and n % tn == 0 and len(acts) == n // tn and rows_per_mod % tm == 0
    tiles_per_mod = rows_per_mod // tm
    return pl.pallas_call(
        functools.partial(_inproj_kernel, acts=acts),
        grid=(m // tm, n // tn),
        in_specs=[pl.BlockSpec((tm, d), lambda i, j: (i, 0)),
                  pl.BlockSpec((1, 1, d), lambda i, j: (i // tiles_per_mod, 0, 0)),
                  pl.BlockSpec((1, 1, d), lambda i, j: (i // tiles_per_mod, 0, 0)),
                  pl.BlockSpec((1, d), lambda i, j: (0, 0)),
                  pl.BlockSpec((d, tn), lambda i, j: (0, j)),
                  pl.BlockSpec((d, LANES), lambda i, j: (0, 0))],
        out_specs=[pl.BlockSpec((tm, tn), lambda i, j: (i, j)),
                   pl.BlockSpec((tm, LANES), lambda i, j: (i, 0))],
        out_shape=[jax.ShapeDtypeStruct((m, n), BF16),
                   jax.ShapeDtypeStruct((m, LANES), F32)],
        scratch_shapes=[pltpu.VMEM((tm, d), BF16)],
        compiler_params=_params(("parallel", "arbitrary")),
        name="inproj",
    )(x2d, shift, scale, gain, w, w_lr)


GLA_BLOCK = 4 * GLA_CHUNK
GLA_PAIR = 2 * GLA_CHUNK


def _log_decay(z):
    log_sig = -(jnp.maximum(-z, 0.0) + jnp.log(1.0 + jnp.exp(-jnp.abs(z))))
    return log_sig * (1.0 / GLA_NORMALIZER)


def _split(x):
    hi = x.astype(BF16)
    return hi, (x - hi.astype(F32)).astype(BF16)


def _nt_dot(a, b):
    return lax.dot_general(a, b, (((1,), (1,)), ((), ())), preferred_element_type=F32)


def _gla_masks():
    n, c = GLA_BLOCK, GLA_CHUNK
    r = jnp.arange(n)[:, None]
    s = jnp.arange(n)[None, :]
    same = (r // c) == (s // c)
    return ((same & (s <= r)).astype(BF16), (same & (s >= r)).astype(BF16), jnp.eye(GLA_DK, dtype=BF16))


def _chunk_rows(blk, cidx):
    return pl.ds(pl.multiple_of((blk * (GLA_BLOCK // GLA_CHUNK) + cidx) * GLA_DK, GLA_DK), GLA_DK)


def _gla_gates(rs, blk, q_ref, k_ref, lr_ref, w_ref, b_ref, lower_ref, eye_ref, fwd, bwd):
    c, nb, dk = GLA_CHUNK, GLA_BLOCK, GLA_DK
    lr = lr_ref[rs, :].astype(BF16)
    g = _log_decay(_dot(lr, w_ref[...]) + b_ref[...])
    g_hi, g_lo = _split(g)
    p = _dot(lower_ref[...], g_hi) + _dot(lower_ref[...], g_lo)
    last = [p[i * c + c - 1:(i + 1) * c, :] for i in range(nb // c)]
    tot = jnp.concatenate([jnp.broadcast_to(t, (c, 2 * dk)) for t in last], axis=0)
    k32 = k_ref[rs, :].astype(F32)
    q32 = None if q_ref is None else q_ref[rs, :].astype(F32) * (GLA_DK ** -0.5)
    plans = ((fwd, slice(0, dk), p, tot - p), (bwd, slice(dk, 2 * dk), tot - p + g, p - g))
    for d, lanes, b_cum, to_end in plans:
        k_d = (k32 * jnp.exp(to_end[:, lanes])).astype(BF16)
        d["kdt"][:, rs] = _nt_dot(eye_ref[...], k_d).astype(BF16)
        t_hi, t_lo = zip(*[_split(t[:, lanes]) for t in last])
        rows_hi = jnp.concatenate([jnp.broadcast_to(t, (dk, dk)) for t in t_hi], axis=0)
        rows_lo = jnp.concatenate([jnp.broadcast_to(t, (dk, dk)) for t in t_lo], axis=0)
        bl = _nt_dot(eye_ref[...], rows_hi) + _nt_dot(eye_ref[...], rows_lo)
        for cidx in range(nb // c):
            d["dec"][_chunk_rows(blk, cidx), :] = jnp.exp(bl[:, cidx * dk:(cidx + 1) * dk])
        if q32 is not None:
            d["qe"][rs, :] = (q32 * jnp.exp(b_cum[:, lanes])).astype(BF16)
            d["ke"][rs, :] = (k32 * jnp.exp(-b_cum[:, lanes])).astype(BF16)


def _gla_scan_block(rs, blk, order, d, v_ref, s_scr, record):
    c = GLA_CHUNK
    v = v_ref[rs, :]
    kd_t = d["kdt"][:, rs]
    zeros = jnp.zeros((c, v.shape[1]), v.dtype)
    s = s_scr[...]
    for cidx in order:
        pair, half = divmod(cidx, 2)
        slab = kd_t[:, pair * GLA_PAIR:(pair + 1) * GLA_PAIR]
        vc = v[cidx * c:(cidx + 1) * c]
        v_only = jnp.concatenate([vc, zeros] if half == 0 else [zeros, vc], axis=0)
        kv = _dot(slab, v_only)
        if record:
            d["snap"][_chunk_rows(blk, cidx), :] = s.astype(BF16)
        dec = d["dec"][_chunk_rows(blk, cidx), :]
        s = jnp.concatenate([dec, dec], axis=1) * s + kv
    s_scr[...] = s


def _gla_output(rs, blk, v_ref, r_ref, dirs, ng_ref, y_ref):
    c = GLA_CHUNK
    v = v_ref[rs, :]
    o = None
    for d in dirs:
        q_e = d["qe"][rs, :]
        att = jnp.where(d["tri"][...] > 0, _nt_dot(q_e, d["ke"][rs, :]), 0.0).astype(BF16)
        inter = [_dot(q_e[cidx * c:(cidx + 1) * c], d["snap"][_chunk_rows(blk, cidx), :])
                 for cidx in range(GLA_BLOCK // c)]
        od = _dot(att, v) + jnp.concatenate(inter, axis=0)
        o = od if o is None else o + od
    o = o * lax.rsqrt(jnp.mean(o * o, axis=-1, keepdims=True) + EPS) * ng_ref[...]
    y_ref[rs, :] = (o * r_ref[rs, :].astype(F32)).astype(y_ref.dtype)


def _gla_kernel(q_ref, k_ref, v_ref, r_ref, lr_ref, kc_ref, vc_ref, lrc_ref, w_ref, b_ref, ng_ref,
                lower_ref, upper_ref, eye_ref,
                y_ref, sf_scr, sb_scr, qef_scr, qeb_scr, kef_scr, keb_scr, kdtf_scr, kdtb_scr,
                decf_scr, decb_scr, snapf_scr, snapb_scr):
    fwd = dict(tri=lower_ref, qe=qef_scr, ke=kef_scr, kdt=kdtf_scr, dec=decf_scr, snap=snapf_scr)
    bwd = dict(tri=upper_ref, qe=qeb_scr, ke=keb_scr, kdt=kdtb_scr, dec=decb_scr, snap=snapb_scr)
    dirs = (fwd, bwd)
    l, lc = q_ref.shape[0], kc_ref.shape[0]
    n_blk, n_cblk = l // GLA_BLOCK, lc // GLA_BLOCK
    asc = tuple(range(GLA_BLOCK // GLA_CHUNK))
    desc = asc[::-1]

    def block_rows(j):
        return pl.ds(pl.multiple_of(j * GLA_BLOCK, GLA_BLOCK), GLA_BLOCK)

    sf_scr[...] = jnp.zeros_like(sf_scr)
    sb_scr[...] = jnp.zeros_like(sb_scr)

    for j in range(n_cblk):
        _gla_gates(block_rows(j), j, None, kc_ref, lrc_ref, w_ref, b_ref, lower_ref, eye_ref, fwd, bwd)
    for j in range(n_cblk):
        _gla_scan_block(block_rows(j), j, asc, fwd, vc_ref, sf_scr, False)
        jb = n_cblk - 1 - j
        _gla_scan_block(block_rows(jb), jb, desc, bwd, vc_ref, sb_scr, False)

    def gates(j, carry):
        _gla_gates(block_rows(j), j, q_ref, k_ref, lr_ref, w_ref, b_ref, lower_ref, eye_ref, fwd, bwd)
        return carry

    def scan(j, carry):
        _gla_scan_block(block_rows(j), j, asc, fwd, v_ref, sf_scr, True)
        jb = n_blk - 1 - j
        _gla_scan_block(block_rows(jb), jb, desc, bwd, v_ref, sb_scr, True)
        return carry

    def output(j, carry):
        _gla_output(block_rows(j), j, v_ref, r_ref, dirs, ng_ref, y_ref)
        return carry

    lax.fori_loop(0, n_blk, gates, 0)
    lax.fori_loop(0, n_blk, scan, 0)
    lax.fori_loop(0, n_blk, output, 0)


def _gla(p_lat, lr_lat, p_ctx, lr_ctx, gate_w, gate_b, norm_g):
    bsz, l, _ = p_lat.shape
    lc = p_ctx.shape[1]
    assert l % GLA_BLOCK == 0 and lc % GLA_BLOCK == 0 and lc <= l
    qb, kb = COL_Q // GLA_DK, COL_K // GLA_DK
    vb, rb = COL_V // GLA_DV, COL_R // GLA_DV
    gate_specs = [pl.BlockSpec((LANES, 2 * GLA_DK), lambda b, h: (0, h)),
                  pl.BlockSpec((1, 2 * GLA_DK), lambda b, h: (0, h))]
    masks = _gla_masks()
    mask_specs = [pl.BlockSpec(m.shape, lambda b, h: (0, 0)) for m in masks]
    n_chunks = l // GLA_CHUNK
    return pl.pallas_call(
        _gla_kernel,
        grid=(bsz, GLA_HEADS),
        in_specs=[pl.BlockSpec((None, l, GLA_DK), lambda b, h: (b, 0, qb + h)),
                  pl.BlockSpec((None, l, GLA_DK), lambda b, h: (b, 0, kb + h)),
                  pl.BlockSpec((None, l, GLA_DV), lambda b, h: (b, 0, vb + h)),
                  pl.BlockSpec((None, l, GLA_DV), lambda b, h: (b, 0, rb + h)),
                  pl.BlockSpec((None, l, LANES), lambda b, h: (b, 0, 0)),
                  pl.BlockSpec((None, lc, GLA_DK), lambda b, h: (b, 0, h)),
                  pl.BlockSpec((None, lc, GLA_DV), lambda b, h: (b, 0, GLA_KDIM // GLA_DV + h)),
                  pl.BlockSpec((None, lc, LANES), lambda b, h: (b, 0, 0))]
                 + gate_specs
                 + [pl.BlockSpec((1, GLA_DV), lambda b, h: (0, 0))] + mask_specs,
        out_specs=pl.BlockSpec((None, l, GLA_DV), lambda b, h: (b, 0, h)),
        out_shape=jax.ShapeDtypeStruct((bsz, l, MIX_B), BF16),
        scratch_shapes=[pltpu.VMEM((GLA_DK, GLA_DV), F32)] * 2
                       + [pltpu.VMEM((l, GLA_DK), BF16)] * 4
                       + [pltpu.VMEM((GLA_DK, l), BF16)] * 2
                       + [pltpu.VMEM((n_chunks * GLA_DK, GLA_DK), F32)] * 2
                       + [pltpu.VMEM((n_chunks * GLA_DK, GLA_DV), BF16)] * 2,
        compiler_params=_params(("parallel", "arbitrary")),
        name="gla",
    )(p_lat, p_lat, p_lat, p_lat, lr_lat, p_ctx, p_ctx, lr_ctx, gate_w, gate_b, norm_g, *masks)


def _sgu_kernel(u_ref, v_ref, lg_ref, lb_ref, ws_ref, bs_ref, y_ref):
    tm = u_ref.shape[0]
    for cidx in range(tm // SGU_CHUNK):
        rs = pl.ds(cidx * SGU_CHUNK, SGU_CHUNK)
        for g in range(SGU_GROUPS):
            cs = pl.ds(g * SGU_GW, SGU_GW)
            v = v_ref[rs, cs].astype(F32)
            mu = jnp.mean(v, axis=-1, keepdims=True)
            vc = v - mu
            var = jnp.mean(vc * vc, axis=-1, keepdims=True)
            vn = vc * lax.rsqrt(var + EPS) * lg_ref[:, cs] + lb_ref[:, cs]
            s = _dot(ws_ref[g], vn.astype(BF16)) + bs_ref[g]
            y_ref[rs, cs] = (u_ref[rs, cs].astype(F32) * s).astype(y_ref.dtype)


def _sgu(p_lat2d, ln_g, ln_b, w_s, b_s, tm=512):
    m = p_lat2d.shape[0]
    return pl.pallas_call(
        _sgu_kernel,
        grid=(m // tm,),
        in_specs=[pl.BlockSpec((tm, MIX_A), lambda i: (i, COL_U // MIX_A)),
                  pl.BlockSpec((tm, MIX_A), lambda i: (i, COL_VS // MIX_A)),
                  pl.BlockSpec((1, MIX_A), lambda i: (0, 0)),
                  pl.BlockSpec((1, MIX_A), lambda i: (0, 0)),
                  pl.BlockSpec((SGU_GROUPS, SGU_CHUNK, SGU_CHUNK), lambda i: (0, 0, 0)),
                  pl.BlockSpec((SGU_GROUPS, SGU_CHUNK, 1), lambda i: (0, 0, 0))],
        out_specs=pl.BlockSpec((tm, MIX_A), lambda i: (i, 0)),
        out_shape=jax.ShapeDtypeStruct((m, MIX_A), BF16),
        compiler_params=_params(("parallel",)),
        name="sgu",
    )(p_lat2d, p_lat2d, ln_g, ln_b, w_s, b_s)


def _rms(x, g):
    return x * lax.rsqrt(jnp.mean(x * x, axis=-1, keepdims=True) + EPS) * g


def _outproj_kernel(ya_ref, yb_ref, wa_ref, wb_ref, x_ref, gt_ref, pg_ref, fg_ref, sc_ref, sh_ref,
                    h_ref, f_ref):
    y = _dot(ya_ref[...], wa_ref[...]) + _dot(yb_ref[...], wb_ref[...])
    h = x_ref[...] + gt_ref[0] * _rms(y, pg_ref[...])
    h_ref[...] = h
    f_ref[...] = (_rms(h, fg_ref[...]) * (1.0 + sc_ref[0]) + sh_ref[0]).astype(f_ref.dtype)


def _outproj(y_a, y_b, w_out, x2d, gate, post_g, ffn_g, scale_f, shift_f, rows_per_mod, tm=512):
    m, d = x2d.shape
    tiles_per_mod = rows_per_mod // tm
    mod_spec = pl.BlockSpec((1, 1, d), lambda i: (i // tiles_per_mod, 0, 0))
    vec_spec = pl.BlockSpec((1, d), lambda i: (0, 0))
    return pl.pallas_call(
        _outproj_kernel,
        grid=(m // tm,),
        in_specs=[pl.BlockSpec((tm, MIX_A), lambda i: (i, 0)),
                  pl.BlockSpec((tm, MIX_B), lambda i: (i, 0)),
                  pl.BlockSpec((MIX_A, d), lambda i: (0, 0)),
                  pl.BlockSpec((MIX_B, d), lambda i: (MIX_A // MIX_B, 0)),
                  pl.BlockSpec((tm, d), lambda i: (i, 0)),
                  mod_spec, vec_spec, vec_spec, mod_spec, mod_spec],
        out_specs=[pl.BlockSpec((tm, d), lambda i: (i, 0)),
                   pl.BlockSpec((tm, d), lambda i: (i, 0))],
        out_shape=[jax.ShapeDtypeStruct((m, d), F32),
                   jax.ShapeDtypeStruct((m, d), BF16)],
        compiler_params=_params(("parallel",)),
        name="outproj",
    )(y_a, y_b, w_out, w_out, x2d, gate, post_g, ffn_g, scale_f, shift_f)


def _ffn_up_kernel(f_ref, fp_ref, fn_ref, wa_ref, wv_ref, cw_ref, cb_ref, g_ref, fext_scr, *,
                   tiles_per_image):
    i = pl.program_id(0)
    j = pl.program_id(1)
    tm = f_ref.shape[0]
    hw = GRID_W

    @pl.when(j == 0)
    def _():
        top = (i % tiles_per_image) == 0
        bottom = (i % tiles_per_image) == tiles_per_image - 1
        fext_scr[pl.ds(0, hw), :] = jnp.where(top, jnp.zeros_like(fp_ref), fp_ref[...])
        fext_scr[pl.ds(hw, tm), :] = f_ref[...]
        fext_scr[pl.ds(hw + tm, hw), :] = jnp.where(bottom, jnp.zeros_like(fn_ref), fn_ref[...])

    a = _dot(fext_scr[...], wa_ref[...])
    val = _dot(f_ref[...], wv_ref[...])
    ext = tm + 2 * hw
    col = lax.broadcasted_iota(jnp.int32, (ext, 1), 0) % hw
    left = jnp.where(col > 0, pltpu.roll(a, 1, 0), 0.0)
    right = jnp.where(col < hw - 1, pltpu.roll(a, ext - 1, 0), 0.0)
    cw = cw_ref[...]
    acc = jnp.broadcast_to(cb_ref[...], val.shape)
    for dr in range(3):
        lo = dr * hw
        acc = acc + cw[3 * dr + 0:3 * dr + 1, :] * left[lo:lo + tm, :]
        acc = acc + cw[3 * dr + 1:3 * dr + 2, :] * a[lo:lo + tm, :]
        acc = acc + cw[3 * dr + 2:3 * dr + 3, :] * right[lo:lo + tm, :]
    g_ref[...] = (_gelu(acc) * val).astype(g_ref.dtype)


def _ffn_up(f2d, w_up, conv_w, conv_b, rows_per_image, tm=1024, tf=512):
    m, d = f2d.shape
    d_ff = w_up.shape[1] // 2
    assert rows_per_image % tm == 0 and tm % GRID_W == 0 and d_ff % tf == 0
    hb = tm // GRID_W
    n_halo = m // GRID_W
    return pl.pallas_call(
        functools.partial(_ffn_up_kernel, tiles_per_image=rows_per_image // tm),
        grid=(m // tm, d_ff // tf),
        in_specs=[pl.BlockSpec((tm, d), lambda i, j: (i, 0)),
                  pl.BlockSpec((GRID_W, d), lambda i, j: (jnp.maximum(i * hb - 1, 0), 0)),
                  pl.BlockSpec((GRID_W, d), lambda i, j: (jnp.minimum((i + 1) * hb, n_halo - 1), 0)),
                  pl.BlockSpec((d, tf), lambda i, j: (0, j)),
                  pl.BlockSpec((d, tf), lambda i, j: (0, d_ff // tf + j)),
                  pl.BlockSpec((9, tf), lambda i, j: (0, j)),
                  pl.BlockSpec((1, tf), lambda i, j: (0, j))],
        out_specs=pl.BlockSpec((tm, tf), lambda i, j: (i, j)),
        out_shape=jax.ShapeDtypeStruct((m, d_ff), BF16),
        scratch_shapes=[pltpu.VMEM((tm + 2 * GRID_W, d), BF16)],
        compiler_params=_params(("parallel", "arbitrary")),
        name="ffn_up",
    )(f2d, f2d, f2d, w_up, w_up, conv_w, conv_b)


def _ffn_down_kernel(g_ref, w_ref, h_ref, gt_ref, pg_ref, o_ref):
    y = _dot(g_ref[...], w_ref[...])
    o_ref[...] = h_ref[...] + gt_ref[0] * _rms(y, pg_ref[...])


def _ffn_down(g2d, w_down, h2d, gate, post_g, rows_per_mod, tm=512):
    m, d = h2d.shape
    d_ff = g2d.shape[1]
    assert m % tm == 0
    tiles_per_mod = rows_per_mod // tm
    return pl.pallas_call(
        _ffn_down_kernel,
        grid=(m // tm,),
        in_specs=[pl.BlockSpec((tm, d_ff), lambda i: (i, 0)),
                  pl.BlockSpec((d_ff, d), lambda i: (0, 0), pipeline_mode=pl.Buffered(1)),
                  pl.BlockSpec((tm, d), lambda i: (i, 0)),
                  pl.BlockSpec((1, 1, d), lambda i: (i // tiles_per_mod, 0, 0)),
                  pl.BlockSpec((1, d), lambda i: (0, 0))],
        out_specs=pl.BlockSpec((tm, d), lambda i: (i, 0)),
        out_shape=jax.ShapeDtypeStruct((m, d), F32),
        compiler_params=pltpu.CompilerParams(dimension_semantics=("arbitrary",),
                                             vmem_limit_bytes=FFN_DOWN_VMEM_LIMIT),
        name="ffn_down",
    )(g2d, w_down, h2d, gate, post_g)


def _gate_operands(w_f, b_f, w_b, b_b):
    r, kdim = w_f.shape
    wp = jnp.zeros((LANES, GLA_HEADS, 2, GLA_DK), BF16)
    wp = wp.at[:r, :, 0].set(w_f.astype(BF16).reshape(r, GLA_HEADS, GLA_DK))
    wp = wp.at[r:2 * r, :, 1].set(w_b.astype(BF16).reshape(r, GLA_HEADS, GLA_DK))
    bias = jnp.stack([b_f.reshape(GLA_HEADS, GLA_DK), b_b.reshape(GLA_HEADS, GLA_DK)], axis=1)
    return wp.reshape(LANES, 2 * kdim), bias.reshape(1, 2 * kdim)


def kernel(x, c, ctx, c_ctx, ada_w, ada_b, pre_mix_g, post_mix_g, pre_ffn_g, post_ffn_g, w_in, sgu_ln_g, sgu_ln_b, sgu_w, sgu_b, gla_gate_w_f, gla_gate_b_f, gla_gate_w_b, gla_gate_b_b, gla_norm_g, w_out, ffn_w_up, ffn_conv_w, ffn_conv_b, ffn_w_down):
    bsz, l, d = x.shape
    lc = ctx.shape[1]
    assert ada_w.shape[0] == 1, "single-layer kernel"
    d_ff = ffn_w_down.shape[1]

    c_rows = jnp.zeros((8, d), F32).at[:bsz].set(c).at[bsz].set(c_ctx)
    mod = _modulation(c_rows, ada_w[0], ada_b)
    sh_m, sc_m, gt_m, sh_f, sc_f, gt_f = [mod[:bsz, t * d:(t + 1) * d].reshape(bsz, 1, d) for t in range(N_MOD)]
    csh_m = mod[bsz:bsz + 1, 0:d].reshape(1, 1, d)
    csc_m = mod[bsz:bsz + 1, d:2 * d].reshape(1, 1, d)

    w_in_bf = w_in[0].astype(BF16)
    w_main = w_in_bf[:, :N_MAIN]
    w_lr = jnp.pad(w_in_bf[:, COL_LR:], ((0, 0), (0, LANES - 2 * GLA_LOWRANK)))
    w_ctx = w_in_bf[:, COL_K:COL_R]

    p_ctx, lr_ctx = _inproj(ctx.reshape(bsz * lc, d), csh_m, csc_m, pre_mix_g, w_ctx, w_lr,
                            ("none",) * 3, tm=lc, tn=512, rows_per_mod=bsz * lc)
    x2d = x.reshape(bsz * l, d)
    p_lat, lr_lat = _inproj(x2d, sh_m, sc_m, pre_mix_g, w_main, w_lr,
                            ("gelu", "gelu", "none", "none", "silu"), tm=1024, tn=1024, rows_per_mod=l)

    y_b = _gla(p_lat.reshape(bsz, l, N_MAIN), lr_lat.reshape(bsz, l, LANES),
               p_ctx.reshape(bsz, lc, -1), lr_ctx.reshape(bsz, lc, LANES),
               *_gate_operands(gla_gate_w_f[0], gla_gate_b_f[0], gla_gate_w_b[0], gla_gate_b_b[0]), gla_norm_g)
    y_a = _sgu(p_lat, sgu_ln_g, sgu_ln_b, sgu_w[0].astype(BF16), sgu_b[0][:, :, None])

    h, f = _outproj(y_a, y_b.reshape(bsz * l, MIX_B), w_out[0].astype(BF16), x2d, gt_m,
                    post_mix_g, pre_ffn_g, sc_f, sh_f, rows_per_mod=l)
    g = _ffn_up(f, ffn_w_up[0].astype(BF16), ffn_conv_w[0].reshape(9, d_ff), ffn_conv_b, rows_per_image=l)
    out = _ffn_down(g, ffn_w_down[0].astype(BF16), h, gt_f, post_ffn_g, rows_per_mod=l)
    return out.reshape(bsz, l, d)
```

```python
import functools

import jax
import jax.numpy as jnp
from jax import lax
from jax.experimental import pallas as pl
from jax.experimental.pallas import tpu as pltpu

F32 = jnp.float32
BF16 = jnp.bfloat16

EPS = 1e-6
GRID_W = 64
MIX_A = 1024
MIX_B = 1024
SGU_GROUPS = 4
SGU_CHUNK = 128
SGU_GW = MIX_A // SGU_GROUPS
GLA_HEADS = 4
GLA_DK = 128
GLA_DV = 256
GLA_KDIM = GLA_HEADS * GLA_DK
GLA_LOWRANK = 16
GLA_NORMALIZER = 16.0
GLA_CHUNK = 64
N_MOD = 6
LANES = 128
MXU_COLS = 256
VMEM_LIMIT = 56 * 1024 * 1024
FFN_DOWN_VMEM_LIMIT = 60 * 1024 * 1024

COL_U, COL_VS, COL_Q, COL_K, COL_V, COL_R, COL_LR = 0, 1024, 2048, 2560, 3072, 4096, 5120
N_MAIN = COL_LR


def _dot(a, b):
    return jnp.dot(a, b, preferred_element_type=F32)


def _nt_dot(a, b):
    return lax.dot_general(a, b, (((1,), (1,)), ((), ())), preferred_element_type=F32)


def _silu(x):
    return x / (1.0 + jnp.exp(-x))


def _gelu(x):
    return jax.nn.gelu(x, approximate=True)


def _params(sem):
    return pltpu.CompilerParams(dimension_semantics=sem, vmem_limit_bytes=VMEM_LIMIT)


def _mod_kernel(c_ref, w_ref, b_ref, o_ref):
    s = _silu(c_ref[...]).astype(BF16)
    o_ref[...] = _dot(s, w_ref[...].astype(BF16)) + b_ref[...]


def _modulation(c_rows, ada_w, ada_b, tn=1024):
    rows, d = c_rows.shape
    n = ada_w.shape[1]
    return pl.pallas_call(
        _mod_kernel,
        grid=(n // tn,),
        in_specs=[pl.BlockSpec((rows, d), lambda j: (0, 0)),
                  pl.BlockSpec((d, tn), lambda j: (0, j)),
                  pl.BlockSpec((1, tn), lambda j: (0, j))],
        out_specs=pl.BlockSpec((rows, tn), lambda j: (0, j)),
        out_shape=jax.ShapeDtypeStruct((rows, n), F32),
        compiler_params=_params(("arbitrary",)),
        name="mod",
    )(c_rows, ada_w, ada_b)


def _inproj_kernel(x_ref, sh_ref, sc_ref, g_ref, w_ref, wlr_ref, p_ref, lr_ref, a_scr, *, acts):
    j = pl.program_id(1)

    @pl.when(j == 0)
    def _():
        x = x_ref[...]
        y = x * lax.rsqrt(jnp.mean(x * x, axis=-1, keepdims=True) + EPS) * g_ref[...]
        a = (y * (1.0 + sc_ref[0]) + sh_ref[0]).astype(BF16)
        a_scr[...] = a
        lr_ref[...] = _nt_dot(a, wlr_ref[...])

    acc = _nt_dot(a_scr[...], w_ref[...])
    fns = {"gelu": _gelu, "silu": _silu, "none": lambda t: t}
    for name in sorted(set(acts)):
        idx = [t for t, a in enumerate(acts) if a == name]
        cond = functools.reduce(jnp.logical_or, [j == t for t in idx])

        @pl.when(cond)
        def _(name=name):
            p_ref[...] = fns[name](acc).astype(p_ref.dtype)


def _inproj(x2d, shift, scale, gain, w_t, w_lr_t, acts, tm, tn, first_col, rows_per_mod):
    m, d = x2d.shape
    n = len(acts) * tn
    assert m % tm == 0 and first_col % tn == 0 and rows_per_mod % tm == 0
    tiles_per_mod = rows_per_mod // tm
    blk0 = first_col // tn
    return pl.pallas_call(
        functools.partial(_inproj_kernel, acts=acts),
        grid=(m // tm, len(acts)),
        in_specs=[pl.BlockSpec((tm, d), lambda i, j: (i, 0)),
                  pl.BlockSpec((1, 1, d), lambda i, j: (i // tiles_per_mod, 0, 0)),
                  pl.BlockSpec((1, 1, d), lambda i, j: (i // tiles_per_mod, 0, 0)),
                  pl.BlockSpec((1, d), lambda i, j: (0, 0)),
                  pl.BlockSpec((tn, d), lambda i, j: (blk0 + j, 0)),
                  pl.BlockSpec((LANES, d), lambda i, j: (0, 0))],
        out_specs=[pl.BlockSpec((tm, tn), lambda i, j: (i, j)),
                   pl.BlockSpec((tm, LANES), lambda i, j: (i, 0))],
        out_shape=[jax.ShapeDtypeStruct((m, n), BF16),
                   jax.ShapeDtypeStruct((m, LANES), F32)],
        scratch_shapes=[pltpu.VMEM((tm, d), BF16)],
        compiler_params=_params(("parallel", "arbitrary")),
        name="inproj",
    )(x2d, shift, scale, gain, w_t, w_lr_t)


GLA_BLOCK = 4 * GLA_CHUNK
GLA_PAIR = 2 * GLA_CHUNK


def _log_decay(z):
    log_sig = -(jnp.maximum(-z, 0.0) + jnp.log(1.0 + jnp.exp(-jnp.abs(z))))
    return log_sig * (1.0 / GLA_NORMALIZER)


def _split(x):
    hi = x.astype(BF16)
    return hi, (x - hi.astype(F32)).astype(BF16)


def _gla_masks():
    n, c = GLA_BLOCK, GLA_CHUNK
    r = jnp.arange(n)[:, None]
    s = jnp.arange(n)[None, :]
    same = (r // c) == (s // c)
    return ((same & (s <= r)).astype(BF16), (same & (s >= r)).astype(BF16), jnp.eye(GLA_DV, dtype=BF16))


SUBLANES = 8


def _chunk_rows(blk, cidx, rows):
    return pl.ds(pl.multiple_of((blk * (GLA_BLOCK // GLA_CHUNK) + cidx) * rows, rows), rows)


def _gla_gates(rs, blk, q_ref, k_ref, v_ref, lr_ref, w_ref, b_ref, lower_ref, eye_ref, vt_scr, fwd, bwd):
    c, nb, dk = GLA_CHUNK, GLA_BLOCK, GLA_DK
    lr = lr_ref[rs, :].astype(BF16)
    g = _log_decay(_dot(lr, w_ref[...]) + b_ref[...])
    g_hi, g_lo = _split(g)
    p = _dot(lower_ref[...], g_hi) + _dot(lower_ref[...], g_lo)
    last = [p[i * c + c - 1:(i + 1) * c, :] for i in range(nb // c)]
    tot = jnp.concatenate([jnp.broadcast_to(t, (c, 2 * dk)) for t in last], axis=0)
    vt_scr[:, rs] = _nt_dot(eye_ref[...], v_ref[rs, :]).astype(BF16)
    k32 = k_ref[rs, :].astype(F32)
    q32 = None if q_ref is None else q_ref[rs, :].astype(F32) * (GLA_DK ** -0.5)
    plans = ((fwd, slice(0, dk), p, tot - p), (bwd, slice(dk, 2 * dk), tot - p + g, p - g))
    for d, lanes, b_cum, to_end in plans:
        d["kd"][rs, :] = (k32 * jnp.exp(to_end[:, lanes])).astype(BF16)
        for cidx in range(nb // c):
            dec = jnp.exp(last[cidx][:, lanes])
            d["dec"][_chunk_rows(blk, cidx, SUBLANES), :] = jnp.broadcast_to(dec, (SUBLANES, dk))
        if q32 is not None:
            d["qe"][rs, :] = (q32 * jnp.exp(b_cum[:, lanes])).astype(BF16)
            d["ke"][rs, :] = (k32 * jnp.exp(-b_cum[:, lanes])).astype(BF16)


def _gla_scan_block(rs, blk, order, d, vt_scr, s_scr, record):
    c = GLA_CHUNK
    v_t = vt_scr[:, rs]
    kd = d["kd"][rs, :]
    zeros = jnp.zeros((c, kd.shape[1]), kd.dtype)
    s = s_scr[...]
    for cidx in order:
        pair, half = divmod(cidx, 2)
        slab = v_t[:, pair * GLA_PAIR:(pair + 1) * GLA_PAIR]
        kc = kd[cidx * c:(cidx + 1) * c]
        k_only = jnp.concatenate([kc, zeros] if half == 0 else [zeros, kc], axis=0)
        kv_t = _dot(slab, k_only)
        if record:
            d["snap"][_chunk_rows(blk, cidx, GLA_DV), :] = s.astype(BF16)
        dec = d["dec"][_chunk_rows(blk, cidx, SUBLANES), :]
        s = s * dec[0:1, :] + kv_t
    s_scr[...] = s


def _gla_output(rs, blk, v_ref, r_ref, dirs, ng_ref, y_ref):
    c = GLA_CHUNK
    v = v_ref[rs, :]
    o = None
    for d in dirs:
        q_e = d["qe"][rs, :]
        att = jnp.where(d["tri"][...] > 0, _nt_dot(q_e, d["ke"][rs, :]), 0.0).astype(BF16)
        inter = [_nt_dot(q_e[cidx * c:(cidx + 1) * c], d["snap"][_chunk_rows(blk, cidx, GLA_DV), :])
                 for cidx in range(GLA_BLOCK // c)]
        od = _dot(att, v) + jnp.concatenate(inter, axis=0)
        o = od if o is None else o + od
    o = o * lax.rsqrt(jnp.mean(o * o, axis=-1, keepdims=True) + EPS) * ng_ref[...]
    y_ref[rs, :] = (o * r_ref[rs, :].astype(F32)).astype(y_ref.dtype)


def _gla_kernel(q_ref, k_ref, v_ref, r_ref, lr_ref, kc_ref, vc_ref, lrc_ref, w_ref, b_ref, ng_ref,
                lower_ref, upper_ref, eye_ref,
                y_ref, sf_scr, sb_scr, vt_scr, qef_scr, qeb_scr, kef_scr, keb_scr, kdf_scr, kdb_scr,
                decf_scr, decb_scr, snapf_scr, snapb_scr):
    fwd = dict(tri=lower_ref, qe=qef_scr, ke=kef_scr, kd=kdf_scr, dec=decf_scr, snap=snapf_scr)
    bwd = dict(tri=upper_ref, qe=qeb_scr, ke=keb_scr, kd=kdb_scr, dec=decb_scr, snap=snapb_scr)
    dirs = (fwd, bwd)
    l, lc = q_ref.shape[0], kc_ref.shape[0]
    n_blk, n_cblk = l // GLA_BLOCK, lc // GLA_BLOCK
    asc = tuple(range(GLA_BLOCK // GLA_CHUNK))
    desc = asc[::-1]

    def block_rows(j):
        return pl.ds(pl.multiple_of(j * GLA_BLOCK, GLA_BLOCK), GLA_BLOCK)

    sf_scr[...] = jnp.zeros_like(sf_scr)
    sb_scr[...] = jnp.zeros_like(sb_scr)

    for j in range(n_cblk):
        _gla_gates(block_rows(j), j, None, kc_ref, vc_ref, lrc_ref, w_ref, b_ref, lower_ref, eye_ref,
                   vt_scr, fwd, bwd)
    for j in range(n_cblk):
        _gla_scan_block(block_rows(j), j, asc, fwd, vt_scr, sf_scr, False)
        jb = n_cblk - 1 - j
        _gla_scan_block(block_rows(jb), jb, desc, bwd, vt_scr, sb_scr, False)

    def gates(j, carry):
        _gla_gates(block_rows(j), j, q_ref, k_ref, v_ref, lr_ref, w_ref, b_ref, lower_ref, eye_ref,
                   vt_scr, fwd, bwd)
        return carry

    def scan(j, carry):
        _gla_scan_block(block_rows(j), j, asc, fwd, vt_scr, sf_scr, True)
        jb = n_blk - 1 - j
        _gla_scan_block(block_rows(jb), jb, desc, bwd, vt_scr, sb_scr, True)
        return carry

    def output(j, carry):
        _gla_output(block_rows(j), j, v_ref, r_ref, dirs, ng_ref, y_ref)
        return carry

    lax.fori_loop(0, n_blk, gates, 0)
    lax.fori_loop(0, n_blk, scan, 0)
    lax.fori_loop(0, n_blk, output, 0, unroll=2)


def _gla(p_lat, lr_lat, p_ctx, lr_ctx, gate_w, gate_b, norm_g):
    bsz, l, _ = p_lat.shape
    lc = p_ctx.shape[1]
    assert l % GLA_BLOCK == 0 and lc % GLA_BLOCK == 0 and lc <= l
    qb, kb = COL_Q // GLA_DK, COL_K // GLA_DK
    vb, rb = COL_V // GLA_DV, COL_R // GLA_DV
    gate_specs = [pl.BlockSpec((LANES, 2 * GLA_DK), lambda b, h: (0, h)),
                  pl.BlockSpec((1, 2 * GLA_DK), lambda b, h: (0, h))]
    masks = _gla_masks()
    mask_specs = [pl.BlockSpec(m.shape, lambda b, h: (0, 0)) for m in masks]
    n_chunks = l // GLA_CHUNK
    return pl.pallas_call(
        _gla_kernel,
        grid=(bsz, GLA_HEADS),
        in_specs=[pl.BlockSpec((None, l, GLA_DK), lambda b, h: (b, 0, qb + h)),
                  pl.BlockSpec((None, l, GLA_DK), lambda b, h: (b, 0, kb + h)),
                  pl.BlockSpec((None, l, GLA_DV), lambda b, h: (b, 0, vb + h)),
                  pl.BlockSpec((None, l, GLA_DV), lambda b, h: (b, 0, rb + h)),
                  pl.BlockSpec((None, l, LANES), lambda b, h: (b, 0, 0)),
                  pl.BlockSpec((None, lc, GLA_DK), lambda b, h: (b, 0, h)),
                  pl.BlockSpec((None, lc, GLA_DV), lambda b, h: (b, 0, GLA_KDIM // GLA_DV + h)),
                  pl.BlockSpec((None, lc, LANES), lambda b, h: (b, 0, 0))]
                 + gate_specs
                 + [pl.BlockSpec((1, GLA_DV), lambda b, h: (0, 0))] + mask_specs,
        out_specs=pl.BlockSpec((None, l, GLA_DV), lambda b, h: (b, 0, h)),
        out_shape=jax.ShapeDtypeStruct((bsz, l, MIX_B), BF16),
        scratch_shapes=[pltpu.VMEM((GLA_DV, GLA_DK), F32)] * 2
                       + [pltpu.VMEM((GLA_DV, l), BF16)]
                       + [pltpu.VMEM((l, GLA_DK), BF16)] * 6
                       + [pltpu.VMEM((n_chunks * SUBLANES, GLA_DK), F32)] * 2
                       + [pltpu.VMEM((n_chunks * GLA_DV, GLA_DK), BF16)] * 2,
        compiler_params=_params(("parallel", "arbitrary")),
        name="gla",
    )(p_lat, p_lat, p_lat, p_lat, lr_lat, p_ctx, p_ctx, lr_ctx, gate_w, gate_b, norm_g, *masks)


def _cast_specs(weights, n_steps):
    ins, outs, shapes = [], [], []
    for w in weights:
        rows, cols = w.shape
        assert rows % (n_steps * 16) == 0, "slabs must be whole packed bf16 row tiles"
        spec = pl.BlockSpec((rows // n_steps, cols), lambda i: (i, 0))
        ins.append(spec)
        outs.append(spec)
        shapes.append(jax.ShapeDtypeStruct(w.shape, BF16))
    return ins, outs, shapes


def _cast_slabs(src_refs, dst_refs):
    for src, dst in zip(src_refs, dst_refs):
        dst[...] = src[...].astype(dst.dtype)


def _sgu_kernel(u_ref, v_ref, lg_ref, lb_ref, ws_ref, bs_ref, *rest):
    n_cast = (len(rest) - 1) // 2
    y_ref = rest[n_cast]
    _cast_slabs(rest[:n_cast], rest[n_cast + 1:])
    tm = u_ref.shape[0]
    for cidx in range(tm // SGU_CHUNK):
        rs = pl.ds(cidx * SGU_CHUNK, SGU_CHUNK)
        for g in range(SGU_GROUPS):
            cs = pl.ds(g * SGU_GW, SGU_GW)
            v = v_ref[rs, cs].astype(F32)
            mu = jnp.mean(v, axis=-1, keepdims=True)
            vc = v - mu
            var = jnp.mean(vc * vc, axis=-1, keepdims=True)
            vn = vc * lax.rsqrt(var + EPS) * lg_ref[:, cs] + lb_ref[:, cs]
            s = _dot(ws_ref[g], vn.astype(BF16)) + bs_ref[g]
            y_ref[rs, cs] = (u_ref[rs, cs].astype(F32) * s).astype(y_ref.dtype)


def _sgu(p_lat2d, ln_g, ln_b, w_s, b_s, cast_weights, tm=512):
    m = p_lat2d.shape[0]
    cast_in, cast_out, cast_shapes = _cast_specs(cast_weights, m // tm)
    return pl.pallas_call(
        _sgu_kernel,
        grid=(m // tm,),
        in_specs=[pl.BlockSpec((tm, MIX_A), lambda i: (i, COL_U // MIX_A)),
                  pl.BlockSpec((tm, MIX_A), lambda i: (i, COL_VS // MIX_A)),
                  pl.BlockSpec((1, MIX_A), lambda i: (0, 0)),
                  pl.BlockSpec((1, MIX_A), lambda i: (0, 0)),
                  pl.BlockSpec((SGU_GROUPS, SGU_CHUNK, SGU_CHUNK), lambda i: (0, 0, 0)),
                  pl.BlockSpec((SGU_GROUPS, SGU_CHUNK, 1), lambda i: (0, 0, 0))] + cast_in,
        out_specs=[pl.BlockSpec((tm, MIX_A), lambda i: (i, 0))] + cast_out,
        out_shape=[jax.ShapeDtypeStruct((m, MIX_A), BF16)] + cast_shapes,
        compiler_params=_params(("parallel",)),
        name="sgu",
    )(p_lat2d, p_lat2d, ln_g, ln_b, w_s, b_s, *cast_weights)


def _rms(x, g):
    return x * lax.rsqrt(jnp.mean(x * x, axis=-1, keepdims=True) + EPS) * g


def _outproj_kernel(ya_ref, yb_ref, wa_ref, wb_ref, x_ref, gt_ref, pg_ref, fg_ref, sc_ref, sh_ref, *rest):
    n_cast = (len(rest) - 2) // 2
    h_ref, f_ref = rest[n_cast:n_cast + 2]
    _cast_slabs(rest[:n_cast], rest[n_cast + 2:])
    y = _dot(ya_ref[...], wa_ref[...]) + _dot(yb_ref[...], wb_ref[...])
    h = x_ref[...] + gt_ref[0] * _rms(y, pg_ref[...])
    h_ref[...] = h
    f_ref[...] = (_rms(h, fg_ref[...]) * (1.0 + sc_ref[0]) + sh_ref[0]).astype(f_ref.dtype)


def _outproj(y_a, y_b, w_out, x2d, gate, post_g, ffn_g, scale_f, shift_f, cast_weights, rows_per_mod, tm=512):
    m, d = x2d.shape
    tiles_per_mod = rows_per_mod // tm
    mod_spec = pl.BlockSpec((1, 1, d), lambda i: (i // tiles_per_mod, 0, 0))
    vec_spec = pl.BlockSpec((1, d), lambda i: (0, 0))
    cast_in, cast_out, cast_shapes = _cast_specs(cast_weights, m // tm)
    return pl.pallas_call(
        _outproj_kernel,
        grid=(m // tm,),
        in_specs=[pl.BlockSpec((tm, MIX_A), lambda i: (i, 0)),
                  pl.BlockSpec((tm, MIX_B), lambda i: (i, 0)),
                  pl.BlockSpec((MIX_A, d), lambda i: (0, 0), pipeline_mode=pl.Buffered(1)),
                  pl.BlockSpec((MIX_B, d), lambda i: (MIX_A // MIX_B, 0), pipeline_mode=pl.Buffered(1)),
                  pl.BlockSpec((tm, d), lambda i: (i, 0)),
                  mod_spec, vec_spec, vec_spec, mod_spec, mod_spec] + cast_in,
        out_specs=[pl.BlockSpec((tm, d), lambda i: (i, 0)),
                   pl.BlockSpec((tm, d), lambda i: (i, 0))] + cast_out,
        out_shape=[jax.ShapeDtypeStruct((m, d), F32),
                   jax.ShapeDtypeStruct((m, d), BF16)] + cast_shapes,
        compiler_params=_params(("arbitrary",)),
        name="outproj",
    )(y_a, y_b, w_out, w_out, x2d, gate, post_g, ffn_g, scale_f, shift_f, *cast_weights)


def _conv_gelu(a, cw, cb):
    hw = GRID_W
    ext = a.shape[0]
    tm = ext - 2 * hw
    col = lax.broadcasted_iota(jnp.int32, (ext, 1), 0) % hw
    left = jnp.where(col > 0, pltpu.roll(a, 1, 0), 0.0)
    right = jnp.where(col < hw - 1, pltpu.roll(a, ext - 1, 0), 0.0)
    acc = jnp.broadcast_to(cb, (tm, a.shape[1]))
    for dr in range(3):
        lo = dr * hw
        acc = acc + cw[3 * dr + 0:3 * dr + 1, :] * left[lo:lo + tm, :]
        acc = acc + cw[3 * dr + 1:3 * dr + 2, :] * a[lo:lo + tm, :]
        acc = acc + cw[3 * dr + 2:3 * dr + 3, :] * right[lo:lo + tm, :]
    return _gelu(acc)


def _ffn_up_kernel(f_ref, fp_ref, fn_ref, wa_ref, wv_ref, cw_ref, cb_ref, g_ref, fext_scr, *,
                   tiles_per_image):
    i = pl.program_id(0)
    j = pl.program_id(1)
    tm = f_ref.shape[0]
    hw = GRID_W

    @pl.when(j == 0)
    def _():
        top = (i % tiles_per_image) == 0
        bottom = (i % tiles_per_image) == tiles_per_image - 1
        fext_scr[pl.ds(0, hw), :] = jnp.where(top, jnp.zeros_like(fp_ref), fp_ref[...])
        fext_scr[pl.ds(hw, tm), :] = f_ref[...]
        fext_scr[pl.ds(hw + tm, hw), :] = jnp.where(bottom, jnp.zeros_like(fn_ref), fn_ref[...])

    nc = MXU_COLS
    cols = [pl.ds(c0, nc) for c0 in range(0, g_ref.shape[1], nc)]
    a_parts = [_dot(fext_scr[...], wa_ref[:, cs]) for cs in cols]
    gates = []
    for cs, a in zip(cols, a_parts):
        gates.append(_conv_gelu(a, cw_ref[:, cs], cb_ref[:, cs]))
        if len(gates) == 1:
            v_parts = [_dot(f_ref[...], wv_ref[:, c2]) for c2 in cols]
    for cs, gate, val in zip(cols, gates, v_parts):
        g_ref[:, cs] = (gate * val).astype(g_ref.dtype)


def _ffn_up(f2d, w_up, conv_w, conv_b, rows_per_image, tm=1024, tf=512):
    m, d = f2d.shape
    d_ff = w_up.shape[1] // 2
    assert rows_per_image % tm == 0 and tm % GRID_W == 0 and d_ff % tf == 0
    assert tf % MXU_COLS == 0
    hb = tm // GRID_W
    n_halo = m // GRID_W
    return pl.pallas_call(
        functools.partial(_ffn_up_kernel, tiles_per_image=rows_per_image // tm),
        grid=(m // tm, d_ff // tf),
        in_specs=[pl.BlockSpec((tm, d), lambda i, j: (i, 0)),
                  pl.BlockSpec((GRID_W, d), lambda i, j: (jnp.maximum(i * hb - 1, 0), 0)),
                  pl.BlockSpec((GRID_W, d), lambda i, j: (jnp.minimum((i + 1) * hb, n_halo - 1), 0)),
                  pl.BlockSpec((d, tf), lambda i, j: (0, j)),
                  pl.BlockSpec((d, tf), lambda i, j: (0, d_ff // tf + j)),
                  pl.BlockSpec((9, tf), lambda i, j: (0, j)),
                  pl.BlockSpec((1, tf), lambda i, j: (0, j))],
        out_specs=pl.BlockSpec((tm, tf), lambda i, j: (i, j)),
        out_shape=jax.ShapeDtypeStruct((m, d_ff), BF16),
        scratch_shapes=[pltpu.VMEM((tm + 2 * GRID_W, d), BF16)],
        compiler_params=_params(("parallel", "arbitrary")),
        name="ffn_up",
    )(f2d, f2d, f2d, w_up, w_up, conv_w, conv_b)


def _ffn_down_kernel(g_ref, w_ref, h_ref, gt_ref, pg_ref, o_ref):
    y = _dot(g_ref[...], w_ref[...])
    o_ref[...] = h_ref[...] + gt_ref[0] * _rms(y, pg_ref[...])


def _ffn_down(g2d, w_down, h2d, gate, post_g, rows_per_mod, tm=512):
    m, d = h2d.shape
    d_ff = g2d.shape[1]
    assert m % tm == 0
    tiles_per_mod = rows_per_mod // tm
    return pl.pallas_call(
        _ffn_down_kernel,
        grid=(m // tm,),
        in_specs=[pl.BlockSpec((tm, d_ff), lambda i: (i, 0)),
                  pl.BlockSpec((d_ff, d), lambda i: (0, 0), pipeline_mode=pl.Buffered(1)),
                  pl.BlockSpec((tm, d), lambda i: (i, 0)),
                  pl.BlockSpec((1, 1, d), lambda i: (i // tiles_per_mod, 0, 0)),
                  pl.BlockSpec((1, d), lambda i: (0, 0))],
        out_specs=pl.BlockSpec((tm, d), lambda i: (i, 0)),
        out_shape=jax.ShapeDtypeStruct((m, d), F32),
        compiler_params=pltpu.CompilerParams(dimension_semantics=("arbitrary",),
                                             vmem_limit_bytes=FFN_DOWN_VMEM_LIMIT),
        name="ffn_down",
    )(g2d, w_down, h2d, gate, post_g)


def _gate_operands(w_f, b_f, w_b, b_b):
    r, kdim = w_f.shape
    wp = jnp.zeros((LANES, GLA_HEADS, 2, GLA_DK), BF16)
    wp = wp.at[:r, :, 0].set(w_f.astype(BF16).reshape(r, GLA_HEADS, GLA_DK))
    wp = wp.at[r:2 * r, :, 1].set(w_b.astype(BF16).reshape(r, GLA_HEADS, GLA_DK))
    bias = jnp.stack([b_f.reshape(GLA_HEADS, GLA_DK), b_b.reshape(GLA_HEADS, GLA_DK)], axis=1)
    return wp.reshape(LANES, 2 * kdim), bias.reshape(1, 2 * kdim)


def kernel(x, c, ctx, c_ctx, ada_w, ada_b, pre_mix_g, post_mix_g, pre_ffn_g, post_ffn_g, w_in, sgu_ln_g, sgu_ln_b, sgu_w, sgu_b, gla_gate_w_f, gla_gate_b_f, gla_gate_w_b, gla_gate_b_b, gla_norm_g, w_out, ffn_w_up, ffn_conv_w, ffn_conv_b, ffn_w_down):
    bsz, l, d = x.shape
    lc = ctx.shape[1]
    assert ada_w.shape[0] == 1, "single-layer kernel"
    d_ff = ffn_w_down.shape[1]

    c_rows = jnp.zeros((8, d), F32).at[:bsz].set(c).at[bsz].set(c_ctx)
    mod = _modulation(c_rows, ada_w[0], ada_b)
    sh_m, sc_m, gt_m, sh_f, sc_f, gt_f = [mod[:bsz, t * d:(t + 1) * d].reshape(bsz, 1, d) for t in range(N_MOD)]
    csh_m = mod[bsz:bsz + 1, 0:d].reshape(1, 1, d)
    csc_m = mod[bsz:bsz + 1, d:2 * d].reshape(1, 1, d)

    w_in_t = jnp.swapaxes(w_in[0], 0, 1).astype(BF16)
    w_lr_t = jnp.pad(w_in_t[COL_LR:], ((0, LANES - 2 * GLA_LOWRANK), (0, 0)))

    p_ctx, lr_ctx = _inproj(ctx.reshape(bsz * lc, d), csh_m, csc_m, pre_mix_g, w_in_t, w_lr_t,
                            ("none",) * 3, tm=lc, tn=512, first_col=COL_K, rows_per_mod=bsz * lc)
    x2d = x.reshape(bsz * l, d)
    p_lat, lr_lat = _inproj(x2d, sh_m, sc_m, pre_mix_g, w_in_t, w_lr_t,
                            ("gelu", "gelu", "none", "none", "silu"), tm=1024, tn=1024, first_col=0,
                            rows_per_mod=l)

    y_b = _gla(p_lat.reshape(bsz, l, N_MAIN), lr_lat.reshape(bsz, l, LANES),
               p_ctx.reshape(bsz, lc, -1), lr_ctx.reshape(bsz, lc, LANES),
               *_gate_operands(gla_gate_w_f[0], gla_gate_b_f[0], gla_gate_w_b[0], gla_gate_b_b[0]), gla_norm_g)
    y_a, w_out_bf, w_down_bf = _sgu(p_lat, sgu_ln_g, sgu_ln_b, sgu_w[0].astype(BF16), sgu_b[0][:, :, None],
                                    cast_weights=(w_out[0], ffn_w_down[0]))
    h, f, w_up_bf = _outproj(y_a, y_b.reshape(bsz * l, MIX_B), w_out_bf, x2d, gt_m,
                             post_mix_g, pre_ffn_g, sc_f, sh_f, cast_weights=(ffn_w_up[0],), rows_per_mod=l)
    g = _ffn_up(f, w_up_bf, ffn_conv_w[0].reshape(9, d_ff), ffn_conv_b, rows_per_image=l)
    out = _ffn_down(g, w_down_bf, h, gt_f, post_ffn_g, rows_per_mod=l)
    return out.reshape(bsz, l, d)
```

```python
import functools

import jax
import jax.numpy as jnp
from jax import lax
from jax.experimental import pallas as pl
from jax.experimental.pallas import tpu as pltpu

F32 = jnp.float32
BF16 = jnp.bfloat16

EPS = 1e-6
GRID_W = 64
MIX_A = 1024
MIX_B = 1024
SGU_GROUPS = 4
SGU_CHUNK = 128
SGU_GW = MIX_A // SGU_GROUPS
GLA_HEADS = 4
GLA_DK = 128
GLA_DV = 256
GLA_KDIM = GLA_HEADS * GLA_DK
GLA_LOWRANK = 16
GLA_NORMALIZER = 16.0
GLA_CHUNK = 64
N_MOD = 6
LANES = 128
ROW_SUBTILE = 256
VMEM_LIMIT = 56 * 1024 * 1024
FFN_DOWN_VMEM_LIMIT = 60 * 1024 * 1024

COL_U, COL_VS, COL_Q, COL_K, COL_V, COL_R, COL_LR = 0, 1024, 2048, 2560, 3072, 4096, 5120
N_MAIN = COL_LR


def _dot(a, b):
    return jnp.dot(a, b, preferred_element_type=F32)


def _nt_dot(a, b):
    return lax.dot_general(a, b, (((1,), (1,)), ((), ())), preferred_element_type=F32)


def _silu(x):
    return x / (1.0 + jnp.exp(-x))


_GELU_A = -2.0 * 0.7978845608028654 * 1.4426950408889634
_GELU_B = _GELU_A * 0.044715


def _gelu(x):
    return x / (1.0 + jnp.exp2((_GELU_A + _GELU_B * (x * x)) * x))


def _params(sem):
    return pltpu.CompilerParams(dimension_semantics=sem, vmem_limit_bytes=VMEM_LIMIT)


def _mod_kernel(c_ref, w_ref, b_ref, o_ref):
    s = _silu(c_ref[...]).astype(BF16)
    o_ref[...] = _dot(s, w_ref[...].astype(BF16)) + b_ref[...]


def _modulation(c_rows, ada_w, ada_b, tn=1024):
    rows, d = c_rows.shape
    n = ada_w.shape[1]
    return pl.pallas_call(
        _mod_kernel,
        grid=(n // tn,),
        in_specs=[pl.BlockSpec((rows, d), lambda j: (0, 0)),
                  pl.BlockSpec((d, tn), lambda j: (0, j)),
                  pl.BlockSpec((1, tn), lambda j: (0, j))],
        out_specs=pl.BlockSpec((rows, tn), lambda j: (0, j)),
        out_shape=jax.ShapeDtypeStruct((rows, n), F32),
        compiler_params=_params(("arbitrary",)),
        name="mod",
    )(c_rows, ada_w, ada_b)


def _inproj_kernel(x_ref, sh_ref, sc_ref, g_ref, w_ref, wlr_ref, p_ref, lr_ref, a_scr, *, acts):
    j = pl.program_id(1)

    @pl.when(j == 0)
    def _():
        x = x_ref[...]
        y = x * lax.rsqrt(jnp.mean(x * x, axis=-1, keepdims=True) + EPS) * g_ref[...]
        a = (y * (1.0 + sc_ref[0]) + sh_ref[0]).astype(BF16)
        a_scr[...] = a
        lr_ref[...] = _nt_dot(a, wlr_ref[...])

    acc = _nt_dot(a_scr[...], w_ref[...])
    fns = {"gelu": _gelu, "silu": _silu, "none": lambda t: t}
    for name in sorted(set(acts)):
        idx = [t for t, a in enumerate(acts) if a == name]
        cond = functools.reduce(jnp.logical_or, [j == t for t in idx])

        @pl.when(cond)
        def _(name=name):
            p_ref[...] = fns[name](acc).astype(p_ref.dtype)


def _inproj(x2d, shift, scale, gain, w_t, w_lr_t, acts, tm, tn, first_col, rows_per_mod):
    m, d = x2d.shape
    n = len(acts) * tn
    assert m % tm == 0 and first_col % tn == 0 and rows_per_mod % tm == 0
    tiles_per_mod = rows_per_mod // tm
    blk0 = first_col // tn
    return pl.pallas_call(
        functools.partial(_inproj_kernel, acts=acts),
        grid=(m // tm, len(acts)),
        in_specs=[pl.BlockSpec((tm, d), lambda i, j: (i, 0)),
                  pl.BlockSpec((1, 1, d), lambda i, j: (i // tiles_per_mod, 0, 0)),
                  pl.BlockSpec((1, 1, d), lambda i, j: (i // tiles_per_mod, 0, 0)),
                  pl.BlockSpec((1, d), lambda i, j: (0, 0)),
                  pl.BlockSpec((tn, d), lambda i, j: (blk0 + j, 0)),
                  pl.BlockSpec((LANES, d), lambda i, j: (0, 0))],
        out_specs=[pl.BlockSpec((tm, tn), lambda i, j: (i, j)),
                   pl.BlockSpec((tm, LANES), lambda i, j: (i, 0))],
        out_shape=[jax.ShapeDtypeStruct((m, n), BF16),
                   jax.ShapeDtypeStruct((m, LANES), F32)],
        scratch_shapes=[pltpu.VMEM((tm, d), BF16)],
        compiler_params=_params(("parallel", "arbitrary")),
        name="inproj",
    )(x2d, shift, scale, gain, w_t, w_lr_t)


def _inproj_rows_kernel(x_ref, sh_ref, sc_ref, g_ref, w_ref, wlr_ref, p_ref, lr_ref, *, acts, tn):
    x = x_ref[...]
    y = x * lax.rsqrt(jnp.mean(x * x, axis=-1, keepdims=True) + EPS) * g_ref[...]
    a = (y * (1.0 + sc_ref[0]) + sh_ref[0]).astype(BF16)
    lr_ref[...] = _nt_dot(a, wlr_ref[...])
    fns = {"gelu": _gelu, "silu": _silu, "none": lambda t: t}
    for t, name in enumerate(acts):
        cs = pl.ds(t * tn, tn)
        p_ref[:, cs] = fns[name](_nt_dot(a, w_ref[cs, :])).astype(p_ref.dtype)


def _inproj_rows(x2d, shift, scale, gain, w_t, w_lr_t, acts, tm, tn, rows_per_mod):
    m, d = x2d.shape
    n = len(acts) * tn
    assert m % tm == 0 and rows_per_mod % tm == 0
    tiles_per_mod = rows_per_mod // tm
    return pl.pallas_call(
        functools.partial(_inproj_rows_kernel, acts=acts, tn=tn),
        grid=(m // tm,),
        in_specs=[pl.BlockSpec((tm, d), lambda i: (i, 0)),
                  pl.BlockSpec((1, 1, d), lambda i: (i // tiles_per_mod, 0, 0)),
                  pl.BlockSpec((1, 1, d), lambda i: (i // tiles_per_mod, 0, 0)),
                  pl.BlockSpec((1, d), lambda i: (0, 0)),
                  pl.BlockSpec((n, d), lambda i: (0, 0), pipeline_mode=pl.Buffered(1)),
                  pl.BlockSpec((LANES, d), lambda i: (0, 0), pipeline_mode=pl.Buffered(1))],
        out_specs=[pl.BlockSpec((tm, n), lambda i: (i, 0)),
                   pl.BlockSpec((tm, LANES), lambda i: (i, 0))],
        out_shape=[jax.ShapeDtypeStruct((m, n), BF16),
                   jax.ShapeDtypeStruct((m, LANES), F32)],
        compiler_params=_params(("arbitrary",)),
        name="inproj_rows",
    )(x2d, shift, scale, gain, w_t, w_lr_t)


GLA_BLOCK = 4 * GLA_CHUNK
GLA_PAIR = 2 * GLA_CHUNK


def _log_decay(z):
    log_sig = -(jnp.maximum(-z, 0.0) + jnp.log(1.0 + jnp.exp(-jnp.abs(z))))
    return log_sig * (1.0 / GLA_NORMALIZER)


def _split(x):
    hi = x.astype(BF16)
    return hi, (x - hi.astype(F32)).astype(BF16)


def _gla_masks():
    n, c = GLA_BLOCK, GLA_CHUNK
    r = jnp.arange(n)[:, None]
    s = jnp.arange(n)[None, :]
    same = (r // c) == (s // c)
    return ((same & (s <= r)).astype(BF16), (same & (s >= r)).astype(BF16), jnp.eye(GLA_DV, dtype=BF16))


SUBLANES = 8


def _chunk_rows(blk, cidx, rows):
    return pl.ds(pl.multiple_of((blk * (GLA_BLOCK // GLA_CHUNK) + cidx) * rows, rows), rows)


def _gla_gates(rs, blk, q_ref, k_ref, v_ref, lr_ref, w_ref, b_ref, lower_ref, eye_ref, vt_scr, fwd, bwd):
    c, nb, dk = GLA_CHUNK, GLA_BLOCK, GLA_DK
    lr = lr_ref[rs, :].astype(BF16)
    g = _log_decay(_dot(lr, w_ref[...]) + b_ref[...])
    g_hi, g_lo = _split(g)
    p = _dot(lower_ref[...], g_hi) + _dot(lower_ref[...], g_lo)
    last = [p[i * c + c - 1:(i + 1) * c, :] for i in range(nb // c)]
    tot = jnp.concatenate([jnp.broadcast_to(t, (c, 2 * dk)) for t in last], axis=0)
    vt_scr[:, rs] = _nt_dot(eye_ref[...], v_ref[rs, :]).astype(BF16)
    k32 = k_ref[rs, :].astype(F32)
    q32 = None if q_ref is None else q_ref[rs, :].astype(F32) * (GLA_DK ** -0.5)
    plans = ((fwd, slice(0, dk), p, tot - p), (bwd, slice(dk, 2 * dk), tot - p + g, p - g))
    for d, lanes, b_cum, to_end in plans:
        d["kd"][rs, :] = (k32 * jnp.exp(to_end[:, lanes])).astype(BF16)
        for cidx in range(nb // c):
            dec = jnp.exp(last[cidx][:, lanes])
            d["dec"][_chunk_rows(blk, cidx, SUBLANES), :] = jnp.broadcast_to(dec, (SUBLANES, dk))
        if q32 is not None:
            d["qe"][rs, :] = (q32 * jnp.exp(b_cum[:, lanes])).astype(BF16)
            d["ke"][rs, :] = (k32 * jnp.exp(-b_cum[:, lanes])).astype(BF16)


def _gla_scan_block(rs, blk, order, d, vt_scr, s_scr, record):
    c = GLA_CHUNK
    v_t = vt_scr[:, rs]
    kd = d["kd"][rs, :]
    zeros = jnp.zeros((c, kd.shape[1]), kd.dtype)
    s = s_scr[...]
    for cidx in order:
        pair, half = divmod(cidx, 2)
        slab = v_t[:, pair * GLA_PAIR:(pair + 1) * GLA_PAIR]
        kc = kd[cidx * c:(cidx + 1) * c]
        k_only = jnp.concatenate([kc, zeros] if half == 0 else [zeros, kc], axis=0)
        kv_t = _dot(slab, k_only)
        if record:
            d["snap"][_chunk_rows(blk, cidx, GLA_DV), :] = s.astype(BF16)
        dec = d["dec"][_chunk_rows(blk, cidx, SUBLANES), :]
        s = s * dec[0:1, :] + kv_t
    s_scr[...] = s


def _gla_output(rs, blk, v_ref, r_ref, dirs, ng_ref, y_ref):
    c = GLA_CHUNK
    v = v_ref[rs, :]
    o = None
    for d in dirs:
        q_e = d["qe"][rs, :]
        att = jnp.where(d["tri"][...] > 0, _nt_dot(q_e, d["ke"][rs, :]), 0.0).astype(BF16)
        inter = [_nt_dot(q_e[cidx * c:(cidx + 1) * c], d["snap"][_chunk_rows(blk, cidx, GLA_DV), :])
                 for cidx in range(GLA_BLOCK // c)]
        od = _dot(att, v) + jnp.concatenate(inter, axis=0)
        o = od if o is None else o + od
    o = o * lax.rsqrt(jnp.mean(o * o, axis=-1, keepdims=True) + EPS) * ng_ref[...]
    y_ref[rs, :] = (o * r_ref[rs, :].astype(F32)).astype(y_ref.dtype)


def _gla_kernel(q_ref, k_ref, v_ref, r_ref, lr_ref, kc_ref, vc_ref, lrc_ref, w_ref, b_ref, ng_ref,
                lower_ref, upper_ref, eye_ref,
                y_ref, sf_scr, sb_scr, vt_scr, qef_scr, qeb_scr, kef_scr, keb_scr, kdf_scr, kdb_scr,
                decf_scr, decb_scr, snapf_scr, snapb_scr):
    fwd = dict(tri=lower_ref, qe=qef_scr, ke=kef_scr, kd=kdf_scr, dec=decf_scr, snap=snapf_scr)
    bwd = dict(tri=upper_ref, qe=qeb_scr, ke=keb_scr, kd=kdb_scr, dec=decb_scr, snap=snapb_scr)
    dirs = (fwd, bwd)
    l, lc = q_ref.shape[0], kc_ref.shape[0]
    n_blk, n_cblk = l // GLA_BLOCK, lc // GLA_BLOCK
    asc = tuple(range(GLA_BLOCK // GLA_CHUNK))
    desc = asc[::-1]

    def block_rows(j):
        return pl.ds(pl.multiple_of(j * GLA_BLOCK, GLA_BLOCK), GLA_BLOCK)

    sf_scr[...] = jnp.zeros_like(sf_scr)
    sb_scr[...] = jnp.zeros_like(sb_scr)

    for j in range(n_cblk):
        _gla_gates(block_rows(j), j, None, kc_ref, vc_ref, lrc_ref, w_ref, b_ref, lower_ref, eye_ref,
                   vt_scr, fwd, bwd)
    for j in range(n_cblk):
        _gla_scan_block(block_rows(j), j, asc, fwd, vt_scr, sf_scr, False)
        jb = n_cblk - 1 - j
        _gla_scan_block(block_rows(jb), jb, desc, bwd, vt_scr, sb_scr, False)

    def gates(j, carry):
        _gla_gates(block_rows(j), j, q_ref, k_ref, v_ref, lr_ref, w_ref, b_ref, lower_ref, eye_ref,
                   vt_scr, fwd, bwd)
        return carry

    def scan(j, carry):
        _gla_scan_block(block_rows(j), j, asc, fwd, vt_scr, sf_scr, True)
        jb = n_blk - 1 - j
        _gla_scan_block(block_rows(jb), jb, desc, bwd, vt_scr, sb_scr, True)
        return carry

    def output(j, carry):
        _gla_output(block_rows(j), j, v_ref, r_ref, dirs, ng_ref, y_ref)
        return carry

    lax.fori_loop(0, n_blk, gates, 0)
    lax.fori_loop(0, n_blk, scan, 0)
    lax.fori_loop(0, n_blk, output, 0, unroll=2)


def _gla(p_lat, lr_lat, p_ctx, lr_ctx, gate_w, gate_b, norm_g):
    bsz, l, _ = p_lat.shape
    lc = p_ctx.shape[1]
    assert l % GLA_BLOCK == 0 and lc % GLA_BLOCK == 0 and lc <= l
    qb, kb = COL_Q // GLA_DK, COL_K // GLA_DK
    vb, rb = COL_V // GLA_DV, COL_R // GLA_DV
    gate_specs = [pl.BlockSpec((LANES, 2 * GLA_DK), lambda b, h: (0, h)),
                  pl.BlockSpec((1, 2 * GLA_DK), lambda b, h: (0, h))]
    masks = _gla_masks()
    mask_specs = [pl.BlockSpec(m.shape, lambda b, h: (0, 0)) for m in masks]
    n_chunks = l // GLA_CHUNK
    return pl.pallas_call(
        _gla_kernel,
        grid=(bsz, GLA_HEADS),
        in_specs=[pl.BlockSpec((None, l, GLA_DK), lambda b, h: (b, 0, qb + h)),
                  pl.BlockSpec((None, l, GLA_DK), lambda b, h: (b, 0, kb + h)),
                  pl.BlockSpec((None, l, GLA_DV), lambda b, h: (b, 0, vb + h)),
                  pl.BlockSpec((None, l, GLA_DV), lambda b, h: (b, 0, rb + h)),
                  pl.BlockSpec((None, l, LANES), lambda b, h: (b, 0, 0)),
                  pl.BlockSpec((None, lc, GLA_DK), lambda b, h: (b, 0, h)),
                  pl.BlockSpec((None, lc, GLA_DV), lambda b, h: (b, 0, GLA_KDIM // GLA_DV + h)),
                  pl.BlockSpec((None, lc, LANES), lambda b, h: (b, 0, 0))]
                 + gate_specs
                 + [pl.BlockSpec((1, GLA_DV), lambda b, h: (0, 0))] + mask_specs,
        out_specs=pl.BlockSpec((None, l, GLA_DV), lambda b, h: (b, 0, h)),
        out_shape=jax.ShapeDtypeStruct((bsz, l, MIX_B), BF16),
        scratch_shapes=[pltpu.VMEM((GLA_DV, GLA_DK), F32)] * 2
                       + [pltpu.VMEM((GLA_DV, l), BF16)]
                       + [pltpu.VMEM((l, GLA_DK), BF16)] * 6
                       + [pltpu.VMEM((n_chunks * SUBLANES, GLA_DK), F32)] * 2
                       + [pltpu.VMEM((n_chunks * GLA_DV, GLA_DK), BF16)] * 2,
        compiler_params=_params(("parallel", "arbitrary")),
        name="gla",
    )(p_lat, p_lat, p_lat, p_lat, lr_lat, p_ctx, p_ctx, lr_ctx, gate_w, gate_b, norm_g, *masks)


def _cast_specs(weights, n_steps, step_index=lambda i: i):
    ins, outs, shapes = [], [], []
    for w in weights:
        rows, cols = w.shape
        assert rows % (n_steps * 16) == 0, "slabs must be whole packed bf16 row tiles"
        spec = pl.BlockSpec((rows // n_steps, cols), lambda *idx: (step_index(*idx), 0))
        ins.append(spec)
        outs.append(spec)
        shapes.append(jax.ShapeDtypeStruct(w.shape, BF16))
    return ins, outs, shapes


def _cast_slabs(src_refs, dst_refs):
    for src, dst in zip(src_refs, dst_refs):
        dst[...] = src[...].astype(dst.dtype)


def _sgu_kernel(u_ref, v_ref, lg_ref, lb_ref, ws_ref, bs_ref, *rest):
    n_cast = (len(rest) - 1) // 2
    y_ref = rest[n_cast]
    _cast_slabs(rest[:n_cast], rest[n_cast + 1:])
    tm = u_ref.shape[0]
    for cidx in range(tm // SGU_CHUNK):
        rs = pl.ds(cidx * SGU_CHUNK, SGU_CHUNK)
        for g in range(SGU_GROUPS):
            cs = pl.ds(g * SGU_GW, SGU_GW)
            v = v_ref[rs, cs].astype(F32)
            mu = jnp.mean(v, axis=-1, keepdims=True)
            vc = v - mu
            var = jnp.mean(vc * vc, axis=-1, keepdims=True)
            vn = vc * lax.rsqrt(var + EPS) * lg_ref[:, cs] + lb_ref[:, cs]
            s = _dot(ws_ref[g], vn.astype(BF16)) + bs_ref[g]
            y_ref[rs, cs] = (u_ref[rs, cs].astype(F32) * s).astype(y_ref.dtype)


def _sgu(p_lat2d, ln_g, ln_b, w_s, b_s, cast_weights, tm=512):
    m = p_lat2d.shape[0]
    cast_in, cast_out, cast_shapes = _cast_specs(cast_weights, m // tm)
    return pl.pallas_call(
        _sgu_kernel,
        grid=(m // tm,),
        in_specs=[pl.BlockSpec((tm, MIX_A), lambda i: (i, COL_U // MIX_A)),
                  pl.BlockSpec((tm, MIX_A), lambda i: (i, COL_VS // MIX_A)),
                  pl.BlockSpec((1, MIX_A), lambda i: (0, 0)),
                  pl.BlockSpec((1, MIX_A), lambda i: (0, 0)),
                  pl.BlockSpec((SGU_GROUPS, SGU_CHUNK, SGU_CHUNK), lambda i: (0, 0, 0)),
                  pl.BlockSpec((SGU_GROUPS, SGU_CHUNK, 1), lambda i: (0, 0, 0))] + cast_in,
        out_specs=[pl.BlockSpec((tm, MIX_A), lambda i: (i, 0))] + cast_out,
        out_shape=[jax.ShapeDtypeStruct((m, MIX_A), BF16)] + cast_shapes,
        compiler_params=_params(("parallel",)),
        name="sgu",
    )(p_lat2d, p_lat2d, ln_g, ln_b, w_s, b_s, *cast_weights)


def _rms(x, g):
    return x * lax.rsqrt(jnp.mean(x * x, axis=-1, keepdims=True) + EPS) * g


def _outproj_kernel(ya_ref, yb_ref, wa_ref, wb_ref, x_ref, gt_ref, pg_ref, fg_ref, sc_ref, sh_ref, *rest):
    n_cast = (len(rest) - 2) // 2
    h_ref, f_ref = rest[n_cast:n_cast + 2]
    _cast_slabs(rest[:n_cast], rest[n_cast + 2:])
    for r0 in range(0, h_ref.shape[0], ROW_SUBTILE):
        rs = pl.ds(r0, ROW_SUBTILE)
        y = _dot(ya_ref[rs, :], wa_ref[...]) + _dot(yb_ref[rs, :], wb_ref[...])
        h = x_ref[rs, :] + gt_ref[0] * _rms(y, pg_ref[...])
        h_ref[rs, :] = h
        f_ref[rs, :] = (_rms(h, fg_ref[...]) * (1.0 + sc_ref[0]) + sh_ref[0]).astype(f_ref.dtype)


def _outproj(y_a, y_b, w_out, x2d, gate, post_g, ffn_g, scale_f, shift_f, cast_weights, rows_per_mod, tm=512):
    m, d = x2d.shape
    tiles_per_mod = rows_per_mod // tm
    mod_spec = pl.BlockSpec((1, 1, d), lambda i: (i // tiles_per_mod, 0, 0))
    vec_spec = pl.BlockSpec((1, d), lambda i: (0, 0))
    cast_in, cast_out, cast_shapes = _cast_specs(cast_weights, m // tm)
    return pl.pallas_call(
        _outproj_kernel,
        grid=(m // tm,),
        in_specs=[pl.BlockSpec((tm, MIX_A), lambda i: (i, 0)),
                  pl.BlockSpec((tm, MIX_B), lambda i: (i, 0)),
                  pl.BlockSpec((MIX_A, d), lambda i: (0, 0), pipeline_mode=pl.Buffered(1)),
                  pl.BlockSpec((MIX_B, d), lambda i: (MIX_A // MIX_B, 0), pipeline_mode=pl.Buffered(1)),
                  pl.BlockSpec((tm, d), lambda i: (i, 0)),
                  mod_spec, vec_spec, vec_spec, mod_spec, mod_spec] + cast_in,
        out_specs=[pl.BlockSpec((tm, d), lambda i: (i, 0)),
                   pl.BlockSpec((tm, d), lambda i: (i, 0))] + cast_out,
        out_shape=[jax.ShapeDtypeStruct((m, d), F32),
                   jax.ShapeDtypeStruct((m, d), BF16)] + cast_shapes,
        compiler_params=_params(("arbitrary",)),
        name="outproj",
    )(y_a, y_b, w_out, w_out, x2d, gate, post_g, ffn_g, scale_f, shift_f, *cast_weights)


def _conv_gelu(a, cw, cb):
    hw = GRID_W
    ext = a.shape[0]
    tm = ext - 2 * hw
    col = lax.broadcasted_iota(jnp.int32, (ext, 1), 0) % hw
    left = jnp.where(col > 0, pltpu.roll(a, 1, 0), 0.0)
    right = jnp.where(col < hw - 1, pltpu.roll(a, ext - 1, 0), 0.0)
    acc = jnp.broadcast_to(cb, (tm, a.shape[1]))
    for dr in range(3):
        lo = dr * hw
        acc = acc + cw[3 * dr + 0:3 * dr + 1, :] * left[lo:lo + tm, :]
        acc = acc + cw[3 * dr + 1:3 * dr + 2, :] * a[lo:lo + tm, :]
        acc = acc + cw[3 * dr + 2:3 * dr + 3, :] * right[lo:lo + tm, :]
    return _gelu(acc)


def _ffn_up_kernel(f_ref, fp_ref, fn_ref, wa_ref, wv_ref, cw_ref, cb_ref, *rest, tiles_per_image):
    n_cast = (len(rest) - 2) // 2
    g_ref, fext_scr = rest[n_cast], rest[-1]
    _cast_slabs(rest[:n_cast], rest[n_cast + 1:-1])
    i = pl.program_id(0)
    j = pl.program_id(1)
    tm = f_ref.shape[0]
    hw = GRID_W

    @pl.when(j == 0)
    def _():
        top = (i % tiles_per_image) == 0
        bottom = (i % tiles_per_image) == tiles_per_image - 1
        fext_scr[pl.ds(0, hw), :] = jnp.where(top, jnp.zeros_like(fp_ref), fp_ref[...])
        fext_scr[pl.ds(hw, tm), :] = f_ref[...]
        fext_scr[pl.ds(hw + tm, hw), :] = jnp.where(bottom, jnp.zeros_like(fn_ref), fn_ref[...])

    a = _dot(fext_scr[...], wa_ref[...])
    val = _dot(f_ref[...], wv_ref[...])
    g_ref[...] = (_conv_gelu(a, cw_ref[...], cb_ref[...]) * val).astype(g_ref.dtype)


def _ffn_up(f2d, w_up, conv_w, conv_b, cast_weights, rows_per_image, tm=1024, tf=512):
    m, d = f2d.shape
    d_ff = w_up.shape[1] // 2
    assert rows_per_image % tm == 0 and tm % GRID_W == 0 and d_ff % tf == 0
    hb = tm // GRID_W
    n_halo = m // GRID_W
    n_j = d_ff // tf
    cast_in, cast_out, cast_shapes = _cast_specs(cast_weights, (m // tm) * n_j, lambda i, j: i * n_j + j)
    return pl.pallas_call(
        functools.partial(_ffn_up_kernel, tiles_per_image=rows_per_image // tm),
        grid=(m // tm, n_j),
        in_specs=[pl.BlockSpec((tm, d), lambda i, j: (i, 0)),
                  pl.BlockSpec((GRID_W, d), lambda i, j: (jnp.maximum(i * hb - 1, 0), 0)),
                  pl.BlockSpec((GRID_W, d), lambda i, j: (jnp.minimum((i + 1) * hb, n_halo - 1), 0)),
                  pl.BlockSpec((d, tf), lambda i, j: (0, j)),
                  pl.BlockSpec((d, tf), lambda i, j: (0, n_j + j)),
                  pl.BlockSpec((9, tf), lambda i, j: (0, j)),
                  pl.BlockSpec((1, tf), lambda i, j: (0, j))] + cast_in,
        out_specs=[pl.BlockSpec((tm, tf), lambda i, j: (i, j))] + cast_out,
        out_shape=[jax.ShapeDtypeStruct((m, d_ff), BF16)] + cast_shapes,
        scratch_shapes=[pltpu.VMEM((tm + 2 * GRID_W, d), BF16)],
        compiler_params=_params(("arbitrary", "arbitrary")),
        name="ffn_up",
    )(f2d, f2d, f2d, w_up, w_up, conv_w, conv_b, *cast_weights)


def _ffn_down_kernel(g_ref, w_ref, h_ref, gt_ref, pg_ref, o_ref):
    for r0 in range(0, o_ref.shape[0], ROW_SUBTILE):
        rs = pl.ds(r0, ROW_SUBTILE)
        y = _dot(g_ref[rs, :], w_ref[...])
        o_ref[rs, :] = h_ref[rs, :] + gt_ref[0] * _rms(y, pg_ref[...])


def _ffn_down(g2d, w_down, h2d, gate, post_g, rows_per_mod, tm=512):
    m, d = h2d.shape
    d_ff = g2d.shape[1]
    assert m % tm == 0
    tiles_per_mod = rows_per_mod // tm
    return pl.pallas_call(
        _ffn_down_kernel,
        grid=(m // tm,),
        in_specs=[pl.BlockSpec((tm, d_ff), lambda i: (i, 0)),
                  pl.BlockSpec((d_ff, d), lambda i: (0, 0), pipeline_mode=pl.Buffered(1)),
                  pl.BlockSpec((tm, d), lambda i: (i, 0)),
                  pl.BlockSpec((1, 1, d), lambda i: (i // tiles_per_mod, 0, 0)),
                  pl.BlockSpec((1, d), lambda i: (0, 0))],
        out_specs=pl.BlockSpec((tm, d), lambda i: (i, 0)),
        out_shape=jax.ShapeDtypeStruct((m, d), F32),
        compiler_params=pltpu.CompilerParams(dimension_semantics=("arbitrary",),
                                             vmem_limit_bytes=FFN_DOWN_VMEM_LIMIT),
        name="ffn_down",
    )(g2d, w_down, h2d, gate, post_g)


def _gate_operands(w_f, b_f, w_b, b_b):
    r, kdim = w_f.shape
    wp = jnp.zeros((LANES, GLA_HEADS, 2, GLA_DK), BF16)
    wp = wp.at[:r, :, 0].set(w_f.astype(BF16).reshape(r, GLA_HEADS, GLA_DK))
    wp = wp.at[r:2 * r, :, 1].set(w_b.astype(BF16).reshape(r, GLA_HEADS, GLA_DK))
    bias = jnp.stack([b_f.reshape(GLA_HEADS, GLA_DK), b_b.reshape(GLA_HEADS, GLA_DK)], axis=1)
    return wp.reshape(LANES, 2 * kdim), bias.reshape(1, 2 * kdim)


def kernel(x, c, ctx, c_ctx, ada_w, ada_b, pre_mix_g, post_mix_g, pre_ffn_g, post_ffn_g, w_in, sgu_ln_g, sgu_ln_b, sgu_w, sgu_b, gla_gate_w_f, gla_gate_b_f, gla_gate_w_b, gla_gate_b_b, gla_norm_g, w_out, ffn_w_up, ffn_conv_w, ffn_conv_b, ffn_w_down):
    bsz, l, d = x.shape
    lc = ctx.shape[1]
    assert ada_w.shape[0] == 1, "single-layer kernel"
    d_ff = ffn_w_down.shape[1]

    c_rows = jnp.zeros((8, d), F32).at[:bsz].set(c).at[bsz].set(c_ctx)
    mod = _modulation(c_rows, ada_w[0], ada_b)
    sh_m, sc_m, gt_m, sh_f, sc_f, gt_f = [mod[:bsz, t * d:(t + 1) * d].reshape(bsz, 1, d) for t in range(N_MOD)]
    csh_m = mod[bsz:bsz + 1, 0:d].reshape(1, 1, d)
    csc_m = mod[bsz:bsz + 1, d:2 * d].reshape(1, 1, d)

    w_in_t = jnp.swapaxes(w_in[0], 0, 1).astype(BF16)
    w_lr_t = jnp.pad(w_in_t[COL_LR:], ((0, LANES - 2 * GLA_LOWRANK), (0, 0)))

    p_ctx, lr_ctx = _inproj(ctx.reshape(bsz * lc, d), csh_m, csc_m, pre_mix_g, w_in_t, w_lr_t,
                            ("none",) * 3, tm=lc, tn=512, first_col=COL_K, rows_per_mod=bsz * lc)
    x2d = x.reshape(bsz * l, d)
    p_lat, lr_lat = _inproj_rows(x2d, sh_m, sc_m, pre_mix_g, w_in_t, w_lr_t,
                                 ("gelu", "gelu", "none", "none", "silu"), tm=512, tn=1024, rows_per_mod=l)

    y_b = _gla(p_lat.reshape(bsz, l, N_MAIN), lr_lat.reshape(bsz, l, LANES),
               p_ctx.reshape(bsz, lc, -1), lr_ctx.reshape(bsz, lc, LANES),
               *_gate_operands(gla_gate_w_f[0], gla_gate_b_f[0], gla_gate_w_b[0], gla_gate_b_b[0]), gla_norm_g)
    y_a, w_out_bf = _sgu(p_lat, sgu_ln_g, sgu_ln_b, sgu_w[0].astype(BF16), sgu_b[0][:, :, None],
                         cast_weights=(w_out[0],))
    h, f, w_up_bf = _outproj(y_a, y_b.reshape(bsz * l, MIX_B), w_out_bf, x2d, gt_m,
                             post_mix_g, pre_ffn_g, sc_f, sh_f, cast_weights=(ffn_w_up[0],), rows_per_mod=l)
    g, w_down_bf = _ffn_up(f, w_up_bf, ffn_conv_w[0].reshape(9, d_ff), ffn_conv_b,
                           cast_weights=(ffn_w_down[0],), rows_per_image=l)
    out = _ffn_down(g, w_down_bf, h, gt_f, post_ffn_g, rows_per_mod=l)
    return out.reshape(bsz, l, d)
```

```python
import functools

import jax
import jax.numpy as jnp
from jax import lax
from jax.experimental import pallas as pl
from jax.experimental.pallas import tpu as pltpu

F32 = jnp.float32
BF16 = jnp.bfloat16

EPS = 1e-6
GRID_W = 64
MIX_A = 1024
MIX_B = 1024
SGU_GROUPS = 4
SGU_CHUNK = 128
SGU_GW = MIX_A // SGU_GROUPS
GLA_HEADS = 4
GLA_DK = 128
GLA_DV = 256
GLA_KDIM = GLA_HEADS * GLA_DK
GLA_LOWRANK = 16
GLA_NORMALIZER = 16.0
GLA_CHUNK = 64
N_MOD = 6
LANES = 128
ROW_SUBTILE = 256
VMEM_LIMIT = 56 * 1024 * 1024
FFN_DOWN_VMEM_LIMIT = 60 * 1024 * 1024

COL_U, COL_VS, COL_Q, COL_K, COL_V, COL_R, COL_LR = 0, 1024, 2048, 2560, 3072, 4096, 5120
N_MAIN = COL_LR


def _dot(a, b):
    return jnp.dot(a, b, preferred_element_type=F32)


def _nt_dot(a, b):
    return lax.dot_general(a, b, (((1,), (1,)), ((), ())), preferred_element_type=F32)


def _silu(x):
    return x / (1.0 + jnp.exp(-x))


_GELU_A = -2.0 * 0.7978845608028654 * 1.4426950408889634
_GELU_B = _GELU_A * 0.044715


def _gelu(x):
    return x / (1.0 + jnp.exp2((_GELU_A + _GELU_B * (x * x)) * x))


def _params(sem):
    return pltpu.CompilerParams(dimension_semantics=sem, vmem_limit_bytes=VMEM_LIMIT)


def _mod_kernel(c_ref, w_ref, b_ref, o_ref):
    s = _silu(c_ref[...]).astype(BF16)
    o_ref[...] = _dot(s, w_ref[...].astype(BF16)) + b_ref[...]


def _modulation(c_rows, ada_w, ada_b, tn=1024):
    rows, d = c_rows.shape
    n = ada_w.shape[1]
    return pl.pallas_call(
        _mod_kernel,
        grid=(n // tn,),
        in_specs=[pl.BlockSpec((rows, d), lambda j: (0, 0)),
                  pl.BlockSpec((d, tn), lambda j: (0, j)),
                  pl.BlockSpec((1, tn), lambda j: (0, j))],
        out_specs=pl.BlockSpec((rows, tn), lambda j: (0, j)),
        out_shape=jax.ShapeDtypeStruct((rows, n), F32),
        compiler_params=_params(("arbitrary",)),
        name="mod",
    )(c_rows, ada_w, ada_b)


def _inproj_kernel(x_ref, sh_ref, sc_ref, g_ref, w_ref, wlr_ref, p_ref, lr_ref, a_scr, *, acts):
    j = pl.program_id(1)

    @pl.when(j == 0)
    def _():
        x = x_ref[...]
        y = x * lax.rsqrt(jnp.mean(x * x, axis=-1, keepdims=True) + EPS) * g_ref[...]
        a = (y * (1.0 + sc_ref[0]) + sh_ref[0]).astype(BF16)
        a_scr[...] = a
        lr_ref[...] = _nt_dot(a, wlr_ref[...])

    acc = _nt_dot(a_scr[...], w_ref[...])
    fns = {"gelu": _gelu, "silu": _silu, "none": lambda t: t}
    for name in sorted(set(acts)):
        idx = [t for t, a in enumerate(acts) if a == name]
        cond = functools.reduce(jnp.logical_or, [j == t for t in idx])

        @pl.when(cond)
        def _(name=name):
            p_ref[...] = fns[name](acc).astype(p_ref.dtype)


def _inproj(x2d, shift, scale, gain, w_t, w_lr_t, acts, tm, tn, first_col, rows_per_mod):
    m, d = x2d.shape
    n = len(acts) * tn
    assert m % tm == 0 and first_col % tn == 0 and rows_per_mod % tm == 0
    tiles_per_mod = rows_per_mod // tm
    blk0 = first_col // tn
    return pl.pallas_call(
        functools.partial(_inproj_kernel, acts=acts),
        grid=(m // tm, len(acts)),
        in_specs=[pl.BlockSpec((tm, d), lambda i, j: (i, 0)),
                  pl.BlockSpec((1, 1, d), lambda i, j: (i // tiles_per_mod, 0, 0)),
                  pl.BlockSpec((1, 1, d), lambda i, j: (i // tiles_per_mod, 0, 0)),
                  pl.BlockSpec((1, d), lambda i, j: (0, 0)),
                  pl.BlockSpec((tn, d), lambda i, j: (blk0 + j, 0)),
                  pl.BlockSpec((LANES, d), lambda i, j: (0, 0))],
        out_specs=[pl.BlockSpec((tm, tn), lambda i, j: (i, j)),
                   pl.BlockSpec((tm, LANES), lambda i, j: (i, 0))],
        out_shape=[jax.ShapeDtypeStruct((m, n), BF16),
                   jax.ShapeDtypeStruct((m, LANES), F32)],
        scratch_shapes=[pltpu.VMEM((tm, d), BF16)],
        compiler_params=_params(("parallel", "arbitrary")),
        name="inproj",
    )(x2d, shift, scale, gain, w_t, w_lr_t)


def _inproj_rows_kernel(x_ref, sh_ref, sc_ref, g_ref, w_ref, wlr_ref, p_ref, lr_ref, *, acts, tn):
    x = x_ref[...]
    y = x * lax.rsqrt(jnp.mean(x * x, axis=-1, keepdims=True) + EPS) * g_ref[...]
    a = (y * (1.0 + sc_ref[0]) + sh_ref[0]).astype(BF16)
    lr_ref[...] = _nt_dot(a, wlr_ref[...])
    fns = {"gelu": _gelu, "silu": _silu, "none": lambda t: t}
    for t, name in enumerate(acts):
        cs = pl.ds(t * tn, tn)
        p_ref[:, cs] = fns[name](_nt_dot(a, w_ref[cs, :])).astype(p_ref.dtype)


def _inproj_rows(x2d, shift, scale, gain, w_t, w_lr_t, acts, tm, tn, rows_per_mod):
    m, d = x2d.shape
    n = len(acts) * tn
    assert m % tm == 0 and rows_per_mod % tm == 0
    tiles_per_mod = rows_per_mod // tm
    return pl.pallas_call(
        functools.partial(_inproj_rows_kernel, acts=acts, tn=tn),
        grid=(m // tm,),
        in_specs=[pl.BlockSpec((tm, d), lambda i: (i, 0)),
                  pl.BlockSpec((1, 1, d), lambda i: (i // tiles_per_mod, 0, 0)),
                  pl.BlockSpec((1, 1, d), lambda i: (i // tiles_per_mod, 0, 0)),
                  pl.BlockSpec((1, d), lambda i: (0, 0)),
                  pl.BlockSpec((n, d), lambda i: (0, 0), pipeline_mode=pl.Buffered(1)),
                  pl.BlockSpec((LANES, d), lambda i: (0, 0), pipeline_mode=pl.Buffered(1))],
        out_specs=[pl.BlockSpec((tm, n), lambda i: (i, 0)),
                   pl.BlockSpec((tm, LANES), lambda i: (i, 0))],
        out_shape=[jax.ShapeDtypeStruct((m, n), BF16),
                   jax.ShapeDtypeStruct((m, LANES), F32)],
        compiler_params=_params(("arbitrary",)),
        name="inproj_rows",
    )(x2d, shift, scale, gain, w_t, w_lr_t)


GLA_BLOCK = 4 * GLA_CHUNK
GLA_PAIR = 2 * GLA_CHUNK
GLA_GATE_BLOCKS = 2


def _log_decay(z):
    log_sig = -(jnp.maximum(-z, 0.0) + jnp.log(1.0 + jnp.exp(-jnp.abs(z))))
    return log_sig * (1.0 / GLA_NORMALIZER)


def _split(x):
    hi = x.astype(BF16)
    return hi, (x - hi.astype(F32)).astype(BF16)


def _gla_masks():
    n, c = GLA_BLOCK, GLA_CHUNK
    r = jnp.arange(n)[:, None]
    s = jnp.arange(n)[None, :]
    same = (r // c) == (s // c)
    return ((same & (s <= r)).astype(BF16), (same & (s >= r)).astype(BF16), jnp.eye(GLA_DV, dtype=BF16))


SUBLANES = 8


def _chunk_rows(blk, cidx, rows):
    return pl.ds(pl.multiple_of((blk * (GLA_BLOCK // GLA_CHUNK) + cidx) * rows, rows), rows)


def _gla_gates(rs, blk, q_ref, k_ref, v_ref, lr_ref, w_ref, b_ref, lower_ref, eye_ref, vt_scr, fwd, bwd):
    c, dk = GLA_CHUNK, GLA_DK
    vt_scr[:, rs] = _nt_dot(eye_ref[...], v_ref[rs, :]).astype(BF16)
    g = _log_decay(_dot(lr_ref[rs, :].astype(BF16), w_ref[...]) + b_ref[...])
    g_hi, g_lo = _split(g)
    nb = g.shape[0]
    p = jnp.concatenate([_dot(lower_ref[...], g_hi[r0:r0 + GLA_BLOCK]) + _dot(lower_ref[...], g_lo[r0:r0 + GLA_BLOCK])
                         for r0 in range(0, nb, GLA_BLOCK)], axis=0)
    last = [p[i * c + c - 1:(i + 1) * c, :] for i in range(nb // c)]
    tot = jnp.concatenate([jnp.broadcast_to(t, (c, 2 * dk)) for t in last], axis=0)
    k32 = k_ref[rs, :].astype(F32)
    q32 = None if q_ref is None else q_ref[rs, :].astype(F32) * (GLA_DK ** -0.5)
    plans = ((fwd, slice(0, dk), p, tot - p), (bwd, slice(dk, 2 * dk), tot - p + g, p - g))
    for d, lanes, b_cum, to_end in plans:
        d["kd"][rs, :] = (k32 * jnp.exp(to_end[:, lanes])).astype(BF16)
        for cidx in range(nb // c):
            dec = jnp.exp(last[cidx][:, lanes])
            d["dec"][_chunk_rows(blk, cidx, SUBLANES), :] = jnp.broadcast_to(dec, (SUBLANES, dk))
        if q32 is not None:
            d["qe"][rs, :] = (q32 * jnp.exp(b_cum[:, lanes])).astype(BF16)
            d["ke"][rs, :] = (k32 * jnp.exp(-b_cum[:, lanes])).astype(BF16)


def _gla_scan_block(rs, blk, order, d, vt_scr, s_scr, record):
    c = GLA_CHUNK
    v_t = vt_scr[:, rs]
    kd = d["kd"][rs, :]
    zeros = jnp.zeros((c, kd.shape[1]), kd.dtype)
    s = s_scr[...]
    for cidx in order:
        pair, half = divmod(cidx, 2)
        slab = v_t[:, pair * GLA_PAIR:(pair + 1) * GLA_PAIR]
        kc = kd[cidx * c:(cidx + 1) * c]
        k_only = jnp.concatenate([kc, zeros] if half == 0 else [zeros, kc], axis=0)
        kv_t = _dot(slab, k_only)
        if record:
            d["snap"][_chunk_rows(blk, cidx, GLA_DV), :] = s.astype(BF16)
        dec = d["dec"][_chunk_rows(blk, cidx, SUBLANES), :]
        s = s * dec[0:1, :] + kv_t
    s_scr[...] = s


def _gla_output(rs, blk, v_ref, r_ref, dirs, ng_ref, y_ref):
    c = GLA_CHUNK
    v = v_ref[rs, :]
    o = None
    for d in dirs:
        q_e = d["qe"][rs, :]
        att = jnp.where(d["tri"][...] > 0, _nt_dot(q_e, d["ke"][rs, :]), 0.0).astype(BF16)
        inter = [_nt_dot(q_e[cidx * c:(cidx + 1) * c], d["snap"][_chunk_rows(blk, cidx, GLA_DV), :])
                 for cidx in range(GLA_BLOCK // c)]
        od = _dot(att, v) + jnp.concatenate(inter, axis=0)
        o = od if o is None else o + od
    o = o * lax.rsqrt(jnp.mean(o * o, axis=-1, keepdims=True) + EPS) * ng_ref[...]
    y_ref[rs, :] = (o * r_ref[rs, :].astype(F32)).astype(y_ref.dtype)


def _gla_kernel(q_ref, k_ref, v_ref, r_ref, lr_ref, kc_ref, vc_ref, lrc_ref, w_ref, b_ref, ng_ref,
                lower_ref, upper_ref, eye_ref,
                y_ref, sf_scr, sb_scr, vt_scr, qef_scr, qeb_scr, kef_scr, keb_scr, kdf_scr, kdb_scr,
                decf_scr, decb_scr, snapf_scr, snapb_scr):
    fwd = dict(tri=lower_ref, qe=qef_scr, ke=kef_scr, kd=kdf_scr, dec=decf_scr, snap=snapf_scr)
    bwd = dict(tri=upper_ref, qe=qeb_scr, ke=keb_scr, kd=kdb_scr, dec=decb_scr, snap=snapb_scr)
    dirs = (fwd, bwd)
    l, lc = q_ref.shape[0], kc_ref.shape[0]
    n_blk, n_cblk = l // GLA_BLOCK, lc // GLA_BLOCK
    asc = tuple(range(GLA_BLOCK // GLA_CHUNK))
    desc = asc[::-1]

    def block_rows(j):
        return pl.ds(pl.multiple_of(j * GLA_BLOCK, GLA_BLOCK), GLA_BLOCK)

    sf_scr[...] = jnp.zeros_like(sf_scr)
    sb_scr[...] = jnp.zeros_like(sb_scr)

    for j in range(n_cblk):
        _gla_gates(block_rows(j), j, None, kc_ref, vc_ref, lrc_ref, w_ref, b_ref, lower_ref, eye_ref,
                   vt_scr, fwd, bwd)
    for j in range(n_cblk):
        _gla_scan_block(block_rows(j), j, asc, fwd, vt_scr, sf_scr, False)
        jb = n_cblk - 1 - j
        _gla_scan_block(block_rows(jb), jb, desc, bwd, vt_scr, sb_scr, False)

    def gates(j, carry):
        rows = GLA_GATE_BLOCKS * GLA_BLOCK
        rs = pl.ds(pl.multiple_of(j * rows, rows), rows)
        _gla_gates(rs, j * GLA_GATE_BLOCKS, q_ref, k_ref, v_ref, lr_ref, w_ref, b_ref, lower_ref, eye_ref,
                   vt_scr, fwd, bwd)
        return carry

    def scan(j, carry):
        _gla_scan_block(block_rows(j), j, asc, fwd, vt_scr, sf_scr, True)
        jb = n_blk - 1 - j
        _gla_scan_block(block_rows(jb), jb, desc, bwd, vt_scr, sb_scr, True)
        return carry

    def output(j, carry):
        _gla_output(block_rows(j), j, v_ref, r_ref, dirs, ng_ref, y_ref)
        return carry

    lax.fori_loop(0, n_blk // GLA_GATE_BLOCKS, gates, 0)
    lax.fori_loop(0, n_blk, scan, 0, unroll=4)
    lax.fori_loop(0, n_blk, output, 0, unroll=4)


def _gla(p_lat, lr_lat, p_ctx, lr_ctx, gate_w, gate_b, norm_g):
    bsz, l, _ = p_lat.shape
    lc = p_ctx.shape[1]
    assert l % GLA_BLOCK == 0 and lc % GLA_BLOCK == 0 and lc <= l
    qb, kb = COL_Q // GLA_DK, COL_K // GLA_DK
    vb, rb = COL_V // GLA_DV, COL_R // GLA_DV
    gate_specs = [pl.BlockSpec((LANES, 2 * GLA_DK), lambda b, h: (0, h)),
                  pl.BlockSpec((1, 2 * GLA_DK), lambda b, h: (0, h))]
    masks = _gla_masks()
    mask_specs = [pl.BlockSpec(m.shape, lambda b, h: (0, 0)) for m in masks]
    n_chunks = l // GLA_CHUNK
    return pl.pallas_call(
        _gla_kernel,
        grid=(bsz, GLA_HEADS),
        in_specs=[pl.BlockSpec((None, l, GLA_DK), lambda b, h: (b, 0, qb + h)),
                  pl.BlockSpec((None, l, GLA_DK), lambda b, h: (b, 0, kb + h)),
                  pl.BlockSpec((None, l, GLA_DV), lambda b, h: (b, 0, vb + h)),
                  pl.BlockSpec((None, l, GLA_DV), lambda b, h: (b, 0, rb + h)),
                  pl.BlockSpec((None, l, LANES), lambda b, h: (b, 0, 0)),
                  pl.BlockSpec((None, lc, GLA_DK), lambda b, h: (b, 0, h)),
                  pl.BlockSpec((None, lc, GLA_DV), lambda b, h: (b, 0, GLA_KDIM // GLA_DV + h)),
                  pl.BlockSpec((None, lc, LANES), lambda b, h: (b, 0, 0))]
                 + gate_specs
                 + [pl.BlockSpec((1, GLA_DV), lambda b, h: (0, 0))] + mask_specs,
        out_specs=pl.BlockSpec((None, l, GLA_DV), lambda b, h: (b, 0, h)),
        out_shape=jax.ShapeDtypeStruct((bsz, l, MIX_B), BF16),
        scratch_shapes=[pltpu.VMEM((GLA_DV, GLA_DK), F32)] * 2
                       + [pltpu.VMEM((GLA_DV, l), BF16)]
                       + [pltpu.VMEM((l, GLA_DK), BF16)] * 6
                       + [pltpu.VMEM((n_chunks * SUBLANES, GLA_DK), F32)] * 2
                       + [pltpu.VMEM((n_chunks * GLA_DV, GLA_DK), BF16)] * 2,
        compiler_params=_params(("parallel", "arbitrary")),
        name="gla",
    )(p_lat, p_lat, p_lat, p_lat, lr_lat, p_ctx, p_ctx, lr_ctx, gate_w, gate_b, norm_g, *masks)


def _cast_specs(weights, n_steps, step_index=lambda i: i):
    ins, outs, shapes = [], [], []
    for w in weights:
        rows, cols = w.shape
        assert rows % (n_steps * 16) == 0, "slabs must be whole packed bf16 row tiles"
        spec = pl.BlockSpec((rows // n_steps, cols), lambda *idx: (step_index(*idx), 0))
        ins.append(spec)
        outs.append(spec)
        shapes.append(jax.ShapeDtypeStruct(w.shape, BF16))
    return ins, outs, shapes


def _cast_slabs(src_refs, dst_refs):
    for src, dst in zip(src_refs, dst_refs):
        dst[...] = src[...].astype(dst.dtype)


def _sgu_kernel(u_ref, v_ref, lg_ref, lb_ref, ws_ref, bs_ref, *rest):
    n_cast = (len(rest) - 1) // 2
    y_ref = rest[n_cast]
    _cast_slabs(rest[:n_cast], rest[n_cast + 1:])
    tm = u_ref.shape[0]
    for cidx in range(tm // SGU_CHUNK):
        rs = pl.ds(cidx * SGU_CHUNK, SGU_CHUNK)
        for g in range(SGU_GROUPS):
            cs = pl.ds(g * SGU_GW, SGU_GW)
            v = v_ref[rs, cs].astype(F32)
            mu = jnp.mean(v, axis=-1, keepdims=True)
            vc = v - mu
            var = jnp.mean(vc * vc, axis=-1, keepdims=True)
            vn = vc * lax.rsqrt(var + EPS) * lg_ref[:, cs] + lb_ref[:, cs]
            s = _dot(ws_ref[g], vn.astype(BF16)) + bs_ref[g]
            y_ref[rs, cs] = (u_ref[rs, cs].astype(F32) * s).astype(y_ref.dtype)


def _sgu(p_lat2d, ln_g, ln_b, w_s, b_s, cast_weights, tm=512):
    m = p_lat2d.shape[0]
    cast_in, cast_out, cast_shapes = _cast_specs(cast_weights, m // tm)
    return pl.pallas_call(
        _sgu_kernel,
        grid=(m // tm,),
        in_specs=[pl.BlockSpec((tm, MIX_A), lambda i: (i, COL_U // MIX_A)),
                  pl.BlockSpec((tm, MIX_A), lambda i: (i, COL_VS // MIX_A)),
                  pl.BlockSpec((1, MIX_A), lambda i: (0, 0)),
                  pl.BlockSpec((1, MIX_A), lambda i: (0, 0)),
                  pl.BlockSpec((SGU_GROUPS, SGU_CHUNK, SGU_CHUNK), lambda i: (0, 0, 0)),
                  pl.BlockSpec((SGU_GROUPS, SGU_CHUNK, 1), lambda i: (0, 0, 0))] + cast_in,
        out_specs=[pl.BlockSpec((tm, MIX_A), lambda i: (i, 0))] + cast_out,
        out_shape=[jax.ShapeDtypeStruct((m, MIX_A), BF16)] + cast_shapes,
        compiler_params=_params(("parallel",)),
        name="sgu",
    )(p_lat2d, p_lat2d, ln_g, ln_b, w_s, b_s, *cast_weights)


def _rms(x, g):
    return x * lax.rsqrt(jnp.mean(x * x, axis=-1, keepdims=True) + EPS) * g


def _outproj_kernel(ya_ref, yb_ref, wa_ref, wb_ref, x_ref, gt_ref, pg_ref, fg_ref, sc_ref, sh_ref, *rest):
    n_cast = (len(rest) - 2) // 2
    h_ref, f_ref = rest[n_cast:n_cast + 2]
    _cast_slabs(rest[:n_cast], rest[n_cast + 2:])
    for r0 in range(0, h_ref.shape[0], ROW_SUBTILE):
        rs = pl.ds(r0, ROW_SUBTILE)
        y = _dot(ya_ref[rs, :], wa_ref[...]) + _dot(yb_ref[rs, :], wb_ref[...])
        h = x_ref[rs, :] + gt_ref[0] * _rms(y, pg_ref[...])
        h_ref[rs, :] = h
        f_ref[rs, :] = (_rms(h, fg_ref[...]) * (1.0 + sc_ref[0]) + sh_ref[0]).astype(f_ref.dtype)


def _outproj(y_a, y_b, w_out, x2d, gate, post_g, ffn_g, scale_f, shift_f, cast_weights, rows_per_mod, tm=512):
    m, d = x2d.shape
    tiles_per_mod = rows_per_mod // tm
    mod_spec = pl.BlockSpec((1, 1, d), lambda i: (i // tiles_per_mod, 0, 0))
    vec_spec = pl.BlockSpec((1, d), lambda i: (0, 0))
    cast_in, cast_out, cast_shapes = _cast_specs(cast_weights, m // tm)
    return pl.pallas_call(
        _outproj_kernel,
        grid=(m // tm,),
        in_specs=[pl.BlockSpec((tm, MIX_A), lambda i: (i, 0)),
                  pl.BlockSpec((tm, MIX_B), lambda i: (i, 0)),
                  pl.BlockSpec((MIX_A, d), lambda i: (0, 0), pipeline_mode=pl.Buffered(1)),
                  pl.BlockSpec((MIX_B, d), lambda i: (MIX_A // MIX_B, 0), pipeline_mode=pl.Buffered(1)),
                  pl.BlockSpec((tm, d), lambda i: (i, 0)),
                  mod_spec, vec_spec, vec_spec, mod_spec, mod_spec] + cast_in,
        out_specs=[pl.BlockSpec((tm, d), lambda i: (i, 0)),
                   pl.BlockSpec((tm, d), lambda i: (i, 0))] + cast_out,
        out_shape=[jax.ShapeDtypeStruct((m, d), F32),
                   jax.ShapeDtypeStruct((m, d), BF16)] + cast_shapes,
        compiler_params=_params(("arbitrary",)),
        name="outproj",
    )(y_a, y_b, w_out, w_out, x2d, gate, post_g, ffn_g, scale_f, shift_f, *cast_weights)


def _conv_gelu(a, cw, cb):
    hw = GRID_W
    ext = a.shape[0]
    tm = ext - 2 * hw
    col = lax.broadcasted_iota(jnp.int32, (ext, 1), 0) % hw
    left = jnp.where(col > 0, pltpu.roll(a, 1, 0), 0.0)
    right = jnp.where(col < hw - 1, pltpu.roll(a, ext - 1, 0), 0.0)
    acc = jnp.broadcast_to(cb, (tm, a.shape[1]))
    for dr in range(3):
        lo = dr * hw
        acc = acc + cw[3 * dr + 0:3 * dr + 1, :] * left[lo:lo + tm, :]
        acc = acc + cw[3 * dr + 1:3 * dr + 2, :] * a[lo:lo + tm, :]
        acc = acc + cw[3 * dr + 2:3 * dr + 3, :] * right[lo:lo + tm, :]
    return _gelu(acc)


def _ffn_up_kernel(f_ref, fp_ref, fn_ref, wa_ref, wv_ref, cw_ref, cb_ref, *rest, tiles_per_image):
    n_cast = (len(rest) - 2) // 2
    g_ref, fext_scr = rest[n_cast], rest[-1]
    _cast_slabs(rest[:n_cast], rest[n_cast + 1:-1])
    i = pl.program_id(0)
    j = pl.program_id(1)
    tm = f_ref.shape[0]
    hw = GRID_W

    @pl.when(j == 0)
    def _():
        top = (i % tiles_per_image) == 0
        bottom = (i % tiles_per_image) == tiles_per_image - 1
        fext_scr[pl.ds(0, hw), :] = jnp.where(top, jnp.zeros_like(fp_ref), fp_ref[...])
        fext_scr[pl.ds(hw, tm), :] = f_ref[...]
        fext_scr[pl.ds(hw + tm, hw), :] = jnp.where(bottom, jnp.zeros_like(fn_ref), fn_ref[...])

    a = _dot(fext_scr[...], wa_ref[...])
    val = _dot(f_ref[...], wv_ref[...])
    g_ref[...] = (_conv_gelu(a, cw_ref[...], cb_ref[...]) * val).astype(g_ref.dtype)


def _ffn_up(f2d, w_up, conv_w, conv_b, cast_weights, rows_per_image, tm=1024, tf=512):
    m, d = f2d.shape
    d_ff = w_up.shape[1] // 2
    assert rows_per_image % tm == 0 and tm % GRID_W == 0 and d_ff % tf == 0
    hb = tm // GRID_W
    n_halo = m // GRID_W
    n_j = d_ff // tf
    cast_in, cast_out, cast_shapes = _cast_specs(cast_weights, (m // tm) * n_j, lambda i, j: i * n_j + j)
    return pl.pallas_call(
        functools.partial(_ffn_up_kernel, tiles_per_image=rows_per_image // tm),
        grid=(m // tm, n_j),
        in_specs=[pl.BlockSpec((tm, d), lambda i, j: (i, 0)),
                  pl.BlockSpec((GRID_W, d), lambda i, j: (jnp.maximum(i * hb - 1, 0), 0)),
                  pl.BlockSpec((GRID_W, d), lambda i, j: (jnp.minimum((i + 1) * hb, n_halo - 1), 0)),
                  pl.BlockSpec((d, tf), lambda i, j: (0, j)),
                  pl.BlockSpec((d, tf), lambda i, j: (0, n_j + j)),
                  pl.BlockSpec((9, tf), lambda i, j: (0, j)),
                  pl.BlockSpec((1, tf), lambda i, j: (0, j))] + cast_in,
        out_specs=[pl.BlockSpec((tm, tf), lambda i, j: (i, j))] + cast_out,
        out_shape=[jax.ShapeDtypeStruct((m, d_ff), BF16)] + cast_shapes,
        scratch_shapes=[pltpu.VMEM((tm + 2 * GRID_W, d), BF16)],
        compiler_params=_params(("arbitrary", "arbitrary")),
        name="ffn_up",
    )(f2d, f2d, f2d, w_up, w_up, conv_w, conv_b, *cast_weights)


def _ffn_down_kernel(g_ref, w_ref, h_ref, gt_ref, pg_ref, o_ref):
    for r0 in range(0, o_ref.shape[0], ROW_SUBTILE):
        rs = pl.ds(r0, ROW_SUBTILE)
        y = _dot(g_ref[rs, :], w_ref[...])
        o_ref[rs, :] = h_ref[rs, :] + gt_ref[0] * _rms(y, pg_ref[...])


def _ffn_down(g2d, w_down, h2d, gate, post_g, rows_per_mod, tm=512):
    m, d = h2d.shape
    d_ff = g2d.shape[1]
    assert m % tm == 0
    tiles_per_mod = rows_per_mod // tm
    return pl.pallas_call(
        _ffn_down_kernel,
        grid=(m // tm,),
        in_specs=[pl.BlockSpec((tm, d_ff), lambda i: (i, 0)),
                  pl.BlockSpec((d_ff, d), lambda i: (0, 0), pipeline_mode=pl.Buffered(1)),
                  pl.BlockSpec((tm, d), lambda i: (i, 0)),
                  pl.BlockSpec((1, 1, d), lambda i: (i // tiles_per_mod, 0, 0)),
                  pl.BlockSpec((1, d), lambda i: (0, 0))],
        out_specs=pl.BlockSpec((tm, d), lambda i: (i, 0)),
        out_shape=jax.ShapeDtypeStruct((m, d), F32),
        compiler_params=pltpu.CompilerParams(dimension_semantics=("arbitrary",),
                                             vmem_limit_bytes=FFN_DOWN_VMEM_LIMIT),
        name="ffn_down",
    )(g2d, w_down, h2d, gate, post_g)


def _gate_operands(w_f, b_f, w_b, b_b):
    r, kdim = w_f.shape
    wp = jnp.zeros((LANES, GLA_HEADS, 2, GLA_DK), BF16)
    wp = wp.at[:r, :, 0].set(w_f.astype(BF16).reshape(r, GLA_HEADS, GLA_DK))
    wp = wp.at[r:2 * r, :, 1].set(w_b.astype(BF16).reshape(r, GLA_HEADS, GLA_DK))
    bias = jnp.stack([b_f.reshape(GLA_HEADS, GLA_DK), b_b.reshape(GLA_HEADS, GLA_DK)], axis=1)
    return wp.reshape(LANES, 2 * kdim), bias.reshape(1, 2 * kdim)


def kernel(x, c, ctx, c_ctx, ada_w, ada_b, pre_mix_g, post_mix_g, pre_ffn_g, post_ffn_g, w_in, sgu_ln_g, sgu_ln_b, sgu_w, sgu_b, gla_gate_w_f, gla_gate_b_f, gla_gate_w_b, gla_gate_b_b, gla_norm_g, w_out, ffn_w_up, ffn_conv_w, ffn_conv_b, ffn_w_down):
    bsz, l, d = x.shape
    lc = ctx.shape[1]
    assert ada_w.shape[0] == 1, "single-layer kernel"
    d_ff = ffn_w_down.shape[1]

    c_rows = jnp.zeros((8, d), F32).at[:bsz].set(c).at[bsz].set(c_ctx)
    mod = _modulation(c_rows, ada_w[0], ada_b)
    sh_m, sc_m, gt_m, sh_f, sc_f, gt_f = [mod[:bsz, t * d:(t + 1) * d].reshape(bsz, 1, d) for t in range(N_MOD)]
    csh_m = mod[bsz:bsz + 1, 0:d].reshape(1, 1, d)
    csc_m = mod[bsz:bsz + 1, d:2 * d].reshape(1, 1, d)

    w_in_t = jnp.swapaxes(w_in[0], 0, 1).astype(BF16)
    w_lr_t = jnp.pad(w_in_t[COL_LR:], ((0, LANES - 2 * GLA_LOWRANK), (0, 0)))

    p_ctx, lr_ctx = _inproj(ctx.reshape(bsz * lc, d), csh_m, csc_m, pre_mix_g, w_in_t, w_lr_t,
                            ("none",) * 3, tm=lc, tn=512, first_col=COL_K, rows_per_mod=bsz * lc)
    x2d = x.reshape(bsz * l, d)
    p_lat, lr_lat = _inproj_rows(x2d, sh_m, sc_m, pre_mix_g, w_in_t, w_lr_t,
                                 ("gelu", "gelu", "none", "none", "silu"), tm=512, tn=1024, rows_per_mod=l)

    y_b = _gla(p_lat.reshape(bsz, l, N_MAIN), lr_lat.reshape(bsz, l, LANES),
               p_ctx.reshape(bsz, lc, -1), lr_ctx.reshape(bsz, lc, LANES),
               *_gate_operands(gla_gate_w_f[0], gla_gate_b_f[0], gla_gate_w_b[0], gla_gate_b_b[0]), gla_norm_g)
    y_a, w_out_bf = _sgu(p_lat, sgu_ln_g, sgu_ln_b, sgu_w[0].astype(BF16), sgu_b[0][:, :, None],
                         cast_weights=(w_out[0],))
    h, f, w_up_bf = _outproj(y_a, y_b.reshape(bsz * l, MIX_B), w_out_bf, x2d, gt_m,
                             post_mix_g, pre_ffn_g, sc_f, sh_f, cast_weights=(ffn_w_up[0],), rows_per_mod=l)
    g, w_down_bf = _ffn_up(f, w_up_bf, ffn_conv_w[0].reshape(9, d_ff), ffn_conv_b,
                           cast_weights=(ffn_w_down[0],), rows_per_image=l)
    out = _ffn_down(g, w_down_bf, h, gt_f, post_ffn_g, rows_per_mod=l)
    return out.reshape(bsz, l, d)
```

```python
import functools

import jax
import jax.numpy as jnp
from jax import lax
from jax.experimental import pallas as pl
from jax.experimental.pallas import tpu as pltpu

F32 = jnp.float32
BF16 = jnp.bfloat16

EPS = 1e-6
GRID_W = 64
MIX_A = 1024
MIX_B = 1024
SGU_GROUPS = 4
SGU_CHUNK = 128
SGU_GW = MIX_A // SGU_GROUPS
GLA_HEADS = 4
GLA_DK = 128
GLA_DV = 256
GLA_KDIM = GLA_HEADS * GLA_DK
GLA_LOWRANK = 16
GLA_NORMALIZER = 16.0
GLA_CHUNK = 64
N_MOD = 6
LANES = 128
ROW_SUBTILE = 256
VMEM_LIMIT = 56 * 1024 * 1024
FFN_DOWN_VMEM_LIMIT = 60 * 1024 * 1024

COL_U, COL_VS, COL_Q, COL_K, COL_V, COL_R, COL_LR = 0, 1024, 2048, 2560, 3072, 4096, 5120
N_MAIN = COL_LR
P_Q, P_K, P_V, P_R, P_COLS = 0, 512, 1024, 2048, 3072


def _dot(a, b):
    return jnp.dot(a, b, preferred_element_type=F32)


def _nt_dot(a, b):
    return lax.dot_general(a, b, (((1,), (1,)), ((), ())), preferred_element_type=F32)


def _silu(x):
    return x / (1.0 + jnp.exp(-x))


_GELU_A = -2.0 * 0.7978845608028654 * 1.4426950408889634
_GELU_B = _GELU_A * 0.044715


def _gelu(x):
    return x / (1.0 + jnp.exp2((_GELU_A + _GELU_B * (x * x)) * x))


def _params(sem):
    return pltpu.CompilerParams(dimension_semantics=sem, vmem_limit_bytes=VMEM_LIMIT)


def _mod_kernel(c_ref, w_ref, b_ref, o_ref):
    s = _silu(c_ref[...]).astype(BF16)
    o_ref[...] = _dot(s, w_ref[...].astype(BF16)) + b_ref[...]


def _modulation(c_rows, ada_w, ada_b, tn=1024):
    rows, d = c_rows.shape
    n = ada_w.shape[1]
    return pl.pallas_call(
        _mod_kernel,
        grid=(n // tn,),
        in_specs=[pl.BlockSpec((rows, d), lambda j: (0, 0)),
                  pl.BlockSpec((d, tn), lambda j: (0, j)),
                  pl.BlockSpec((1, tn), lambda j: (0, j))],
        out_specs=pl.BlockSpec((rows, tn), lambda j: (0, j)),
        out_shape=jax.ShapeDtypeStruct((rows, n), F32),
        compiler_params=_params(("arbitrary",)),
        name="mod",
    )(c_rows, ada_w, ada_b)


def _inproj_kernel(x_ref, sh_ref, sc_ref, g_ref, w_ref, wlr_ref, p_ref, lr_ref, a_scr, *, acts):
    j = pl.program_id(1)

    @pl.when(j == 0)
    def _():
        x = x_ref[...]
        y = x * lax.rsqrt(jnp.mean(x * x, axis=-1, keepdims=True) + EPS) * g_ref[...]
        a = (y * (1.0 + sc_ref[0]) + sh_ref[0]).astype(BF16)
        a_scr[...] = a
        lr_ref[...] = _nt_dot(a, wlr_ref[...])

    acc = _nt_dot(a_scr[...], w_ref[...])
    fns = {"gelu": _gelu, "silu": _silu, "none": lambda t: t}
    for name in sorted(set(acts)):
        idx = [t for t, a in enumerate(acts) if a == name]
        cond = functools.reduce(jnp.logical_or, [j == t for t in idx])

        @pl.when(cond)
        def _(name=name):
            p_ref[...] = fns[name](acc).astype(p_ref.dtype)


def _inproj(x2d, shift, scale, gain, w_t, w_lr_t, acts, tm, tn, first_col, rows_per_mod):
    m, d = x2d.shape
    n = len(acts) * tn
    assert m % tm == 0 and first_col % tn == 0 and rows_per_mod % tm == 0
    tiles_per_mod = rows_per_mod // tm
    blk0 = first_col // tn
    return pl.pallas_call(
        functools.partial(_inproj_kernel, acts=acts),
        grid=(m // tm, len(acts)),
        in_specs=[pl.BlockSpec((tm, d), lambda i, j: (i, 0)),
                  pl.BlockSpec((1, 1, d), lambda i, j: (i // tiles_per_mod, 0, 0)),
                  pl.BlockSpec((1, 1, d), lambda i, j: (i // tiles_per_mod, 0, 0)),
                  pl.BlockSpec((1, d), lambda i, j: (0, 0)),
                  pl.BlockSpec((tn, d), lambda i, j: (blk0 + j, 0)),
                  pl.BlockSpec((LANES, d), lambda i, j: (0, 0))],
        out_specs=[pl.BlockSpec((tm, tn), lambda i, j: (i, j)),
                   pl.BlockSpec((tm, LANES), lambda i, j: (i, 0))],
        out_shape=[jax.ShapeDtypeStruct((m, n), BF16),
                   jax.ShapeDtypeStruct((m, LANES), F32)],
        scratch_shapes=[pltpu.VMEM((tm, d), BF16)],
        compiler_params=_params(("parallel", "arbitrary")),
        name="inproj",
    )(x2d, shift, scale, gain, w_t, w_lr_t)


def _spatial_gating(u, v, lg_ref, lb_ref, ws_ref, bs_ref, ya_ref):
    for r0 in range(0, u.shape[0], SGU_CHUNK):
        for g in range(SGU_GROUPS):
            c0 = g * SGU_GW
            vg = v[r0:r0 + SGU_CHUNK, c0:c0 + SGU_GW]
            vc = vg - jnp.mean(vg, axis=-1, keepdims=True)
            var = jnp.mean(vc * vc, axis=-1, keepdims=True)
            cs = pl.ds(c0, SGU_GW)
            vn = vc * lax.rsqrt(var + EPS) * lg_ref[:, cs] + lb_ref[:, cs]
            s = _dot(ws_ref[g], vn.astype(BF16)) + bs_ref[g]
            ya_ref[pl.ds(r0, SGU_CHUNK), cs] = (u[r0:r0 + SGU_CHUNK, c0:c0 + SGU_GW] * s).astype(ya_ref.dtype)


def _inproj_rows_kernel(x_ref, sh_ref, sc_ref, g_ref, w_ref, wlr_ref, lg_ref, lb_ref, ws_ref, bs_ref, *rest):
    n_cast = (len(rest) - 3) // 2
    p_ref, lr_ref, ya_ref = rest[n_cast:n_cast + 3]
    _cast_slabs(rest[:n_cast], rest[n_cast + 3:])
    x = x_ref[...]
    y = x * lax.rsqrt(jnp.mean(x * x, axis=-1, keepdims=True) + EPS) * g_ref[...]
    a = (y * (1.0 + sc_ref[0]) + sh_ref[0]).astype(BF16)
    lr_ref[...] = _nt_dot(a, wlr_ref[...])
    u = _gelu(_nt_dot(a, w_ref[pl.ds(COL_U, MIX_A), :]))
    v = _gelu(_nt_dot(a, w_ref[pl.ds(COL_VS, MIX_A), :]))
    _spatial_gating(u, v, lg_ref, lb_ref, ws_ref, bs_ref, ya_ref)
    qk = pl.ds(COL_Q, 2 * GLA_KDIM)
    p_ref[:, pl.ds(P_Q, 2 * GLA_KDIM)] = _nt_dot(a, w_ref[qk, :]).astype(p_ref.dtype)
    p_ref[:, pl.ds(P_V, MIX_B)] = _nt_dot(a, w_ref[pl.ds(COL_V, MIX_B), :]).astype(p_ref.dtype)
    p_ref[:, pl.ds(P_R, MIX_B)] = _silu(_nt_dot(a, w_ref[pl.ds(COL_R, MIX_B), :])).astype(p_ref.dtype)


def _inproj_rows(x2d, shift, scale, gain, w_t, w_lr_t, ln_g, ln_b, w_s, b_s, cast_weights, tm, rows_per_mod):
    m, d = x2d.shape
    assert m % tm == 0 and rows_per_mod % tm == 0 and tm % SGU_CHUNK == 0
    tiles_per_mod = rows_per_mod // tm
    cast_in, cast_out, cast_shapes = _cast_specs(cast_weights, m // tm)
    whole = lambda shape: pl.BlockSpec(shape, lambda i: (0,) * len(shape))
    return pl.pallas_call(
        _inproj_rows_kernel,
        grid=(m // tm,),
        in_specs=[pl.BlockSpec((tm, d), lambda i: (i, 0)),
                  pl.BlockSpec((1, 1, d), lambda i: (i // tiles_per_mod, 0, 0)),
                  pl.BlockSpec((1, 1, d), lambda i: (i // tiles_per_mod, 0, 0)),
                  whole((1, d)),
                  pl.BlockSpec((N_MAIN, d), lambda i: (0, 0), pipeline_mode=pl.Buffered(1)),
                  pl.BlockSpec((LANES, d), lambda i: (0, 0), pipeline_mode=pl.Buffered(1)),
                  whole((1, MIX_A)), whole((1, MIX_A)),
                  whole((SGU_GROUPS, SGU_CHUNK, SGU_CHUNK)), whole((SGU_GROUPS, SGU_CHUNK, 1))] + cast_in,
        out_specs=[pl.BlockSpec((tm, P_COLS), lambda i: (i, 0)),
                   pl.BlockSpec((tm, LANES), lambda i: (i, 0)),
                   pl.BlockSpec((tm, MIX_A), lambda i: (i, 0))] + cast_out,
        out_shape=[jax.ShapeDtypeStruct((m, P_COLS), BF16),
                   jax.ShapeDtypeStruct((m, LANES), F32),
                   jax.ShapeDtypeStruct((m, MIX_A), BF16)] + cast_shapes,
        compiler_params=_params(("arbitrary",)),
        name="inproj_rows",
    )(x2d, shift, scale, gain, w_t, w_lr_t, ln_g, ln_b, w_s, b_s, *cast_weights)


GLA_BLOCK = 4 * GLA_CHUNK
GLA_PAIR = 2 * GLA_CHUNK
GLA_GATE_BLOCKS = 2


def _log_decay(z):
    log_sig = -(jnp.maximum(-z, 0.0) + jnp.log(1.0 + jnp.exp(-jnp.abs(z))))
    return log_sig * (1.0 / GLA_NORMALIZER)


def _split(x):
    hi = x.astype(BF16)
    return hi, (x - hi.astype(F32)).astype(BF16)


def _gla_masks():
    n, c = GLA_BLOCK, GLA_CHUNK
    r = jnp.arange(n)[:, None]
    s = jnp.arange(n)[None, :]
    same = (r // c) == (s // c)
    return ((same & (s <= r)).astype(BF16), (same & (s >= r)).astype(BF16), jnp.eye(GLA_DV, dtype=BF16))


SUBLANES = 8


def _chunk_rows(blk, cidx, rows):
    return pl.ds(pl.multiple_of((blk * (GLA_BLOCK // GLA_CHUNK) + cidx) * rows, rows), rows)


def _gla_gates(rs, blk, q_ref, k_ref, v_ref, lr_ref, w_ref, b_ref, lower_ref, eye_ref, vt_scr, fwd, bwd):
    c, dk = GLA_CHUNK, GLA_DK
    vt_scr[:, rs] = _nt_dot(eye_ref[...], v_ref[rs, :]).astype(BF16)
    g = _log_decay(_dot(lr_ref[rs, :].astype(BF16), w_ref[...]) + b_ref[...])
    g_hi, g_lo = _split(g)
    nb = g.shape[0]
    p = jnp.concatenate([_dot(lower_ref[...], g_hi[r0:r0 + GLA_BLOCK]) + _dot(lower_ref[...], g_lo[r0:r0 + GLA_BLOCK])
                         for r0 in range(0, nb, GLA_BLOCK)], axis=0)
    last = [p[i * c + c - 1:(i + 1) * c, :] for i in range(nb // c)]
    tot = jnp.concatenate([jnp.broadcast_to(t, (c, 2 * dk)) for t in last], axis=0)
    k32 = k_ref[rs, :].astype(F32)
    q32 = None if q_ref is None else q_ref[rs, :].astype(F32) * (GLA_DK ** -0.5)
    plans = ((fwd, slice(0, dk), p, tot - p), (bwd, slice(dk, 2 * dk), tot - p + g, p - g))
    for d, lanes, b_cum, to_end in plans:
        d["kd"][rs, :] = (k32 * jnp.exp(to_end[:, lanes])).astype(BF16)
        for cidx in range(nb // c):
            dec = jnp.exp(last[cidx][:, lanes])
            d["dec"][_chunk_rows(blk, cidx, SUBLANES), :] = jnp.broadcast_to(dec, (SUBLANES, dk))
        if q32 is not None:
            d["qe"][rs, :] = (q32 * jnp.exp(b_cum[:, lanes])).astype(BF16)
            d["ke"][rs, :] = (k32 * jnp.exp(-b_cum[:, lanes])).astype(BF16)


def _gla_scan_block(rs, blk, order, d, vt_scr, s_scr, record):
    c = GLA_CHUNK
    v_t = vt_scr[:, rs]
    kd = d["kd"][rs, :]
    zeros = jnp.zeros((c, kd.shape[1]), kd.dtype)
    s = s_scr[...]
    for cidx in order:
        pair, half = divmod(cidx, 2)
        slab = v_t[:, pair * GLA_PAIR:(pair + 1) * GLA_PAIR]
        kc = kd[cidx * c:(cidx + 1) * c]
        k_only = jnp.concatenate([kc, zeros] if half == 0 else [zeros, kc], axis=0)
        kv_t = _dot(slab, k_only)
        if record:
            d["snap"][_chunk_rows(blk, cidx, GLA_DV), :] = s.astype(BF16)
        dec = d["dec"][_chunk_rows(blk, cidx, SUBLANES), :]
        s = s * dec[0:1, :] + kv_t
    s_scr[...] = s


def _gla_output(rs, blk, v_ref, r_ref, dirs, ng_ref, y_ref):
    c = GLA_CHUNK
    v = v_ref[rs, :]
    o = None
    for d in dirs:
        q_e = d["qe"][rs, :]
        att = jnp.where(d["tri"][...] > 0, _nt_dot(q_e, d["ke"][rs, :]), 0.0).astype(BF16)
        inter = [_nt_dot(q_e[cidx * c:(cidx + 1) * c], d["snap"][_chunk_rows(blk, cidx, GLA_DV), :])
                 for cidx in range(GLA_BLOCK // c)]
        od = _dot(att, v) + jnp.concatenate(inter, axis=0)
        o = od if o is None else o + od
    o = o * lax.rsqrt(jnp.mean(o * o, axis=-1, keepdims=True) + EPS) * ng_ref[...]
    y_ref[rs, :] = (o * r_ref[rs, :].astype(F32)).astype(y_ref.dtype)


def _gla_kernel(q_ref, k_ref, v_ref, r_ref, lr_ref, kc_ref, vc_ref, lrc_ref, w_ref, b_ref, ng_ref,
                lower_ref, upper_ref, eye_ref,
                y_ref, sf_scr, sb_scr, vt_scr, qef_scr, qeb_scr, kef_scr, keb_scr, kdf_scr, kdb_scr,
                decf_scr, decb_scr, snapf_scr, snapb_scr):
    fwd = dict(tri=lower_ref, qe=qef_scr, ke=kef_scr, kd=kdf_scr, dec=decf_scr, snap=snapf_scr)
    bwd = dict(tri=upper_ref, qe=qeb_scr, ke=keb_scr, kd=kdb_scr, dec=decb_scr, snap=snapb_scr)
    dirs = (fwd, bwd)
    l, lc = q_ref.shape[0], kc_ref.shape[0]
    n_blk, n_cblk = l // GLA_BLOCK, lc // GLA_BLOCK
    asc = tuple(range(GLA_BLOCK // GLA_CHUNK))
    desc = asc[::-1]

    def block_rows(j):
        return pl.ds(pl.multiple_of(j * GLA_BLOCK, GLA_BLOCK), GLA_BLOCK)

    sf_scr[...] = jnp.zeros_like(sf_scr)
    sb_scr[...] = jnp.zeros_like(sb_scr)

    for j in range(n_cblk):
        _gla_gates(block_rows(j), j, None, kc_ref, vc_ref, lrc_ref, w_ref, b_ref, lower_ref, eye_ref,
                   vt_scr, fwd, bwd)
    for j in range(n_cblk):
        _gla_scan_block(block_rows(j), j, asc, fwd, vt_scr, sf_scr, False)
        jb = n_cblk - 1 - j
        _gla_scan_block(block_rows(jb), jb, desc, bwd, vt_scr, sb_scr, False)

    def gates(j, carry):
        rows = GLA_GATE_BLOCKS * GLA_BLOCK
        rs = pl.ds(pl.multiple_of(j * rows, rows), rows)
        _gla_gates(rs, j * GLA_GATE_BLOCKS, q_ref, k_ref, v_ref, lr_ref, w_ref, b_ref, lower_ref, eye_ref,
                   vt_scr, fwd, bwd)
        return carry

    def scan(j, carry):
        _gla_scan_block(block_rows(j), j, asc, fwd, vt_scr, sf_scr, True)
        jb = n_blk - 1 - j
        _gla_scan_block(block_rows(jb), jb, desc, bwd, vt_scr, sb_scr, True)
        return carry

    def output(j, carry):
        _gla_output(block_rows(j), j, v_ref, r_ref, dirs, ng_ref, y_ref)
        return carry

    lax.fori_loop(0, n_blk // GLA_GATE_BLOCKS, gates, 0)
    lax.fori_loop(0, n_blk, scan, 0, unroll=4)
    lax.fori_loop(0, n_blk, output, 0, unroll=4)


def _gla(p_lat, lr_lat, p_ctx, lr_ctx, gate_w, gate_b, norm_g):
    bsz, l, _ = p_lat.shape
    lc = p_ctx.shape[1]
    assert l % GLA_BLOCK == 0 and lc % GLA_BLOCK == 0 and lc <= l
    qb, kb = P_Q // GLA_DK, P_K // GLA_DK
    vb, rb = P_V // GLA_DV, P_R // GLA_DV
    gate_specs = [pl.BlockSpec((LANES, 2 * GLA_DK), lambda b, h: (0, h)),
                  pl.BlockSpec((1, 2 * GLA_DK), lambda b, h: (0, h))]
    masks = _gla_masks()
    mask_specs = [pl.BlockSpec(m.shape, lambda b, h: (0, 0)) for m in masks]
    n_chunks = l // GLA_CHUNK
    return pl.pallas_call(
        _gla_kernel,
        grid=(bsz, GLA_HEADS),
        in_specs=[pl.BlockSpec((None, l, GLA_DK), lambda b, h: (b, 0, qb + h)),
                  pl.BlockSpec((None, l, GLA_DK), lambda b, h: (b, 0, kb + h)),
                  pl.BlockSpec((None, l, GLA_DV), lambda b, h: (b, 0, vb + h)),
                  pl.BlockSpec((None, l, GLA_DV), lambda b, h: (b, 0, rb + h)),
                  pl.BlockSpec((None, l, LANES), lambda b, h: (b, 0, 0)),
                  pl.BlockSpec((None, lc, GLA_DK), lambda b, h: (b, 0, h)),
                  pl.BlockSpec((None, lc, GLA_DV), lambda b, h: (b, 0, GLA_KDIM // GLA_DV + h)),
                  pl.BlockSpec((None, lc, LANES), lambda b, h: (b, 0, 0))]
                 + gate_specs
                 + [pl.BlockSpec((1, GLA_DV), lambda b, h: (0, 0))] + mask_specs,
        out_specs=pl.BlockSpec((None, l, GLA_DV), lambda b, h: (b, 0, h)),
        out_shape=jax.ShapeDtypeStruct((bsz, l, MIX_B), BF16),
        scratch_shapes=[pltpu.VMEM((GLA_DV, GLA_DK), F32)] * 2
                       + [pltpu.VMEM((GLA_DV, l), BF16)]
                       + [pltpu.VMEM((l, GLA_DK), BF16)] * 6
                       + [pltpu.VMEM((n_chunks * SUBLANES, GLA_DK), F32)] * 2
                       + [pltpu.VMEM((n_chunks * GLA_DV, GLA_DK), BF16)] * 2,
        compiler_params=_params(("parallel", "arbitrary")),
        name="gla",
    )(p_lat, p_lat, p_lat, p_lat, lr_lat, p_ctx, p_ctx, lr_ctx, gate_w, gate_b, norm_g, *masks)


def _cast_specs(weights, n_steps, step_index=lambda i: i):
    ins, outs, shapes = [], [], []
    for w in weights:
        rows, cols = w.shape
        assert rows % (n_steps * 16) == 0, "slabs must be whole packed bf16 row tiles"
        spec = pl.BlockSpec((rows // n_steps, cols), lambda *idx: (step_index(*idx), 0))
        ins.append(spec)
        outs.append(spec)
        shapes.append(jax.ShapeDtypeStruct(w.shape, BF16))
    return ins, outs, shapes


def _cast_slabs(src_refs, dst_refs):
    for src, dst in zip(src_refs, dst_refs):
        dst[...] = src[...].astype(dst.dtype)


def _rms(x, g):
    return x * lax.rsqrt(jnp.mean(x * x, axis=-1, keepdims=True) + EPS) * g


def _outproj_kernel(ya_ref, yb_ref, wa_ref, wb_ref, x_ref, gt_ref, pg_ref, fg_ref, sc_ref, sh_ref, *rest):
    n_cast = (len(rest) - 2) // 2
    h_ref, f_ref = rest[n_cast:n_cast + 2]
    _cast_slabs(rest[:n_cast], rest[n_cast + 2:])
    for r0 in range(0, h_ref.shape[0], ROW_SUBTILE):
        rs = pl.ds(r0, ROW_SUBTILE)
        y = _dot(ya_ref[rs, :], wa_ref[...]) + _dot(yb_ref[rs, :], wb_ref[...])
        h = x_ref[rs, :] + gt_ref[0] * _rms(y, pg_ref[...])
        h_ref[rs, :] = h
        f_ref[rs, :] = (_rms(h, fg_ref[...]) * (1.0 + sc_ref[0]) + sh_ref[0]).astype(f_ref.dtype)


def _outproj(y_a, y_b, w_out, x2d, gate, post_g, ffn_g, scale_f, shift_f, cast_weights, rows_per_mod, tm=512):
    m, d = x2d.shape
    tiles_per_mod = rows_per_mod // tm
    mod_spec = pl.BlockSpec((1, 1, d), lambda i: (i // tiles_per_mod, 0, 0))
    vec_spec = pl.BlockSpec((1, d), lambda i: (0, 0))
    cast_in, cast_out, cast_shapes = _cast_specs(cast_weights, m // tm)
    return pl.pallas_call(
        _outproj_kernel,
        grid=(m // tm,),
        in_specs=[pl.BlockSpec((tm, MIX_A), lambda i: (i, 0)),
                  pl.BlockSpec((tm, MIX_B), lambda i: (i, 0)),
                  pl.BlockSpec((MIX_A, d), lambda i: (0, 0), pipeline_mode=pl.Buffered(1)),
                  pl.BlockSpec((MIX_B, d), lambda i: (MIX_A // MIX_B, 0), pipeline_mode=pl.Buffered(1)),
                  pl.BlockSpec((tm, d), lambda i: (i, 0)),
                  mod_spec, vec_spec, vec_spec, mod_spec, mod_spec] + cast_in,
        out_specs=[pl.BlockSpec((tm, d), lambda i: (i, 0)),
                   pl.BlockSpec((tm, d), lambda i: (i, 0))] + cast_out,
        out_shape=[jax.ShapeDtypeStruct((m, d), F32),
                   jax.ShapeDtypeStruct((m, d), BF16)] + cast_shapes,
        compiler_params=_params(("arbitrary",)),
        name="outproj",
    )(y_a, y_b, w_out, w_out, x2d, gate, post_g, ffn_g, scale_f, shift_f, *cast_weights)


def _conv_gelu(a, cw, cb):
    hw = GRID_W
    ext = a.shape[0]
    tm = ext - 2 * hw
    col = lax.broadcasted_iota(jnp.int32, (ext, 1), 0) % hw
    left = jnp.where(col > 0, pltpu.roll(a, 1, 0), 0.0)
    right = jnp.where(col < hw - 1, pltpu.roll(a, ext - 1, 0), 0.0)
    acc = jnp.broadcast_to(cb, (tm, a.shape[1]))
    for dr in range(3):
        lo = dr * hw
        acc = acc + cw[3 * dr + 0:3 * dr + 1, :] * left[lo:lo + tm, :]
        acc = acc + cw[3 * dr + 1:3 * dr + 2, :] * a[lo:lo + tm, :]
        acc = acc + cw[3 * dr + 2:3 * dr + 3, :] * right[lo:lo + tm, :]
    return _gelu(acc)


def _ffn_up_kernel(f_ref, fp_ref, fn_ref, wa_ref, wv_ref, cw_ref, cb_ref, *rest, tiles_per_image):
    n_cast = (len(rest) - 2) // 2
    g_ref, fext_scr = rest[n_cast], rest[-1]
    _cast_slabs(rest[:n_cast], rest[n_cast + 1:-1])
    i = pl.program_id(0)
    j = pl.program_id(1)
    tm = f_ref.shape[0]
    hw = GRID_W

    @pl.when(j == 0)
    def _():
        top = (i % tiles_per_image) == 0
        bottom = (i % tiles_per_image) == tiles_per_image - 1
        fext_scr[pl.ds(0, hw), :] = jnp.where(top, jnp.zeros_like(fp_ref), fp_ref[...])
        fext_scr[pl.ds(hw, tm), :] = f_ref[...]
        fext_scr[pl.ds(hw + tm, hw), :] = jnp.where(bottom, jnp.zeros_like(fn_ref), fn_ref[...])

    a = _dot(fext_scr[...], wa_ref[...])
    val = _dot(f_ref[...], wv_ref[...])
    g_ref[...] = (_conv_gelu(a, cw_ref[...], cb_ref[...]) * val).astype(g_ref.dtype)


def _ffn_up(f2d, w_up, conv_w, conv_b, cast_weights, rows_per_image, tm=1024, tf=512):
    m, d = f2d.shape
    d_ff = w_up.shape[1] // 2
    assert rows_per_image % tm == 0 and tm % GRID_W == 0 and d_ff % tf == 0
    hb = tm // GRID_W
    n_halo = m // GRID_W
    n_j = d_ff // tf
    cast_in, cast_out, cast_shapes = _cast_specs(cast_weights, (m // tm) * n_j, lambda i, j: i * n_j + j)
    return pl.pallas_call(
        functools.partial(_ffn_up_kernel, tiles_per_image=rows_per_image // tm),
        grid=(m // tm, n_j),
        in_specs=[pl.BlockSpec((tm, d), lambda i, j: (i, 0)),
                  pl.BlockSpec((GRID_W, d), lambda i, j: (jnp.maximum(i * hb - 1, 0), 0)),
                  pl.BlockSpec((GRID_W, d), lambda i, j: (jnp.minimum((i + 1) * hb, n_halo - 1), 0)),
                  pl.BlockSpec((d, tf), lambda i, j: (0, j)),
                  pl.BlockSpec((d, tf), lambda i, j: (0, n_j + j)),
                  pl.BlockSpec((9, tf), lambda i, j: (0, j)),
                  pl.BlockSpec((1, tf), lambda i, j: (0, j))] + cast_in,
        out_specs=[pl.BlockSpec((tm, tf), lambda i, j: (i, j))] + cast_out,
        out_shape=[jax.ShapeDtypeStruct((m, d_ff), BF16)] + cast_shapes,
        scratch_shapes=[pltpu.VMEM((tm + 2 * GRID_W, d), BF16)],
        compiler_params=_params(("arbitrary", "arbitrary")),
        name="ffn_up",
    )(f2d, f2d, f2d, w_up, w_up, conv_w, conv_b, *cast_weights)


def _ffn_down_kernel(g_ref, w_ref, h_ref, gt_ref, pg_ref, o_ref):
    for r0 in range(0, o_ref.shape[0], ROW_SUBTILE):
        rs = pl.ds(r0, ROW_SUBTILE)
        y = _dot(g_ref[rs, :], w_ref[...])
        o_ref[rs, :] = h_ref[rs, :] + gt_ref[0] * _rms(y, pg_ref[...])


def _ffn_down(g2d, w_down, h2d, gate, post_g, rows_per_mod, tm=512):
    m, d = h2d.shape
    d_ff = g2d.shape[1]
    assert m % tm == 0
    tiles_per_mod = rows_per_mod // tm
    return pl.pallas_call(
        _ffn_down_kernel,
        grid=(m // tm,),
        in_specs=[pl.BlockSpec((tm, d_ff), lambda i: (i, 0)),
                  pl.BlockSpec((d_ff, d), lambda i: (0, 0), pipeline_mode=pl.Buffered(1)),
                  pl.BlockSpec((tm, d), lambda i: (i, 0)),
                  pl.BlockSpec((1, 1, d), lambda i: (i // tiles_per_mod, 0, 0)),
                  pl.BlockSpec((1, d), lambda i: (0, 0))],
        out_specs=pl.BlockSpec((tm, d), lambda i: (i, 0)),
        out_shape=jax.ShapeDtypeStruct((m, d), F32),
        compiler_params=pltpu.CompilerParams(dimension_semantics=("arbitrary",),
                                             vmem_limit_bytes=FFN_DOWN_VMEM_LIMIT),
        name="ffn_down",
    )(g2d, w_down, h2d, gate, post_g)


def _gate_operands(w_f, b_f, w_b, b_b):
    r, kdim = w_f.shape
    wp = jnp.zeros((LANES, GLA_HEADS, 2, GLA_DK), BF16)
    wp = wp.at[:r, :, 0].set(w_f.astype(BF16).reshape(r, GLA_HEADS, GLA_DK))
    wp = wp.at[r:2 * r, :, 1].set(w_b.astype(BF16).reshape(r, GLA_HEADS, GLA_DK))
    bias = jnp.stack([b_f.reshape(GLA_HEADS, GLA_DK), b_b.reshape(GLA_HEADS, GLA_DK)], axis=1)
    return wp.reshape(LANES, 2 * kdim), bias.reshape(1, 2 * kdim)


def kernel(x, c, ctx, c_ctx, ada_w, ada_b, pre_mix_g, post_mix_g, pre_ffn_g, post_ffn_g, w_in, sgu_ln_g, sgu_ln_b, sgu_w, sgu_b, gla_gate_w_f, gla_gate_b_f, gla_gate_w_b, gla_gate_b_b, gla_norm_g, w_out, ffn_w_up, ffn_conv_w, ffn_conv_b, ffn_w_down):
    bsz, l, d = x.shape
    lc = ctx.shape[1]
    assert ada_w.shape[0] == 1, "single-layer kernel"
    d_ff = ffn_w_down.shape[1]

    c_rows = jnp.zeros((8, d), F32).at[:bsz].set(c).at[bsz].set(c_ctx)
    mod = _modulation(c_rows, ada_w[0], ada_b)
    sh_m, sc_m, gt_m, sh_f, sc_f, gt_f = [mod[:bsz, t * d:(t + 1) * d].reshape(bsz, 1, d) for t in range(N_MOD)]
    csh_m = mod[bsz:bsz + 1, 0:d].reshape(1, 1, d)
    csc_m = mod[bsz:bsz + 1, d:2 * d].reshape(1, 1, d)

    w_in_t = jnp.swapaxes(w_in[0], 0, 1).astype(BF16)
    w_lr_t = jnp.pad(w_in_t[COL_LR:], ((0, LANES - 2 * GLA_LOWRANK), (0, 0)))

    p_ctx, lr_ctx = _inproj(ctx.reshape(bsz * lc, d), csh_m, csc_m, pre_mix_g, w_in_t, w_lr_t,
                            ("none",) * 3, tm=lc, tn=512, first_col=COL_K, rows_per_mod=bsz * lc)
    x2d = x.reshape(bsz * l, d)
    p_lat, lr_lat, y_a, w_out_bf = _inproj_rows(
        x2d, sh_m, sc_m, pre_mix_g, w_in_t, w_lr_t, sgu_ln_g, sgu_ln_b, sgu_w[0].astype(BF16),
        sgu_b[0][:, :, None], cast_weights=(w_out[0],), tm=512, rows_per_mod=l)

    y_b = _gla(p_lat.reshape(bsz, l, P_COLS), lr_lat.reshape(bsz, l, LANES),
               p_ctx.reshape(bsz, lc, -1), lr_ctx.reshape(bsz, lc, LANES),
               *_gate_operands(gla_gate_w_f[0], gla_gate_b_f[0], gla_gate_w_b[0], gla_gate_b_b[0]), gla_norm_g)
    h, f, w_up_bf = _outproj(y_a, y_b.reshape(bsz * l, MIX_B), w_out_bf, x2d, gt_m,
                             post_mix_g, pre_ffn_g, sc_f, sh_f, cast_weights=(ffn_w_up[0],), rows_per_mod=l)
    g, w_down_bf = _ffn_up(f, w_up_bf, ffn_conv_w[0].reshape(9, d_ff), ffn_conv_b,
                           cast_weights=(ffn_w_down[0],), rows_per_image=l)
    out = _ffn_down(g, w_down_bf, h, gt_f, post_ffn_g, rows_per_mod=l)
    return out.reshape(bsz, l, d)
```

```python
import functools

import jax
import jax.numpy as jnp
from jax import lax
from jax.experimental import pallas as pl
from jax.experimental.pallas import tpu as pltpu

F32 = jnp.float32
BF16 = jnp.bfloat16

EPS = 1e-6
GRID_W = 64
MIX_A = 1024
MIX_B = 1024
SGU_GROUPS = 4
SGU_CHUNK = 128
SGU_GW = MIX_A // SGU_GROUPS
GLA_HEADS = 4
GLA_DK = 128
GLA_DV = 256
GLA_KDIM = GLA_HEADS * GLA_DK
GLA_LOWRANK = 16
GLA_NORMALIZER = 16.0
GLA_CHUNK = 64
N_MOD = 6
LANES = 128
ROW_SUBTILE = 256
VMEM_LIMIT = 56 * 1024 * 1024
FFN_DOWN_VMEM_LIMIT = 60 * 1024 * 1024

COL_U, COL_VS, COL_Q, COL_K, COL_V, COL_R, COL_LR = 0, 1024, 2048, 2560, 3072, 4096, 5120
N_MAIN = COL_LR
P_Q, P_K, P_V, P_R, P_HEAD = 0, 128, 256, 512, 768
P_COLS = GLA_HEADS * P_HEAD


def _dot(a, b):
    return jnp.dot(a, b, preferred_element_type=F32)


def _nt_dot(a, b):
    return lax.dot_general(a, b, (((1,), (1,)), ((), ())), preferred_element_type=F32)


def _silu(x):
    return x / (1.0 + jnp.exp(-x))


_GELU_A = -2.0 * 0.7978845608028654 * 1.4426950408889634
_GELU_B = _GELU_A * 0.044715


def _gelu(x):
    return x / (1.0 + jnp.exp2((_GELU_A + _GELU_B * (x * x)) * x))


def _params(sem):
    return pltpu.CompilerParams(dimension_semantics=sem, vmem_limit_bytes=VMEM_LIMIT)


def _mod_kernel(c_ref, w_ref, b_ref, o_ref):
    s = _silu(c_ref[...]).astype(BF16)
    o_ref[...] = _dot(s, w_ref[...].astype(BF16)) + b_ref[...]


def _modulation(c_rows, ada_w, ada_b, tn=1024):
    rows, d = c_rows.shape
    n = ada_w.shape[1]
    return pl.pallas_call(
        _mod_kernel,
        grid=(n // tn,),
        in_specs=[pl.BlockSpec((rows, d), lambda j: (0, 0)),
                  pl.BlockSpec((d, tn), lambda j: (0, j)),
                  pl.BlockSpec((1, tn), lambda j: (0, j))],
        out_specs=pl.BlockSpec((rows, tn), lambda j: (0, j)),
        out_shape=jax.ShapeDtypeStruct((rows, n), F32),
        compiler_params=_params(("arbitrary",)),
        name="mod",
    )(c_rows, ada_w, ada_b)


def _inproj_kernel(x_ref, sh_ref, sc_ref, g_ref, w_ref, wlr_ref, p_ref, lr_ref, a_scr, *, acts):
    j = pl.program_id(1)

    @pl.when(j == 0)
    def _():
        x = x_ref[...]
        y = x * lax.rsqrt(jnp.mean(x * x, axis=-1, keepdims=True) + EPS) * g_ref[...]
        a = (y * (1.0 + sc_ref[0]) + sh_ref[0]).astype(BF16)
        a_scr[...] = a
        lr_ref[...] = _nt_dot(a, wlr_ref[...])

    acc = _nt_dot(a_scr[...], w_ref[...])
    fns = {"gelu": _gelu, "silu": _silu, "none": lambda t: t}
    for name in sorted(set(acts)):
        idx = [t for t, a in enumerate(acts) if a == name]
        cond = functools.reduce(jnp.logical_or, [j == t for t in idx])

        @pl.when(cond)
        def _(name=name):
            p_ref[...] = fns[name](acc).astype(p_ref.dtype)


def _inproj(x2d, shift, scale, gain, w_t, w_lr_t, acts, tm, tn, first_col, rows_per_mod):
    m, d = x2d.shape
    n = len(acts) * tn
    assert m % tm == 0 and first_col % tn == 0 and rows_per_mod % tm == 0
    tiles_per_mod = rows_per_mod // tm
    blk0 = first_col // tn
    return pl.pallas_call(
        functools.partial(_inproj_kernel, acts=acts),
        grid=(m // tm, len(acts)),
        in_specs=[pl.BlockSpec((tm, d), lambda i, j: (i, 0)),
                  pl.BlockSpec((1, 1, d), lambda i, j: (i // tiles_per_mod, 0, 0)),
                  pl.BlockSpec((1, 1, d), lambda i, j: (i // tiles_per_mod, 0, 0)),
                  pl.BlockSpec((1, d), lambda i, j: (0, 0)),
                  pl.BlockSpec((tn, d), lambda i, j: (blk0 + j, 0)),
                  pl.BlockSpec((LANES, d), lambda i, j: (0, 0))],
        out_specs=[pl.BlockSpec((tm, tn), lambda i, j: (i, j)),
                   pl.BlockSpec((tm, LANES), lambda i, j: (i, 0))],
        out_shape=[jax.ShapeDtypeStruct((m, n), BF16),
                   jax.ShapeDtypeStruct((m, LANES), F32)],
        scratch_shapes=[pltpu.VMEM((tm, d), BF16)],
        compiler_params=_params(("parallel", "arbitrary")),
        name="inproj",
    )(x2d, shift, scale, gain, w_t, w_lr_t)


def _spatial_gating(u, v, lg_ref, lb_ref, ws_ref, bs_ref, ya_ref):
    for r0 in range(0, u.shape[0], SGU_CHUNK):
        for g in range(SGU_GROUPS):
            c0 = g * SGU_GW
            vg = v[r0:r0 + SGU_CHUNK, c0:c0 + SGU_GW]
            vc = vg - jnp.mean(vg, axis=-1, keepdims=True)
            var = jnp.mean(vc * vc, axis=-1, keepdims=True)
            cs = pl.ds(c0, SGU_GW)
            vn = vc * lax.rsqrt(var + EPS) * lg_ref[:, cs] + lb_ref[:, cs]
            s = _dot(ws_ref[g], vn.astype(BF16)) + bs_ref[g]
            ya_ref[pl.ds(r0, SGU_CHUNK), cs] = (u[r0:r0 + SGU_CHUNK, c0:c0 + SGU_GW] * s).astype(ya_ref.dtype)


def _inproj_rows_kernel(x_ref, sh_ref, sc_ref, g_ref, w_ref, wlr_ref, lg_ref, lb_ref, ws_ref, bs_ref, *rest):
    n_cast = (len(rest) - 3) // 2
    p_ref, lr_ref, ya_ref = rest[n_cast:n_cast + 3]
    _cast_slabs(rest[:n_cast], rest[n_cast + 3:])
    x = x_ref[...]
    y = x * lax.rsqrt(jnp.mean(x * x, axis=-1, keepdims=True) + EPS) * g_ref[...]
    a = (y * (1.0 + sc_ref[0]) + sh_ref[0]).astype(BF16)
    lr_ref[...] = _nt_dot(a, wlr_ref[...])
    u = _gelu(_nt_dot(a, w_ref[pl.ds(COL_U, MIX_A), :]))
    v = _gelu(_nt_dot(a, w_ref[pl.ds(COL_VS, MIX_A), :]))
    _spatial_gating(u, v, lg_ref, lb_ref, ws_ref, bs_ref, ya_ref)
    qk = _nt_dot(a, w_ref[pl.ds(COL_Q, 2 * GLA_KDIM), :]).astype(p_ref.dtype)
    vv = _nt_dot(a, w_ref[pl.ds(COL_V, MIX_B), :]).astype(p_ref.dtype)
    rr = _silu(_nt_dot(a, w_ref[pl.ds(COL_R, MIX_B), :])).astype(p_ref.dtype)
    for h in range(GLA_HEADS):
        base = h * P_HEAD
        p_ref[:, pl.ds(base + P_Q, GLA_DK)] = qk[:, h * GLA_DK:(h + 1) * GLA_DK]
        p_ref[:, pl.ds(base + P_K, GLA_DK)] = qk[:, GLA_KDIM + h * GLA_DK:GLA_KDIM + (h + 1) * GLA_DK]
        p_ref[:, pl.ds(base + P_V, GLA_DV)] = vv[:, h * GLA_DV:(h + 1) * GLA_DV]
        p_ref[:, pl.ds(base + P_R, GLA_DV)] = rr[:, h * GLA_DV:(h + 1) * GLA_DV]


def _inproj_rows(x2d, shift, scale, gain, w_t, w_lr_t, ln_g, ln_b, w_s, b_s, cast_weights, tm, rows_per_mod):
    m, d = x2d.shape
    assert m % tm == 0 and rows_per_mod % tm == 0 and tm % SGU_CHUNK == 0
    tiles_per_mod = rows_per_mod // tm
    cast_in, cast_out, cast_shapes = _cast_specs(cast_weights, m // tm)
    whole = lambda shape: pl.BlockSpec(shape, lambda i: (0,) * len(shape))
    return pl.pallas_call(
        _inproj_rows_kernel,
        grid=(m // tm,),
        in_specs=[pl.BlockSpec((tm, d), lambda i: (i, 0)),
                  pl.BlockSpec((1, 1, d), lambda i: (i // tiles_per_mod, 0, 0)),
                  pl.BlockSpec((1, 1, d), lambda i: (i // tiles_per_mod, 0, 0)),
                  whole((1, d)),
                  pl.BlockSpec((N_MAIN, d), lambda i: (0, 0), pipeline_mode=pl.Buffered(1)),
                  pl.BlockSpec((LANES, d), lambda i: (0, 0), pipeline_mode=pl.Buffered(1)),
                  whole((1, MIX_A)), whole((1, MIX_A)),
                  whole((SGU_GROUPS, SGU_CHUNK, SGU_CHUNK)), whole((SGU_GROUPS, SGU_CHUNK, 1))] + cast_in,
        out_specs=[pl.BlockSpec((tm, P_COLS), lambda i: (i, 0)),
                   pl.BlockSpec((tm, LANES), lambda i: (i, 0)),
                   pl.BlockSpec((tm, MIX_A), lambda i: (i, 0))] + cast_out,
        out_shape=[jax.ShapeDtypeStruct((m, P_COLS), BF16),
                   jax.ShapeDtypeStruct((m, LANES), F32),
                   jax.ShapeDtypeStruct((m, MIX_A), BF16)] + cast_shapes,
        compiler_params=_params(("arbitrary",)),
        name="inproj_rows",
    )(x2d, shift, scale, gain, w_t, w_lr_t, ln_g, ln_b, w_s, b_s, *cast_weights)


GLA_BLOCK = 4 * GLA_CHUNK
GLA_PAIR = 2 * GLA_CHUNK
GLA_GATE_BLOCKS = 2


LOG2E = 1.4426950408889634


def _log2_decay(z):
    soft = jnp.log2(1.0 + jnp.exp2(jnp.abs(z) * (-LOG2E)))
    return jnp.minimum(z, 0.0) * (LOG2E / GLA_NORMALIZER) - soft * (1.0 / GLA_NORMALIZER)


def _split(x):
    hi = x.astype(BF16)
    return hi, (x - hi.astype(F32)).astype(BF16)


def _gla_masks():
    n, c = GLA_BLOCK, GLA_CHUNK
    r = jnp.arange(n)[:, None]
    s = jnp.arange(n)[None, :]
    same = (r // c) == (s // c)
    return ((same & (s <= r)).astype(BF16), (same & (s >= r)).astype(BF16), jnp.eye(GLA_DV, dtype=BF16))


SUBLANES = 8


def _chunk_rows(blk, cidx, rows):
    return pl.ds(pl.multiple_of((blk * (GLA_BLOCK // GLA_CHUNK) + cidx) * rows, rows), rows)


def _gla_gates(rs, blk, q_ref, k_ref, v_ref, lr_ref, w_ref, b_ref, lower_ref, eye_ref, vt_scr, fwd, bwd):
    c, dk = GLA_CHUNK, GLA_DK
    vt_scr[:, rs] = _nt_dot(eye_ref[...], v_ref[rs, :]).astype(BF16)
    g = _log2_decay(_dot(lr_ref[rs, :].astype(BF16), w_ref[...]) + b_ref[...])
    g_hi, g_lo = _split(g)
    nb = g.shape[0]
    p = jnp.concatenate([_dot(lower_ref[...], g_hi[r0:r0 + GLA_BLOCK]) + _dot(lower_ref[...], g_lo[r0:r0 + GLA_BLOCK])
                         for r0 in range(0, nb, GLA_BLOCK)], axis=0)
    last = [p[i * c + c - 1:(i + 1) * c, :] for i in range(nb // c)]
    tot = jnp.concatenate([jnp.broadcast_to(t, (c, 2 * dk)) for t in last], axis=0)
    k32 = k_ref[rs, :].astype(F32)
    q32 = None if q_ref is None else q_ref[rs, :].astype(F32) * (GLA_DK ** -0.5)
    plans = ((fwd, slice(0, dk), p, tot - p), (bwd, slice(dk, 2 * dk), tot - p + g, p - g))
    for d, lanes, b_cum, to_end in plans:
        d["kd"][rs, :] = (k32 * jnp.exp2(to_end[:, lanes])).astype(BF16)
        for cidx in range(nb // c):
            dec = jnp.exp2(last[cidx][:, lanes])
            d["dec"][_chunk_rows(blk, cidx, SUBLANES), :] = jnp.broadcast_to(dec, (SUBLANES, dk))
        if q32 is not None:
            d["qe"][rs, :] = (q32 * jnp.exp2(b_cum[:, lanes])).astype(BF16)
            d["ke"][rs, :] = (k32 * jnp.exp2(-b_cum[:, lanes])).astype(BF16)


def _gla_scan_block(rs, blk, order, d, vt_scr, s_scr, record):
    c = GLA_CHUNK
    v_t = vt_scr[:, rs]
    kd = d["kd"][rs, :]
    zeros = jnp.zeros((c, kd.shape[1]), kd.dtype)
    s = s_scr[...]
    for cidx in order:
        pair, half = divmod(cidx, 2)
        slab = v_t[:, pair * GLA_PAIR:(pair + 1) * GLA_PAIR]
        kc = kd[cidx * c:(cidx + 1) * c]
        k_only = jnp.concatenate([kc, zeros] if half == 0 else [zeros, kc], axis=0)
        kv_t = _dot(slab, k_only)
        if record:
            d["snap"][_chunk_rows(blk, cidx, GLA_DV), :] = s.astype(BF16)
        dec = d["dec"][_chunk_rows(blk, cidx, SUBLANES), :]
        s = s * dec[0:1, :] + kv_t
    s_scr[...] = s


def _gla_output(rs, blk, v_ref, r_ref, dirs, ng_ref, y_ref):
    c = GLA_CHUNK
    v = v_ref[rs, :]
    o = None
    for d in dirs:
        q_e = d["qe"][rs, :]
        att = jnp.where(d["tri"][...] > 0, _nt_dot(q_e, d["ke"][rs, :]), 0.0).astype(BF16)
        inter = [_nt_dot(q_e[cidx * c:(cidx + 1) * c], d["snap"][_chunk_rows(blk, cidx, GLA_DV), :])
                 for cidx in range(GLA_BLOCK // c)]
        od = _dot(att, v) + jnp.concatenate(inter, axis=0)
        o = od if o is None else o + od
    o = o * lax.rsqrt(jnp.mean(o * o, axis=-1, keepdims=True) + EPS) * ng_ref[...]
    y_ref[rs, :] = (o * r_ref[rs, :].astype(F32)).astype(y_ref.dtype)


def _gla_kernel(p_ref, lr_ref, kc_ref, vc_ref, lrc_ref, w_ref, b_ref, ng_ref,
                lower_ref, upper_ref, eye_ref,
                y_ref, sf_scr, sb_scr, vt_scr, qef_scr, qeb_scr, kef_scr, keb_scr, kdf_scr, kdb_scr,
                decf_scr, decb_scr, snapf_scr, snapb_scr):
    fwd = dict(tri=lower_ref, qe=qef_scr, ke=kef_scr, kd=kdf_scr, dec=decf_scr, snap=snapf_scr)
    bwd = dict(tri=upper_ref, qe=qeb_scr, ke=keb_scr, kd=kdb_scr, dec=decb_scr, snap=snapb_scr)
    dirs = (fwd, bwd)
    q_ref, k_ref = p_ref.at[:, pl.ds(P_Q, GLA_DK)], p_ref.at[:, pl.ds(P_K, GLA_DK)]
    v_ref, r_ref = p_ref.at[:, pl.ds(P_V, GLA_DV)], p_ref.at[:, pl.ds(P_R, GLA_DV)]
    l, lc = p_ref.shape[0], kc_ref.shape[0]
    n_blk, n_cblk = l // GLA_BLOCK, lc // GLA_BLOCK
    asc = tuple(range(GLA_BLOCK // GLA_CHUNK))
    desc = asc[::-1]

    def block_rows(j):
        return pl.ds(pl.multiple_of(j * GLA_BLOCK, GLA_BLOCK), GLA_BLOCK)

    sf_scr[...] = jnp.zeros_like(sf_scr)
    sb_scr[...] = jnp.zeros_like(sb_scr)

    for j in range(n_cblk):
        _gla_gates(block_rows(j), j, None, kc_ref, vc_ref, lrc_ref, w_ref, b_ref, lower_ref, eye_ref,
                   vt_scr, fwd, bwd)
    for j in range(n_cblk):
        _gla_scan_block(block_rows(j), j, asc, fwd, vt_scr, sf_scr, False)
        jb = n_cblk - 1 - j
        _gla_scan_block(block_rows(jb), jb, desc, bwd, vt_scr, sb_scr, False)

    def gates(j, carry):
        rows = GLA_GATE_BLOCKS * GLA_BLOCK
        rs = pl.ds(pl.multiple_of(j * rows, rows), rows)
        _gla_gates(rs, j * GLA_GATE_BLOCKS, q_ref, k_ref, v_ref, lr_ref, w_ref, b_ref, lower_ref, eye_ref,
                   vt_scr, fwd, bwd)
        return carry

    def scan(j, carry):
        _gla_scan_block(block_rows(j), j, asc, fwd, vt_scr, sf_scr, True)
        jb = n_blk - 1 - j
        _gla_scan_block(block_rows(jb), jb, desc, bwd, vt_scr, sb_scr, True)
        return carry

    def output(j, carry):
        _gla_output(block_rows(j), j, v_ref, r_ref, dirs, ng_ref, y_ref)
        return carry

    lax.fori_loop(0, n_blk // GLA_GATE_BLOCKS, gates, 0)
    lax.fori_loop(0, n_blk, scan, 0, unroll=4)
    lax.fori_loop(0, n_blk, output, 0, unroll=4)


def _gla(p_lat, lr_lat, p_ctx, lr_ctx, gate_w, gate_b, norm_g):
    bsz, l, _ = p_lat.shape
    lc = p_ctx.shape[1]
    assert l % GLA_BLOCK == 0 and lc % GLA_BLOCK == 0 and lc <= l
    gate_specs = [pl.BlockSpec((LANES, 2 * GLA_DK), lambda b, h: (0, h)),
                  pl.BlockSpec((1, 2 * GLA_DK), lambda b, h: (0, h))]
    masks = _gla_masks()
    mask_specs = [pl.BlockSpec(m.shape, lambda b, h: (0, 0)) for m in masks]
    n_chunks = l // GLA_CHUNK
    return pl.pallas_call(
        _gla_kernel,
        grid=(bsz, GLA_HEADS),
        in_specs=[pl.BlockSpec((None, l, P_HEAD), lambda b, h: (b, 0, h)),
                  pl.BlockSpec((None, l, LANES), lambda b, h: (b, 0, 0)),
                  pl.BlockSpec((None, lc, GLA_DK), lambda b, h: (b, 0, h)),
                  pl.BlockSpec((None, lc, GLA_DV), lambda b, h: (b, 0, GLA_KDIM // GLA_DV + h)),
                  pl.BlockSpec((None, lc, LANES), lambda b, h: (b, 0, 0))]
                 + gate_specs
                 + [pl.BlockSpec((1, GLA_DV), lambda b, h: (0, 0))] + mask_specs,
        out_specs=pl.BlockSpec((None, l, GLA_DV), lambda b, h: (b, 0, h)),
        out_shape=jax.ShapeDtypeStruct((bsz, l, MIX_B), BF16),
        scratch_shapes=[pltpu.VMEM((GLA_DV, GLA_DK), F32)] * 2
                       + [pltpu.VMEM((GLA_DV, l), BF16)]
                       + [pltpu.VMEM((l, GLA_DK), BF16)] * 6
                       + [pltpu.VMEM((n_chunks * SUBLANES, GLA_DK), F32)] * 2
                       + [pltpu.VMEM((n_chunks * GLA_DV, GLA_DK), BF16)] * 2,
        compiler_params=_params(("parallel", "arbitrary")),
        name="gla",
    )(p_lat, lr_lat, p_ctx, p_ctx, lr_ctx, gate_w, gate_b, norm_g, *masks)


def _cast_specs(weights, n_steps, step_index=lambda i: i):
    ins, outs, shapes = [], [], []
    for w in weights:
        rows, cols = w.shape
        assert rows % (n_steps * 16) == 0, "slabs must be whole packed bf16 row tiles"
        spec = pl.BlockSpec((rows // n_steps, cols), lambda *idx: (step_index(*idx), 0))
        ins.append(spec)
        outs.append(spec)
        shapes.append(jax.ShapeDtypeStruct(w.shape, BF16))
    return ins, outs, shapes


def _cast_slabs(src_refs, dst_refs):
    for src, dst in zip(src_refs, dst_refs):
        dst[...] = src[...].astype(dst.dtype)


def _rms(x, g):
    return x * lax.rsqrt(jnp.mean(x * x, axis=-1, keepdims=True) + EPS) * g


def _outproj_kernel(ya_ref, yb_ref, wa_ref, wb_ref, x_ref, gt_ref, pg_ref, fg_ref, sc_ref, sh_ref, *rest):
    n_cast = (len(rest) - 2) // 2
    h_ref, f_ref = rest[n_cast:n_cast + 2]
    _cast_slabs(rest[:n_cast], rest[n_cast + 2:])
    for r0 in range(0, h_ref.shape[0], ROW_SUBTILE):
        rs = pl.ds(r0, ROW_SUBTILE)
        y = _dot(ya_ref[rs, :], wa_ref[...]) + _dot(yb_ref[rs, :], wb_ref[...])
        h = x_ref[rs, :] + gt_ref[0] * _rms(y, pg_ref[...])
        h_ref[rs, :] = h
        f_ref[rs, :] = (_rms(h, fg_ref[...]) * (1.0 + sc_ref[0]) + sh_ref[0]).astype(f_ref.dtype)


def _outproj(y_a, y_b, w_out, x2d, gate, post_g, ffn_g, scale_f, shift_f, cast_weights, rows_per_mod, tm=512):
    m, d = x2d.shape
    tiles_per_mod = rows_per_mod // tm
    mod_spec = pl.BlockSpec((1, 1, d), lambda i: (i // tiles_per_mod, 0, 0))
    vec_spec = pl.BlockSpec((1, d), lambda i: (0, 0))
    cast_in, cast_out, cast_shapes = _cast_specs(cast_weights, m // tm)
    return pl.pallas_call(
        _outproj_kernel,
        grid=(m // tm,),
        in_specs=[pl.BlockSpec((tm, MIX_A), lambda i: (i, 0)),
                  pl.BlockSpec((tm, MIX_B), lambda i: (i, 0)),
                  pl.BlockSpec((MIX_A, d), lambda i: (0, 0), pipeline_mode=pl.Buffered(1)),
                  pl.BlockSpec((MIX_B, d), lambda i: (MIX_A // MIX_B, 0), pipeline_mode=pl.Buffered(1)),
                  pl.BlockSpec((tm, d), lambda i: (i, 0)),
                  mod_spec, vec_spec, vec_spec, mod_spec, mod_spec] + cast_in,
        out_specs=[pl.BlockSpec((tm, d), lambda i: (i, 0)),
                   pl.BlockSpec((tm, d), lambda i: (i, 0))] + cast_out,
        out_shape=[jax.ShapeDtypeStruct((m, d), F32),
                   jax.ShapeDtypeStruct((m, d), BF16)] + cast_shapes,
        compiler_params=_params(("arbitrary",)),
        name="outproj",
    )(y_a, y_b, w_out, w_out, x2d, gate, post_g, ffn_g, scale_f, shift_f, *cast_weights)


def _conv_gelu(a, cw, cb):
    hw = GRID_W
    ext = a.shape[0]
    tm = ext - 2 * hw
    col = lax.broadcasted_iota(jnp.int32, (ext, 1), 0) % hw
    left = jnp.where(col > 0, pltpu.roll(a, 1, 0), 0.0)
    right = jnp.where(col < hw - 1, pltpu.roll(a, ext - 1, 0), 0.0)
    acc = jnp.broadcast_to(cb, (tm, a.shape[1]))
    for dr in range(3):
        lo = dr * hw
        acc = acc + cw[3 * dr + 0:3 * dr + 1, :] * left[lo:lo + tm, :]
        acc = acc + cw[3 * dr + 1:3 * dr + 2, :] * a[lo:lo + tm, :]
        acc = acc + cw[3 * dr + 2:3 * dr + 3, :] * right[lo:lo + tm, :]
    return _gelu(acc)


def _ffn_up_kernel(f_ref, fp_ref, fn_ref, wa_ref, wv_ref, cw_ref, cb_ref, *rest, tiles_per_image):
    n_cast = (len(rest) - 2) // 2
    g_ref, fext_scr = rest[n_cast], rest[-1]
    _cast_slabs(rest[:n_cast], rest[n_cast + 1:-1])
    i = pl.program_id(0)
    j = pl.program_id(1)
    tm = f_ref.shape[0]
    hw = GRID_W

    @pl.when(j == 0)
    def _():
        top = (i % tiles_per_image) == 0
        bottom = (i % tiles_per_image) == tiles_per_image - 1
        fext_scr[pl.ds(0, hw), :] = jnp.where(top, jnp.zeros_like(fp_ref), fp_ref[...])
        fext_scr[pl.ds(hw, tm), :] = f_ref[...]
        fext_scr[pl.ds(hw + tm, hw), :] = jnp.where(bottom, jnp.zeros_like(fn_ref), fn_ref[...])

    a = _dot(fext_scr[...], wa_ref[...])
    val = _dot(f_ref[...], wv_ref[...])
    g_ref[...] = (_conv_gelu(a, cw_ref[...], cb_ref[...]) * val).astype(g_ref.dtype)


def _ffn_up(f2d, w_up, conv_w, conv_b, cast_weights, rows_per_image, tm=1024, tf=512):
    m, d = f2d.shape
    d_ff = w_up.shape[1] // 2
    assert rows_per_image % tm == 0 and tm % GRID_W == 0 and d_ff % tf == 0
    hb = tm // GRID_W
    n_halo = m // GRID_W
    n_j = d_ff // tf
    cast_in, cast_out, cast_shapes = _cast_specs(cast_weights, (m // tm) * n_j, lambda i, j: i * n_j + j)
    return pl.pallas_call(
        functools.partial(_ffn_up_kernel, tiles_per_image=rows_per_image // tm),
        grid=(m // tm, n_j),
        in_specs=[pl.BlockSpec((tm, d), lambda i, j: (i, 0)),
                  pl.BlockSpec((GRID_W, d), lambda i, j: (jnp.maximum(i * hb - 1, 0), 0)),
                  pl.BlockSpec((GRID_W, d), lambda i, j: (jnp.minimum((i + 1) * hb, n_halo - 1), 0)),
                  pl.BlockSpec((d, tf), lambda i, j: (0, j)),
                  pl.BlockSpec((d, tf), lambda i, j: (0, n_j + j)),
                  pl.BlockSpec((9, tf), lambda i, j: (0, j)),
                  pl.BlockSpec((1, tf), lambda i, j: (0, j))] + cast_in,
        out_specs=[pl.BlockSpec((tm, tf), lambda i, j: (i, j))] + cast_out,
        out_shape=[jax.ShapeDtypeStruct((m, d_ff), BF16)] + cast_shapes,
        scratch_shapes=[pltpu.VMEM((tm + 2 * GRID_W, d), BF16)],
        compiler_params=_params(("arbitrary", "arbitrary")),
        name="ffn_up",
    )(f2d, f2d, f2d, w_up, w_up, conv_w, conv_b, *cast_weights)


def _ffn_down_kernel(g_ref, w_ref, h_ref, gt_ref, pg_ref, o_ref):
    for r0 in range(0, o_ref.shape[0], ROW_SUBTILE):
        rs = pl.ds(r0, ROW_SUBTILE)
        y = _dot(g_ref[rs, :], w_ref[...])
        o_ref[rs, :] = h_ref[rs, :] + gt_ref[0] * _rms(y, pg_ref[...])


def _ffn_down(g2d, w_down, h2d, gate, post_g, rows_per_mod, tm=512):
    m, d = h2d.shape
    d_ff = g2d.shape[1]
    assert m % tm == 0
    tiles_per_mod = rows_per_mod // tm
    return pl.pallas_call(
        _ffn_down_kernel,
        grid=(m // tm,),
        in_specs=[pl.BlockSpec((tm, d_ff), lambda i: (i, 0)),
                  pl.BlockSpec((d_ff, d), lambda i: (0, 0), pipeline_mode=pl.Buffered(1)),
                  pl.BlockSpec((tm, d), lambda i: (i, 0)),
                  pl.BlockSpec((1, 1, d), lambda i: (i // tiles_per_mod, 0, 0)),
                  pl.BlockSpec((1, d), lambda i: (0, 0))],
        out_specs=pl.BlockSpec((tm, d), lambda i: (i, 0)),
        out_shape=jax.ShapeDtypeStruct((m, d), F32),
        compiler_params=pltpu.CompilerParams(dimension_semantics=("arbitrary",),
                                             vmem_limit_bytes=FFN_DOWN_VMEM_LIMIT),
        name="ffn_down",
    )(g2d, w_down, h2d, gate, post_g)


def _gate_operands(w_f, b_f, w_b, b_b):
    r, kdim = w_f.shape
    wp = jnp.zeros((LANES, GLA_HEADS, 2, GLA_DK), BF16)
    wp = wp.at[:r, :, 0].set(w_f.astype(BF16).reshape(r, GLA_HEADS, GLA_DK))
    wp = wp.at[r:2 * r, :, 1].set(w_b.astype(BF16).reshape(r, GLA_HEADS, GLA_DK))
    bias = jnp.stack([b_f.reshape(GLA_HEADS, GLA_DK), b_b.reshape(GLA_HEADS, GLA_DK)], axis=1)
    return wp.reshape(LANES, 2 * kdim), bias.reshape(1, 2 * kdim)


def kernel(x, c, ctx, c_ctx, ada_w, ada_b, pre_mix_g, post_mix_g, pre_ffn_g, post_ffn_g, w_in, sgu_ln_g, sgu_ln_b, sgu_w, sgu_b, gla_gate_w_f, gla_gate_b_f, gla_gate_w_b, gla_gate_b_b, gla_norm_g, w_out, ffn_w_up, ffn_conv_w, ffn_conv_b, ffn_w_down):
    bsz, l, d = x.shape
    lc = ctx.shape[1]
    assert ada_w.shape[0] == 1, "single-layer kernel"
    d_ff = ffn_w_down.shape[1]

    c_rows = jnp.zeros((8, d), F32).at[:bsz].set(c).at[bsz].set(c_ctx)
    mod = _modulation(c_rows, ada_w[0], ada_b)
    sh_m, sc_m, gt_m, sh_f, sc_f, gt_f = [mod[:bsz, t * d:(t + 1) * d].reshape(bsz, 1, d) for t in range(N_MOD)]
    csh_m = mod[bsz:bsz + 1, 0:d].reshape(1, 1, d)
    csc_m = mod[bsz:bsz + 1, d:2 * d].reshape(1, 1, d)

    w_in_t = jnp.swapaxes(w_in[0], 0, 1).astype(BF16)
    w_lr_t = jnp.pad(w_in_t[COL_LR:], ((0, LANES - 2 * GLA_LOWRANK), (0, 0)))

    p_ctx, lr_ctx = _inproj(ctx.reshape(bsz * lc, d), csh_m, csc_m, pre_mix_g, w_in_t, w_lr_t,
                            ("none",) * 3, tm=2 * lc, tn=512, first_col=COL_K, rows_per_mod=bsz * lc)
    x2d = x.reshape(bsz * l, d)
    p_lat, lr_lat, y_a, w_out_bf = _inproj_rows(
        x2d, sh_m, sc_m, pre_mix_g, w_in_t, w_lr_t, sgu_ln_g, sgu_ln_b, sgu_w[0].astype(BF16),
        sgu_b[0][:, :, None], cast_weights=(w_out[0],), tm=512, rows_per_mod=l)

    y_b = _gla(p_lat.reshape(bsz, l, P_COLS), lr_lat.reshape(bsz, l, LANES),
               p_ctx.reshape(bsz, lc, -1), lr_ctx.reshape(bsz, lc, LANES),
               *_gate_operands(gla_gate_w_f[0], gla_gate_b_f[0], gla_gate_w_b[0], gla_gate_b_b[0]), gla_norm_g)
    h, f, w_up_bf = _outproj(y_a, y_b.reshape(bsz * l, MIX_B), w_out_bf, x2d, gt_m,
                             post_mix_g, pre_ffn_g, sc_f, sh_f, cast_weights=(ffn_w_up[0],), rows_per_mod=l)
    g, w_down_bf = _ffn_up(f, w_up_bf, ffn_conv_w[0].reshape(9, d_ff), ffn_conv_b,
                           cast_weights=(ffn_w_down[0],), rows_per_image=l)
    out = _ffn_down(g, w_down_bf, h, gt_f, post_ffn_g, rows_per_mod=l)
    return out.reshape(bsz, l, d)
```

```python
import functools

import jax
import jax.numpy as jnp
from jax import lax
from jax.experimental import pallas as pl
from jax.experimental.pallas import tpu as pltpu

F32 = jnp.float32
BF16 = jnp.bfloat16

EPS = 1e-6
GRID_W = 64
MIX_A = 1024
MIX_B = 1024
SGU_GROUPS = 4
SGU_CHUNK = 128
SGU_GW = MIX_A // SGU_GROUPS
GLA_HEADS = 4
GLA_DK = 128
GLA_DV = 256
GLA_KDIM = GLA_HEADS * GLA_DK
GLA_LOWRANK = 16
GLA_NORMALIZER = 16.0
GLA_CHUNK = 64
N_MOD = 6
LANES = 128
ROW_SUBTILE = 256
VMEM_LIMIT = 56 * 1024 * 1024
FFN_DOWN_VMEM_LIMIT = 60 * 1024 * 1024

COL_U, COL_VS, COL_Q, COL_K, COL_V, COL_R, COL_LR = 0, 1024, 2048, 2560, 3072, 4096, 5120
N_MAIN = COL_LR
P_Q, P_K, P_V, P_R, P_HEAD = 0, 128, 256, 512, 768
P_COLS = GLA_HEADS * P_HEAD


def _dot(a, b):
    return jnp.dot(a, b, preferred_element_type=F32)


def _nt_dot(a, b):
    return lax.dot_general(a, b, (((1,), (1,)), ((), ())), preferred_element_type=F32)


def _silu(x):
    return x / (1.0 + jnp.exp(-x))


_GELU_A = -2.0 * 0.7978845608028654 * 1.4426950408889634
_GELU_B = _GELU_A * 0.044715


def _gelu(x):
    return x / (1.0 + jnp.exp2((_GELU_A + _GELU_B * (x * x)) * x))


def _params(sem):
    return pltpu.CompilerParams(dimension_semantics=sem, vmem_limit_bytes=VMEM_LIMIT)


def _mod_kernel(c_ref, w_ref, b_ref, o_ref):
    s = _silu(c_ref[...]).astype(BF16)
    o_ref[...] = _dot(s, w_ref[...].astype(BF16)) + b_ref[...]


def _modulation(c_rows, ada_w, ada_b, tn=1024):
    rows, d = c_rows.shape
    n = ada_w.shape[1]
    return pl.pallas_call(
        _mod_kernel,
        grid=(n // tn,),
        in_specs=[pl.BlockSpec((rows, d), lambda j: (0, 0)),
                  pl.BlockSpec((d, tn), lambda j: (0, j)),
                  pl.BlockSpec((1, tn), lambda j: (0, j))],
        out_specs=pl.BlockSpec((rows, tn), lambda j: (0, j)),
        out_shape=jax.ShapeDtypeStruct((rows, n), F32),
        compiler_params=_params(("arbitrary",)),
        name="mod",
    )(c_rows, ada_w, ada_b)


def _norm_modulate(x, g_ref, sc_ref, sh_ref):
    gain = g_ref[...] * (1.0 + sc_ref[0])
    return (x * lax.rsqrt(jnp.mean(x * x, axis=-1, keepdims=True) + EPS) * gain + sh_ref[0]).astype(BF16)


def _inproj_kernel(x_ref, sh_ref, sc_ref, g_ref, w_ref, wlr_ref, p_ref, lr_ref, a_scr, *, acts):
    j = pl.program_id(1)

    @pl.when(j == 0)
    def _():
        a = _norm_modulate(x_ref[...], g_ref, sc_ref, sh_ref)
        a_scr[...] = a
        lr_ref[...] = _nt_dot(a, wlr_ref[...])

    acc = _nt_dot(a_scr[...], w_ref[...])
    fns = {"gelu": _gelu, "silu": _silu, "none": lambda t: t}
    for name in sorted(set(acts)):
        idx = [t for t, a in enumerate(acts) if a == name]
        cond = functools.reduce(jnp.logical_or, [j == t for t in idx])

        @pl.when(cond)
        def _(name=name):
            p_ref[...] = fns[name](acc).astype(p_ref.dtype)


def _inproj(x2d, shift, scale, gain, w_t, w_lr_t, acts, tm, tn, first_col, rows_per_mod):
    m, d = x2d.shape
    n = len(acts) * tn
    assert m % tm == 0 and first_col % tn == 0 and rows_per_mod % tm == 0
    tiles_per_mod = rows_per_mod // tm
    blk0 = first_col // tn
    return pl.pallas_call(
        functools.partial(_inproj_kernel, acts=acts),
        grid=(m // tm, len(acts)),
        in_specs=[pl.BlockSpec((tm, d), lambda i, j: (i, 0)),
                  pl.BlockSpec((1, 1, d), lambda i, j: (i // tiles_per_mod, 0, 0)),
                  pl.BlockSpec((1, 1, d), lambda i, j: (i // tiles_per_mod, 0, 0)),
                  pl.BlockSpec((1, d), lambda i, j: (0, 0)),
                  pl.BlockSpec((tn, d), lambda i, j: (blk0 + j, 0)),
                  pl.BlockSpec((LANES, d), lambda i, j: (0, 0))],
        out_specs=[pl.BlockSpec((tm, tn), lambda i, j: (i, j)),
                   pl.BlockSpec((tm, LANES), lambda i, j: (i, 0))],
        out_shape=[jax.ShapeDtypeStruct((m, n), BF16),
                   jax.ShapeDtypeStruct((m, LANES), F32)],
        scratch_shapes=[pltpu.VMEM((tm, d), BF16)],
        compiler_params=_params(("parallel", "arbitrary")),
        name="inproj",
    )(x2d, shift, scale, gain, w_t, w_lr_t)


def _spatial_gating(u, v, lg_ref, lb_ref, ws_ref, bs_ref, ya_ref):
    for r0 in range(0, u.shape[0], SGU_CHUNK):
        for g in range(SGU_GROUPS):
            c0 = g * SGU_GW
            vg = v[r0:r0 + SGU_CHUNK, c0:c0 + SGU_GW]
            vc = vg - jnp.mean(vg, axis=-1, keepdims=True)
            var = jnp.mean(vc * vc, axis=-1, keepdims=True)
            cs = pl.ds(c0, SGU_GW)
            vn = vc * lax.rsqrt(var + EPS) * lg_ref[:, cs] + lb_ref[:, cs]
            s = _dot(ws_ref[g], vn.astype(BF16)) + bs_ref[g]
            ya_ref[pl.ds(r0, SGU_CHUNK), cs] = (u[r0:r0 + SGU_CHUNK, c0:c0 + SGU_GW] * s).astype(ya_ref.dtype)


def _inproj_rows_kernel(x_ref, sh_ref, sc_ref, g_ref, w_ref, wlr_ref, lg_ref, lb_ref, ws_ref, bs_ref, *rest):
    n_cast = (len(rest) - 3) // 2
    p_ref, lr_ref, ya_ref = rest[n_cast:n_cast + 3]
    _cast_slabs(rest[:n_cast], rest[n_cast + 3:])
    a = _norm_modulate(x_ref[...], g_ref, sc_ref, sh_ref)
    lr_ref[...] = _nt_dot(a, wlr_ref[...])
    u = _gelu(_nt_dot(a, w_ref[pl.ds(COL_U, MIX_A), :]))
    v = _gelu(_nt_dot(a, w_ref[pl.ds(COL_VS, MIX_A), :]))
    _spatial_gating(u, v, lg_ref, lb_ref, ws_ref, bs_ref, ya_ref)
    qk = _nt_dot(a, w_ref[pl.ds(COL_Q, 2 * GLA_KDIM), :]).astype(p_ref.dtype)
    vv = _nt_dot(a, w_ref[pl.ds(COL_V, MIX_B), :]).astype(p_ref.dtype)
    rr = _silu(_nt_dot(a, w_ref[pl.ds(COL_R, MIX_B), :])).astype(p_ref.dtype)
    for h in range(GLA_HEADS):
        base = h * P_HEAD
        p_ref[:, pl.ds(base + P_Q, GLA_DK)] = qk[:, h * GLA_DK:(h + 1) * GLA_DK]
        p_ref[:, pl.ds(base + P_K, GLA_DK)] = qk[:, GLA_KDIM + h * GLA_DK:GLA_KDIM + (h + 1) * GLA_DK]
        p_ref[:, pl.ds(base + P_V, GLA_DV)] = vv[:, h * GLA_DV:(h + 1) * GLA_DV]
        p_ref[:, pl.ds(base + P_R, GLA_DV)] = rr[:, h * GLA_DV:(h + 1) * GLA_DV]


def _inproj_rows(x2d, shift, scale, gain, w_t, w_lr_t, ln_g, ln_b, w_s, b_s, cast_weights, tm, rows_per_mod):
    m, d = x2d.shape
    assert m % tm == 0 and rows_per_mod % tm == 0 and tm % SGU_CHUNK == 0
    tiles_per_mod = rows_per_mod // tm
    cast_in, cast_out, cast_shapes = _cast_specs(cast_weights, m // tm)
    whole = lambda shape: pl.BlockSpec(shape, lambda i: (0,) * len(shape))
    return pl.pallas_call(
        _inproj_rows_kernel,
        grid=(m // tm,),
        in_specs=[pl.BlockSpec((tm, d), lambda i: (i, 0)),
                  pl.BlockSpec((1, 1, d), lambda i: (i // tiles_per_mod, 0, 0)),
                  pl.BlockSpec((1, 1, d), lambda i: (i // tiles_per_mod, 0, 0)),
                  whole((1, d)),
                  pl.BlockSpec((N_MAIN, d), lambda i: (0, 0), pipeline_mode=pl.Buffered(1)),
                  pl.BlockSpec((LANES, d), lambda i: (0, 0), pipeline_mode=pl.Buffered(1)),
                  whole((1, MIX_A)), whole((1, MIX_A)),
                  whole((SGU_GROUPS, SGU_CHUNK, SGU_CHUNK)), whole((SGU_GROUPS, SGU_CHUNK, 1))] + cast_in,
        out_specs=[pl.BlockSpec((tm, P_COLS), lambda i: (i, 0)),
                   pl.BlockSpec((tm, LANES), lambda i: (i, 0)),
                   pl.BlockSpec((tm, MIX_A), lambda i: (i, 0))] + cast_out,
        out_shape=[jax.ShapeDtypeStruct((m, P_COLS), BF16),
                   jax.ShapeDtypeStruct((m, LANES), F32),
                   jax.ShapeDtypeStruct((m, MIX_A), BF16)] + cast_shapes,
        compiler_params=_params(("arbitrary",)),
        name="inproj_rows",
    )(x2d, shift, scale, gain, w_t, w_lr_t, ln_g, ln_b, w_s, b_s, *cast_weights)


GLA_BLOCK = 4 * GLA_CHUNK
GLA_PAIR = 2 * GLA_CHUNK
GLA_GATE_BLOCKS = 2


LOG2E = 1.4426950408889634


def _log2_decay(z):
    soft = jnp.log2(1.0 + jnp.exp2(jnp.abs(z) * (-LOG2E)))
    return jnp.minimum(z, 0.0) * (LOG2E / GLA_NORMALIZER) - soft * (1.0 / GLA_NORMALIZER)


def _split(x):
    hi = x.astype(BF16)
    return hi, (x - hi.astype(F32)).astype(BF16)


def _gla_masks():
    n, c = GLA_BLOCK, GLA_CHUNK
    r = jnp.arange(n)[:, None]
    s = jnp.arange(n)[None, :]
    same = (r // c) == (s // c)
    return ((same & (s <= r)).astype(BF16), (same & (s >= r)).astype(BF16), jnp.eye(GLA_DV, dtype=BF16))


SUBLANES = 8


def _chunk_rows(blk, cidx, rows):
    return pl.ds(pl.multiple_of((blk * (GLA_BLOCK // GLA_CHUNK) + cidx) * rows, rows), rows)


def _gla_gates(rs, blk, q_ref, k_ref, v_ref, lr_ref, w_ref, b_ref, lower_ref, eye_ref, vt_scr, fwd, bwd):
    c, dk = GLA_CHUNK, GLA_DK
    vt_scr[:, rs] = _nt_dot(eye_ref[...], v_ref[rs, :]).astype(BF16)
    g = _log2_decay(_dot(lr_ref[rs, :].astype(BF16), w_ref[...]) + b_ref[...])
    g_hi, g_lo = _split(g)
    nb = g.shape[0]
    p = jnp.concatenate([_dot(lower_ref[...], g_hi[r0:r0 + GLA_BLOCK]) + _dot(lower_ref[...], g_lo[r0:r0 + GLA_BLOCK])
                         for r0 in range(0, nb, GLA_BLOCK)], axis=0)
    last = [p[i * c + c - 1:(i + 1) * c, :] for i in range(nb // c)]
    tot = jnp.concatenate([jnp.broadcast_to(t, (c, 2 * dk)) for t in last], axis=0)
    k32 = k_ref[rs, :].astype(F32)
    q32 = None if q_ref is None else q_ref[rs, :].astype(F32) * (GLA_DK ** -0.5)
    plans = ((fwd, slice(0, dk), p, tot - p), (bwd, slice(dk, 2 * dk), tot - p + g, p - g))
    for d, lanes, b_cum, to_end in plans:
        d["kd"][rs, :] = (k32 * jnp.exp2(to_end[:, lanes])).astype(BF16)
        for cidx in range(nb // c):
            dec = jnp.exp2(last[cidx][:, lanes])
            d["dec"][_chunk_rows(blk, cidx, SUBLANES), :] = jnp.broadcast_to(dec, (SUBLANES, dk))
        if q32 is not None:
            d["qe"][rs, :] = (q32 * jnp.exp2(b_cum[:, lanes])).astype(BF16)
            d["ke"][rs, :] = (k32 * jnp.exp2(-b_cum[:, lanes])).astype(BF16)


def _gla_scan_block(rs, blk, order, d, vt_scr, s_scr, record):
    c = GLA_CHUNK
    v_t = vt_scr[:, rs]
    kd = d["kd"][rs, :]
    zeros = jnp.zeros((c, kd.shape[1]), kd.dtype)
    s = s_scr[...]
    for cidx in order:
        pair, half = divmod(cidx, 2)
        slab = v_t[:, pair * GLA_PAIR:(pair + 1) * GLA_PAIR]
        kc = kd[cidx * c:(cidx + 1) * c]
        k_only = jnp.concatenate([kc, zeros] if half == 0 else [zeros, kc], axis=0)
        kv_t = _dot(slab, k_only)
        if record:
            d["snap"][_chunk_rows(blk, cidx, GLA_DV), :] = s.astype(BF16)
        dec = d["dec"][_chunk_rows(blk, cidx, SUBLANES), :]
        s = s * dec[0:1, :] + kv_t
    s_scr[...] = s


def _gla_output(rs, blk, v_ref, r_ref, qe_scr, snap_scr, fwd, bwd, ng_ref, y_ref):
    c, dk = GLA_CHUNK, GLA_DK
    qe = qe_scr[rs, :]
    s_f = _nt_dot(qe[:, :dk], fwd["ke"][rs, :])
    s_b = _nt_dot(qe[:, dk:], bwd["ke"][rs, :])
    att = jnp.where(fwd["tri"][...] > 0, s_f, 0.0) + jnp.where(bwd["tri"][...] > 0, s_b, 0.0)
    inter = [_nt_dot(qe[cidx * c:(cidx + 1) * c], snap_scr[_chunk_rows(blk, cidx, GLA_DV), :])
             for cidx in range(GLA_BLOCK // c)]
    o = _dot(att.astype(BF16), v_ref[rs, :]) + jnp.concatenate(inter, axis=0)
    o = o * lax.rsqrt(jnp.mean(o * o, axis=-1, keepdims=True) + EPS) * ng_ref[...]
    y_ref[rs, :] = (o * r_ref[rs, :].astype(F32)).astype(y_ref.dtype)


def _gla_kernel(p_ref, lr_ref, kc_ref, vc_ref, lrc_ref, w_ref, b_ref, ng_ref,
                lower_ref, upper_ref, eye_ref,
                y_ref, sf_scr, sb_scr, vt_scr, qe_scr, kef_scr, keb_scr, kdf_scr, kdb_scr,
                decf_scr, decb_scr, snap_scr):
    lanes_f, lanes_b = pl.ds(0, GLA_DK), pl.ds(GLA_DK, GLA_DK)
    fwd = dict(tri=lower_ref, qe=qe_scr.at[:, lanes_f], ke=kef_scr, kd=kdf_scr, dec=decf_scr,
               snap=snap_scr.at[:, lanes_f])
    bwd = dict(tri=upper_ref, qe=qe_scr.at[:, lanes_b], ke=keb_scr, kd=kdb_scr, dec=decb_scr,
               snap=snap_scr.at[:, lanes_b])
    q_ref, k_ref = p_ref.at[:, pl.ds(P_Q, GLA_DK)], p_ref.at[:, pl.ds(P_K, GLA_DK)]
    v_ref, r_ref = p_ref.at[:, pl.ds(P_V, GLA_DV)], p_ref.at[:, pl.ds(P_R, GLA_DV)]
    l, lc = p_ref.shape[0], kc_ref.shape[0]
    n_blk, n_cblk = l // GLA_BLOCK, lc // GLA_BLOCK
    asc = tuple(range(GLA_BLOCK // GLA_CHUNK))
    desc = asc[::-1]

    def block_rows(j):
        return pl.ds(pl.multiple_of(j * GLA_BLOCK, GLA_BLOCK), GLA_BLOCK)

    sf_scr[...] = jnp.zeros_like(sf_scr)
    sb_scr[...] = jnp.zeros_like(sb_scr)

    for j in range(n_cblk):
        _gla_gates(block_rows(j), j, None, kc_ref, vc_ref, lrc_ref, w_ref, b_ref, lower_ref, eye_ref,
                   vt_scr, fwd, bwd)
    for j in range(n_cblk):
        _gla_scan_block(block_rows(j), j, asc, fwd, vt_scr, sf_scr, False)
        jb = n_cblk - 1 - j
        _gla_scan_block(block_rows(jb), jb, desc, bwd, vt_scr, sb_scr, False)

    def gates(j, carry):
        rows = GLA_GATE_BLOCKS * GLA_BLOCK
        rs = pl.ds(pl.multiple_of(j * rows, rows), rows)
        _gla_gates(rs, j * GLA_GATE_BLOCKS, q_ref, k_ref, v_ref, lr_ref, w_ref, b_ref, lower_ref, eye_ref,
                   vt_scr, fwd, bwd)
        return carry

    def scan(j, carry):
        _gla_scan_block(block_rows(j), j, asc, fwd, vt_scr, sf_scr, True)
        jb = n_blk - 1 - j
        _gla_scan_block(block_rows(jb), jb, desc, bwd, vt_scr, sb_scr, True)
        return carry

    def output(j, carry):
        _gla_output(block_rows(j), j, v_ref, r_ref, qe_scr, snap_scr, fwd, bwd, ng_ref, y_ref)
        return carry

    lax.fori_loop(0, n_blk // GLA_GATE_BLOCKS, gates, 0)
    lax.fori_loop(0, n_blk, scan, 0, unroll=4)
    lax.fori_loop(0, n_blk, output, 0, unroll=4)


def _gla(p_lat, lr_lat, p_ctx, lr_ctx, gate_w, gate_b, norm_g):
    bsz, l, _ = p_lat.shape
    lc = p_ctx.shape[1]
    assert l % GLA_BLOCK == 0 and lc % GLA_BLOCK == 0 and lc <= l
    gate_specs = [pl.BlockSpec((LANES, 2 * GLA_DK), lambda b, h: (0, h)),
                  pl.BlockSpec((1, 2 * GLA_DK), lambda b, h: (0, h))]
    masks = _gla_masks()
    mask_specs = [pl.BlockSpec(m.shape, lambda b, h: (0, 0)) for m in masks]
    n_chunks = l // GLA_CHUNK
    return pl.pallas_call(
        _gla_kernel,
        grid=(bsz, GLA_HEADS),
        in_specs=[pl.BlockSpec((None, l, P_HEAD), lambda b, h: (b, 0, h)),
                  pl.BlockSpec((None, l, LANES), lambda b, h: (b, 0, 0)),
                  pl.BlockSpec((None, lc, GLA_DK), lambda b, h: (b, 0, h)),
                  pl.BlockSpec((None, lc, GLA_DV), lambda b, h: (b, 0, GLA_KDIM // GLA_DV + h)),
                  pl.BlockSpec((None, lc, LANES), lambda b, h: (b, 0, 0))]
                 + gate_specs
                 + [pl.BlockSpec((1, GLA_DV), lambda b, h: (0, 0))] + mask_specs,
        out_specs=pl.BlockSpec((None, l, GLA_DV), lambda b, h: (b, 0, h)),
        out_shape=jax.ShapeDtypeStruct((bsz, l, MIX_B), BF16),
        scratch_shapes=[pltpu.VMEM((GLA_DV, GLA_DK), F32)] * 2
                       + [pltpu.VMEM((GLA_DV, l), BF16)]
                       + [pltpu.VMEM((l, 2 * GLA_DK), BF16)]
                       + [pltpu.VMEM((l, GLA_DK), BF16)] * 4
                       + [pltpu.VMEM((n_chunks * SUBLANES, GLA_DK), F32)] * 2
                       + [pltpu.VMEM((n_chunks * GLA_DV, 2 * GLA_DK), BF16)],
        compiler_params=_params(("parallel", "arbitrary")),
        name="gla",
    )(p_lat, lr_lat, p_ctx, p_ctx, lr_ctx, gate_w, gate_b, norm_g, *masks)


def _cast_specs(weights, n_steps, step_index=lambda i: i):
    ins, outs, shapes = [], [], []
    for w in weights:
        rows, cols = w.shape
        assert rows % (n_steps * 16) == 0, "slabs must be whole packed bf16 row tiles"
        spec = pl.BlockSpec((rows // n_steps, cols), lambda *idx: (step_index(*idx), 0))
        ins.append(spec)
        outs.append(spec)
        shapes.append(jax.ShapeDtypeStruct(w.shape, BF16))
    return ins, outs, shapes


def _cast_slabs(src_refs, dst_refs):
    for src, dst in zip(src_refs, dst_refs):
        dst[...] = src[...].astype(dst.dtype)


def _rms(x, g):
    return x * lax.rsqrt(jnp.mean(x * x, axis=-1, keepdims=True) + EPS) * g


def _outproj_kernel(ya_ref, yb_ref, wa_ref, wb_ref, x_ref, gt_ref, pg_ref, fg_ref, sc_ref, sh_ref, *rest):
    n_cast = (len(rest) - 2) // 2
    h_ref, f_ref = rest[n_cast:n_cast + 2]
    _cast_slabs(rest[:n_cast], rest[n_cast + 2:])
    gate_gain = gt_ref[0] * pg_ref[...]
    mod_gain = fg_ref[...] * (1.0 + sc_ref[0])
    for r0 in range(0, h_ref.shape[0], ROW_SUBTILE):
        rs = pl.ds(r0, ROW_SUBTILE)
        y = _dot(ya_ref[rs, :], wa_ref[...]) + _dot(yb_ref[rs, :], wb_ref[...])
        h = x_ref[rs, :] + _rms(y, gate_gain)
        h_ref[rs, :] = h
        f_ref[rs, :] = (_rms(h, mod_gain) + sh_ref[0]).astype(f_ref.dtype)


def _outproj(y_a, y_b, w_out, x2d, gate, post_g, ffn_g, scale_f, shift_f, cast_weights, rows_per_mod, tm=512):
    m, d = x2d.shape
    tiles_per_mod = rows_per_mod // tm
    mod_spec = pl.BlockSpec((1, 1, d), lambda i: (i // tiles_per_mod, 0, 0))
    vec_spec = pl.BlockSpec((1, d), lambda i: (0, 0))
    cast_in, cast_out, cast_shapes = _cast_specs(cast_weights, m // tm)
    return pl.pallas_call(
        _outproj_kernel,
        grid=(m // tm,),
        in_specs=[pl.BlockSpec((tm, MIX_A), lambda i: (i, 0)),
                  pl.BlockSpec((tm, MIX_B), lambda i: (i, 0)),
                  pl.BlockSpec((MIX_A, d), lambda i: (0, 0), pipeline_mode=pl.Buffered(1)),
                  pl.BlockSpec((MIX_B, d), lambda i: (MIX_A // MIX_B, 0), pipeline_mode=pl.Buffered(1)),
                  pl.BlockSpec((tm, d), lambda i: (i, 0)),
                  mod_spec, vec_spec, vec_spec, mod_spec, mod_spec] + cast_in,
        out_specs=[pl.BlockSpec((tm, d), lambda i: (i, 0)),
                   pl.BlockSpec((tm, d), lambda i: (i, 0))] + cast_out,
        out_shape=[jax.ShapeDtypeStruct((m, d), F32),
                   jax.ShapeDtypeStruct((m, d), BF16)] + cast_shapes,
        compiler_params=_params(("arbitrary",)),
        name="outproj",
    )(y_a, y_b, w_out, w_out, x2d, gate, post_g, ffn_g, scale_f, shift_f, *cast_weights)


def _conv_gelu(a, cw, cb):
    hw = GRID_W
    ext = a.shape[0]
    tm = ext - 2 * hw
    rows = [a[dr * hw:dr * hw + tm, :] for dr in range(3)]
    taps = [sum(cw[3 * dr + dc:3 * dr + dc + 1, :] * rows[dr] for dr in range(3)) for dc in range(3)]
    col = lax.broadcasted_iota(jnp.int32, (tm, 1), 0) % hw
    left = jnp.where(col > 0, pltpu.roll(taps[0], 1, 0), 0.0)
    right = jnp.where(col < hw - 1, pltpu.roll(taps[2], tm - 1, 0), 0.0)
    return _gelu(cb + taps[1] + left + right)


def _ffn_up_kernel(f_ref, fp_ref, fn_ref, wa_ref, wv_ref, cw_ref, cb_ref, *rest, tiles_per_image):
    n_cast = (len(rest) - 2) // 2
    g_ref, fext_scr = rest[n_cast], rest[-1]
    _cast_slabs(rest[:n_cast], rest[n_cast + 1:-1])
    i = pl.program_id(0)
    j = pl.program_id(1)
    tm = f_ref.shape[0]
    hw = GRID_W

    @pl.when(j == 0)
    def _():
        top = (i % tiles_per_image) == 0
        bottom = (i % tiles_per_image) == tiles_per_image - 1
        fext_scr[pl.ds(0, hw), :] = jnp.where(top, jnp.zeros_like(fp_ref), fp_ref[...])
        fext_scr[pl.ds(hw, tm), :] = f_ref[...]
        fext_scr[pl.ds(hw + tm, hw), :] = jnp.where(bottom, jnp.zeros_like(fn_ref), fn_ref[...])

    a = _dot(fext_scr[...], wa_ref[...])
    val = _dot(f_ref[...], wv_ref[...])
    g_ref[...] = (_conv_gelu(a, cw_ref[...], cb_ref[...]) * val).astype(g_ref.dtype)


def _ffn_up(f2d, w_up, conv_w, conv_b, cast_weights, rows_per_image, tm=1024, tf=512):
    m, d = f2d.shape
    d_ff = w_up.shape[1] // 2
    assert rows_per_image % tm == 0 and tm % GRID_W == 0 and d_ff % tf == 0
    hb = tm // GRID_W
    n_halo = m // GRID_W
    n_j = d_ff // tf
    cast_in, cast_out, cast_shapes = _cast_specs(cast_weights, (m // tm) * n_j, lambda i, j: i * n_j + j)
    return pl.pallas_call(
        functools.partial(_ffn_up_kernel, tiles_per_image=rows_per_image // tm),
        grid=(m // tm, n_j),
        in_specs=[pl.BlockSpec((tm, d), lambda i, j: (i, 0)),
                  pl.BlockSpec((GRID_W, d), lambda i, j: (jnp.maximum(i * hb - 1, 0), 0)),
                  pl.BlockSpec((GRID_W, d), lambda i, j: (jnp.minimum((i + 1) * hb, n_halo - 1), 0)),
                  pl.BlockSpec((d, tf), lambda i, j: (0, j)),
                  pl.BlockSpec((d, tf), lambda i, j: (0, n_j + j)),
                  pl.BlockSpec((9, tf), lambda i, j: (0, j)),
                  pl.BlockSpec((1, tf), lambda i, j: (0, j))] + cast_in,
        out_specs=[pl.BlockSpec((tm, tf), lambda i, j: (i, j))] + cast_out,
        out_shape=[jax.ShapeDtypeStruct((m, d_ff), BF16)] + cast_shapes,
        scratch_shapes=[pltpu.VMEM((tm + 2 * GRID_W, d), BF16)],
        compiler_params=_params(("arbitrary", "arbitrary")),
        name="ffn_up",
    )(f2d, f2d, f2d, w_up, w_up, conv_w, conv_b, *cast_weights)


def _ffn_down_kernel(g_ref, w_ref, h_ref, gt_ref, pg_ref, o_ref):
    gate_gain = gt_ref[0] * pg_ref[...]
    for r0 in range(0, o_ref.shape[0], ROW_SUBTILE):
        rs = pl.ds(r0, ROW_SUBTILE)
        y = _dot(g_ref[rs, :], w_ref[...])
        o_ref[rs, :] = h_ref[rs, :] + _rms(y, gate_gain)


def _ffn_down(g2d, w_down, h2d, gate, post_g, rows_per_mod, tm=512):
    m, d = h2d.shape
    d_ff = g2d.shape[1]
    assert m % tm == 0
    tiles_per_mod = rows_per_mod // tm
    return pl.pallas_call(
        _ffn_down_kernel,
        grid=(m // tm,),
        in_specs=[pl.BlockSpec((tm, d_ff), lambda i: (i, 0)),
                  pl.BlockSpec((d_ff, d), lambda i: (0, 0), pipeline_mode=pl.Buffered(1)),
                  pl.BlockSpec((tm, d), lambda i: (i, 0)),
                  pl.BlockSpec((1, 1, d), lambda i: (i // tiles_per_mod, 0, 0)),
                  pl.BlockSpec((1, d), lambda i: (0, 0))],
        out_specs=pl.BlockSpec((tm, d), lambda i: (i, 0)),
        out_shape=jax.ShapeDtypeStruct((m, d), F32),
        compiler_params=pltpu.CompilerParams(dimension_semantics=("arbitrary",),
                                             vmem_limit_bytes=FFN_DOWN_VMEM_LIMIT),
        name="ffn_down",
    )(g2d, w_down, h2d, gate, post_g)


def _gate_operands(w_f, b_f, w_b, b_b):
    r, kdim = w_f.shape
    wp = jnp.zeros((LANES, GLA_HEADS, 2, GLA_DK), BF16)
    wp = wp.at[:r, :, 0].set(w_f.astype(BF16).reshape(r, GLA_HEADS, GLA_DK))
    wp = wp.at[r:2 * r, :, 1].set(w_b.astype(BF16).reshape(r, GLA_HEADS, GLA_DK))
    bias = jnp.stack([b_f.reshape(GLA_HEADS, GLA_DK), b_b.reshape(GLA_HEADS, GLA_DK)], axis=1)
    return wp.reshape(LANES, 2 * kdim), bias.reshape(1, 2 * kdim)


def kernel(x, c, ctx, c_ctx, ada_w, ada_b, pre_mix_g, post_mix_g, pre_ffn_g, post_ffn_g, w_in, sgu_ln_g, sgu_ln_b, sgu_w, sgu_b, gla_gate_w_f, gla_gate_b_f, gla_gate_w_b, gla_gate_b_b, gla_norm_g, w_out, ffn_w_up, ffn_conv_w, ffn_conv_b, ffn_w_down):
    bsz, l, d = x.shape
    lc = ctx.shape[1]
    assert ada_w.shape[0] == 1, "single-layer kernel"
    d_ff = ffn_w_down.shape[1]

    c_rows = jnp.zeros((8, d), F32).at[:bsz].set(c).at[bsz].set(c_ctx)
    mod = _modulation(c_rows, ada_w[0], ada_b)
    sh_m, sc_m, gt_m, sh_f, sc_f, gt_f = [mod[:bsz, t * d:(t + 1) * d].reshape(bsz, 1, d) for t in range(N_MOD)]
    csh_m = mod[bsz:bsz + 1, 0:d].reshape(1, 1, d)
    csc_m = mod[bsz:bsz + 1, d:2 * d].reshape(1, 1, d)

    w_in_t = jnp.swapaxes(w_in[0], 0, 1).astype(BF16)
    w_lr_t = jnp.pad(w_in_t[COL_LR:], ((0, LANES - 2 * GLA_LOWRANK), (0, 0)))

    p_ctx, lr_ctx = _inproj(ctx.reshape(bsz * lc, d), csh_m, csc_m, pre_mix_g, w_in_t, w_lr_t,
                            ("none",) * 3, tm=2 * lc, tn=512, first_col=COL_K, rows_per_mod=bsz * lc)
    x2d = x.reshape(bsz * l, d)
    p_lat, lr_lat, y_a, w_out_bf = _inproj_rows(
        x2d, sh_m, sc_m, pre_mix_g, w_in_t, w_lr_t, sgu_ln_g, sgu_ln_b, sgu_w[0].astype(BF16),
        sgu_b[0][:, :, None], cast_weights=(w_out[0],), tm=512, rows_per_mod=l)

    y_b = _gla(p_lat.reshape(bsz, l, P_COLS), lr_lat.reshape(bsz, l, LANES),
               p_ctx.reshape(bsz, lc, -1), lr_ctx.reshape(bsz, lc, LANES),
               *_gate_operands(gla_gate_w_f[0], gla_gate_b_f[0], gla_gate_w_b[0], gla_gate_b_b[0]), gla_norm_g)
    h, f, w_up_bf = _outproj(y_a, y_b.reshape(bsz * l, MIX_B), w_out_bf, x2d, gt_m,
                             post_mix_g, pre_ffn_g, sc_f, sh_f, cast_weights=(ffn_w_up[0],), rows_per_mod=l)
    g, w_down_bf = _ffn_up(f, w_up_bf, ffn_conv_w[0].reshape(9, d_ff), ffn_conv_b,
                           cast_weights=(ffn_w_down[0],), rows_per_image=l)
    out = _ffn_down(g, w_down_bf, h, gt_f, post_ffn_g, rows_per_mod=l)
    return out.reshape(bsz, l, d)
```

```python
import functools

import jax
import jax.numpy as jnp
from jax import lax
from jax.experimental import pallas as pl
from jax.experimental.pallas import tpu as pltpu

F32 = jnp.float32
BF16 = jnp.bfloat16

EPS = 1e-6
GRID_W = 64
MIX_A = 1024
MIX_B = 1024
SGU_GROUPS = 4
SGU_CHUNK = 128
SGU_GW = MIX_A // SGU_GROUPS
GLA_HEADS = 4
GLA_DK = 128
GLA_DV = 256
GLA_KDIM = GLA_HEADS * GLA_DK
GLA_LOWRANK = 16
GLA_NORMALIZER = 16.0
GLA_CHUNK = 64
N_MOD = 6
LANES = 128
SUBLANES = 8
BF16_TILE_ROWS = 16
ROW_SUBTILE = 256
VMEM_LIMIT = 56 * 1024 * 1024
FFN_DOWN_VMEM_LIMIT = 60 * 1024 * 1024

COL_U = 0
COL_VS = COL_U + MIX_A
COL_Q = COL_VS + MIX_A
COL_K = COL_Q + GLA_KDIM
COL_V = COL_K + GLA_KDIM
COL_R = COL_V + MIX_B
COL_LR = COL_R + MIX_B
N_MAIN = COL_LR
P_Q = 0
P_K = P_Q + GLA_DK
P_V = P_K + GLA_DK
P_R = P_V + GLA_DV
P_HEAD = P_R + GLA_DV
P_COLS = GLA_HEADS * P_HEAD


def _dot(a, b):
    return jnp.dot(a, b, preferred_element_type=F32)


def _nt_dot(a, b):
    return lax.dot_general(a, b, (((1,), (1,)), ((), ())), preferred_element_type=F32)


def _silu(x):
    return x / (1.0 + jnp.exp(-x))


_GELU_A = -2.0 * 0.7978845608028654 * 1.4426950408889634
_GELU_B = _GELU_A * 0.044715


def _gelu(x):
    return x / (1.0 + jnp.exp2((_GELU_A + _GELU_B * (x * x)) * x))


def _params(sem):
    return pltpu.CompilerParams(dimension_semantics=sem, vmem_limit_bytes=VMEM_LIMIT)


def _mod_kernel(c_ref, w_ref, b_ref, o_ref):
    s = _silu(c_ref[...]).astype(BF16)
    o_ref[...] = _dot(s, w_ref[...].astype(BF16)) + b_ref[...]


def _modulation(c_rows, ada_w, ada_b, tn=2048):
    rows, d = c_rows.shape
    n = ada_w.shape[1]
    return pl.pallas_call(
        _mod_kernel,
        grid=(n // tn,),
        in_specs=[pl.BlockSpec((rows, d), lambda j: (0, 0)),
                  pl.BlockSpec((d, tn), lambda j: (0, j)),
                  pl.BlockSpec((1, tn), lambda j: (0, j))],
        out_specs=pl.BlockSpec((rows, tn), lambda j: (0, j)),
        out_shape=jax.ShapeDtypeStruct((rows, n), F32),
        compiler_params=_params(("arbitrary",)),
        name="mod",
    )(c_rows, ada_w, ada_b)


def _norm_modulate(x, g_ref, sc_ref, sh_ref):
    gain = g_ref[...] * (1.0 + sc_ref[0])
    return (x * lax.rsqrt(jnp.mean(x * x, axis=-1, keepdims=True) + EPS) * gain + sh_ref[0]).astype(BF16)


def _inproj_kernel(x_ref, sh_ref, sc_ref, g_ref, w_ref, wlr_ref, p_ref, lr_ref, a_scr, *, acts):
    j = pl.program_id(1)

    @pl.when(j == 0)
    def _():
        a = _norm_modulate(x_ref[...], g_ref, sc_ref, sh_ref)
        a_scr[...] = a
        lr_ref[...] = _nt_dot(a, wlr_ref[...])

    acc = _nt_dot(a_scr[...], w_ref[...])
    fns = {"gelu": _gelu, "silu": _silu, "none": lambda t: t}
    for name in sorted(set(acts)):
        idx = [t for t, a in enumerate(acts) if a == name]
        cond = functools.reduce(jnp.logical_or, [j == t for t in idx])

        @pl.when(cond)
        def _(name=name):
            p_ref[...] = fns[name](acc).astype(p_ref.dtype)


def _inproj(x2d, shift, scale, gain, w_t, w_lr_t, acts, tm, tn, first_col, rows_per_mod):
    m, d = x2d.shape
    n = len(acts) * tn
    assert m % tm == 0 and first_col % tn == 0 and rows_per_mod % tm == 0
    tiles_per_mod = rows_per_mod // tm
    blk0 = first_col // tn
    return pl.pallas_call(
        functools.partial(_inproj_kernel, acts=acts),
        grid=(m // tm, len(acts)),
        in_specs=[pl.BlockSpec((tm, d), lambda i, j: (i, 0)),
                  pl.BlockSpec((1, 1, d), lambda i, j: (i // tiles_per_mod, 0, 0)),
                  pl.BlockSpec((1, 1, d), lambda i, j: (i // tiles_per_mod, 0, 0)),
                  pl.BlockSpec((1, d), lambda i, j: (0, 0)),
                  pl.BlockSpec((tn, d), lambda i, j: (blk0 + j, 0)),
                  pl.BlockSpec((LANES, d), lambda i, j: (0, 0))],
        out_specs=[pl.BlockSpec((tm, tn), lambda i, j: (i, j)),
                   pl.BlockSpec((tm, LANES), lambda i, j: (i, 0))],
        out_shape=[jax.ShapeDtypeStruct((m, n), BF16),
                   jax.ShapeDtypeStruct((m, LANES), F32)],
        scratch_shapes=[pltpu.VMEM((tm, d), BF16)],
        compiler_params=_params(("parallel", "arbitrary")),
        name="inproj",
    )(x2d, shift, scale, gain, w_t, w_lr_t)


def _spatial_gating(u, v, lg_ref, lb_ref, ws_ref, bs_ref, ya_ref):
    for r0 in range(0, u.shape[0], SGU_CHUNK):
        for g in range(SGU_GROUPS):
            c0 = g * SGU_GW
            vg = v[r0:r0 + SGU_CHUNK, c0:c0 + SGU_GW]
            vc = vg - jnp.mean(vg, axis=-1, keepdims=True)
            var = jnp.mean(vc * vc, axis=-1, keepdims=True)
            cs = pl.ds(c0, SGU_GW)
            vn = vc * lax.rsqrt(var + EPS) * lg_ref[:, cs] + lb_ref[:, cs]
            s = _dot(ws_ref[g], vn.astype(BF16)) + bs_ref[g]
            ya_ref[pl.ds(r0, SGU_CHUNK), cs] = (u[r0:r0 + SGU_CHUNK, c0:c0 + SGU_GW] * s).astype(ya_ref.dtype)


def _inproj_rows_kernel(x_ref, sh_ref, sc_ref, g_ref, w_ref, wlr_ref, lg_ref, lb_ref, ws_ref, bs_ref, *rest):
    n_cast = (len(rest) - 3) // 2
    p_ref, lr_ref, ya_ref = rest[n_cast:n_cast + 3]
    _cast_slabs(rest[:n_cast], rest[n_cast + 3:])
    a = _norm_modulate(x_ref[...], g_ref, sc_ref, sh_ref)
    lr_ref[...] = _nt_dot(a, wlr_ref[...])
    u = _gelu(_nt_dot(a, w_ref[pl.ds(COL_U, MIX_A), :]))
    v = _gelu(_nt_dot(a, w_ref[pl.ds(COL_VS, MIX_A), :]))
    _spatial_gating(u, v, lg_ref, lb_ref, ws_ref, bs_ref, ya_ref)
    qk = _nt_dot(a, w_ref[pl.ds(COL_Q, 2 * GLA_KDIM), :]).astype(p_ref.dtype)
    vv = _nt_dot(a, w_ref[pl.ds(COL_V, MIX_B), :]).astype(p_ref.dtype)
    rr = _silu(_nt_dot(a, w_ref[pl.ds(COL_R, MIX_B), :])).astype(p_ref.dtype)
    for h in range(GLA_HEADS):
        base = h * P_HEAD
        p_ref[:, pl.ds(base + P_Q, GLA_DK)] = qk[:, h * GLA_DK:(h + 1) * GLA_DK]
        p_ref[:, pl.ds(base + P_K, GLA_DK)] = qk[:, GLA_KDIM + h * GLA_DK:GLA_KDIM + (h + 1) * GLA_DK]
        p_ref[:, pl.ds(base + P_V, GLA_DV)] = vv[:, h * GLA_DV:(h + 1) * GLA_DV]
        p_ref[:, pl.ds(base + P_R, GLA_DV)] = rr[:, h * GLA_DV:(h + 1) * GLA_DV]


def _inproj_rows(x2d, shift, scale, gain, w_t, w_lr_t, ln_g, ln_b, w_s, b_s, cast_weights, tm, rows_per_mod):
    m, d = x2d.shape
    assert m % tm == 0 and rows_per_mod % tm == 0 and tm % SGU_CHUNK == 0
    tiles_per_mod = rows_per_mod // tm
    cast_in, cast_out, cast_shapes = _cast_specs(cast_weights, m // tm)
    whole = lambda shape: pl.BlockSpec(shape, lambda i: (0,) * len(shape))
    return pl.pallas_call(
        _inproj_rows_kernel,
        grid=(m // tm,),
        in_specs=[pl.BlockSpec((tm, d), lambda i: (i, 0)),
                  pl.BlockSpec((1, 1, d), lambda i: (i // tiles_per_mod, 0, 0)),
                  pl.BlockSpec((1, 1, d), lambda i: (i // tiles_per_mod, 0, 0)),
                  whole((1, d)),
                  pl.BlockSpec((N_MAIN, d), lambda i: (0, 0), pipeline_mode=pl.Buffered(1)),
                  pl.BlockSpec((LANES, d), lambda i: (0, 0), pipeline_mode=pl.Buffered(1)),
                  whole((1, MIX_A)), whole((1, MIX_A)),
                  whole((SGU_GROUPS, SGU_CHUNK, SGU_CHUNK)), whole((SGU_GROUPS, SGU_CHUNK, 1))] + cast_in,
        out_specs=[pl.BlockSpec((tm, P_COLS), lambda i: (i, 0)),
                   pl.BlockSpec((tm, LANES), lambda i: (i, 0)),
                   pl.BlockSpec((tm, MIX_A), lambda i: (i, 0))] + cast_out,
        out_shape=[jax.ShapeDtypeStruct((m, P_COLS), BF16),
                   jax.ShapeDtypeStruct((m, LANES), F32),
                   jax.ShapeDtypeStruct((m, MIX_A), BF16)] + cast_shapes,
        compiler_params=_params(("arbitrary",)),
        name="inproj_rows",
    )(x2d, shift, scale, gain, w_t, w_lr_t, ln_g, ln_b, w_s, b_s, *cast_weights)


GLA_BLOCK = 4 * GLA_CHUNK
GLA_PAIR = 2 * GLA_CHUNK
GLA_GATE_BLOCKS = 2
GLA_UNROLL = 4


LOG2E = 1.4426950408889634


def _log2_decay(z):
    soft = jnp.log2(1.0 + jnp.exp2(jnp.abs(z) * (-LOG2E)))
    return jnp.minimum(z, 0.0) * (LOG2E / GLA_NORMALIZER) - soft * (1.0 / GLA_NORMALIZER)


def _split(x):
    hi = x.astype(BF16)
    return hi, (x - hi.astype(F32)).astype(BF16)


def _gla_masks():
    n, c = GLA_BLOCK, GLA_CHUNK
    r = jnp.arange(n)[:, None]
    s = jnp.arange(n)[None, :]
    same = (r // c) == (s // c)
    return ((same & (s <= r)).astype(BF16), (same & (s >= r)).astype(BF16), jnp.eye(GLA_DV, dtype=BF16))


def _chunk_rows(blk, cidx, rows):
    return pl.ds(pl.multiple_of((blk * (GLA_BLOCK // GLA_CHUNK) + cidx) * rows, rows), rows)


def _gla_gates(rs, blk, q_ref, k_ref, v_ref, lr_ref, w_ref, b_ref, lower_ref, eye_ref, vt_scr, fwd, bwd):
    c, dk = GLA_CHUNK, GLA_DK
    vt_scr[:, rs] = _nt_dot(eye_ref[...], v_ref[rs, :]).astype(BF16)
    g = _log2_decay(_dot(lr_ref[rs, :].astype(BF16), w_ref[...]) + b_ref[...])
    g_hi, g_lo = _split(g)
    nb = g.shape[0]
    p = jnp.concatenate([_dot(lower_ref[...], g_hi[r0:r0 + GLA_BLOCK]) + _dot(lower_ref[...], g_lo[r0:r0 + GLA_BLOCK])
                         for r0 in range(0, nb, GLA_BLOCK)], axis=0)
    last = [p[i * c + c - 1:(i + 1) * c, :] for i in range(nb // c)]
    tot = jnp.concatenate([jnp.broadcast_to(t, (c, 2 * dk)) for t in last], axis=0)
    k32 = k_ref[rs, :].astype(F32)
    q32 = None if q_ref is None else q_ref[rs, :].astype(F32) * (GLA_DK ** -0.5)
    plans = ((fwd, slice(0, dk), p, tot - p), (bwd, slice(dk, 2 * dk), tot - p + g, p - g))
    for d, lanes, b_cum, to_end in plans:
        d["kd"][rs, :] = (k32 * jnp.exp2(to_end[:, lanes])).astype(BF16)
        for cidx in range(nb // c):
            dec = jnp.exp2(last[cidx][:, lanes])
            d["dec"][_chunk_rows(blk, cidx, SUBLANES), :] = jnp.broadcast_to(dec, (SUBLANES, dk))
        if q32 is not None:
            d["qe"][rs, :] = (q32 * jnp.exp2(b_cum[:, lanes])).astype(BF16)
            d["ke"][rs, :] = (k32 * jnp.exp2(-b_cum[:, lanes])).astype(BF16)


def _gla_scan_block(rs, blk, order, d, vt_scr, s_scr, record):
    c = GLA_CHUNK
    v_t = vt_scr[:, rs]
    kd = d["kd"][rs, :]
    zeros = jnp.zeros((c, kd.shape[1]), kd.dtype)
    s = s_scr[...]
    for cidx in order:
        pair, half = divmod(cidx, 2)
        slab = v_t[:, pair * GLA_PAIR:(pair + 1) * GLA_PAIR]
        kc = kd[cidx * c:(cidx + 1) * c]
        k_only = jnp.concatenate([kc, zeros] if half == 0 else [zeros, kc], axis=0)
        kv_t = _dot(slab, k_only)
        if record:
            d["snap"][_chunk_rows(blk, cidx, GLA_DV), :] = s.astype(BF16)
        dec = d["dec"][_chunk_rows(blk, cidx, SUBLANES), :]
        s = s * dec[0:1, :] + kv_t
    s_scr[...] = s


def _gla_output(rs, blk, v_ref, r_ref, qe_scr, snap_scr, fwd, bwd, ng_ref, y_ref):
    c, dk = GLA_CHUNK, GLA_DK
    qe = qe_scr[rs, :]
    s_f = _nt_dot(qe[:, :dk], fwd["ke"][rs, :])
    s_b = _nt_dot(qe[:, dk:], bwd["ke"][rs, :])
    att = jnp.where(fwd["tri"][...] > 0, s_f, 0.0) + jnp.where(bwd["tri"][...] > 0, s_b, 0.0)
    inter = [_nt_dot(qe[cidx * c:(cidx + 1) * c], snap_scr[_chunk_rows(blk, cidx, GLA_DV), :])
             for cidx in range(GLA_BLOCK // c)]
    o = _dot(att.astype(BF16), v_ref[rs, :]) + jnp.concatenate(inter, axis=0)
    o = o * lax.rsqrt(jnp.mean(o * o, axis=-1, keepdims=True) + EPS) * ng_ref[...]
    y_ref[rs, :] = (o * r_ref[rs, :].astype(F32)).astype(y_ref.dtype)


def _gla_kernel(p_ref, lr_ref, kc_ref, vc_ref, lrc_ref, w_ref, b_ref, ng_ref,
                lower_ref, upper_ref, eye_ref,
                y_ref, sf_scr, sb_scr, vt_scr, qe_scr, kef_scr, keb_scr, kdf_scr, kdb_scr,
                decf_scr, decb_scr, snap_scr):
    lanes_f, lanes_b = pl.ds(0, GLA_DK), pl.ds(GLA_DK, GLA_DK)
    fwd = dict(tri=lower_ref, qe=qe_scr.at[:, lanes_f], ke=kef_scr, kd=kdf_scr, dec=decf_scr,
               snap=snap_scr.at[:, lanes_f])
    bwd = dict(tri=upper_ref, qe=qe_scr.at[:, lanes_b], ke=keb_scr, kd=kdb_scr, dec=decb_scr,
               snap=snap_scr.at[:, lanes_b])
    q_ref, k_ref = p_ref.at[:, pl.ds(P_Q, GLA_DK)], p_ref.at[:, pl.ds(P_K, GLA_DK)]
    v_ref, r_ref = p_ref.at[:, pl.ds(P_V, GLA_DV)], p_ref.at[:, pl.ds(P_R, GLA_DV)]
    l, lc = p_ref.shape[0], kc_ref.shape[0]
    n_blk, n_cblk = l // GLA_BLOCK, lc // GLA_BLOCK
    asc = tuple(range(GLA_BLOCK // GLA_CHUNK))
    desc = asc[::-1]

    def block_rows(j):
        return pl.ds(pl.multiple_of(j * GLA_BLOCK, GLA_BLOCK), GLA_BLOCK)

    sf_scr[...] = jnp.zeros_like(sf_scr)
    sb_scr[...] = jnp.zeros_like(sb_scr)

    for j in range(n_cblk):
        _gla_gates(block_rows(j), j, None, kc_ref, vc_ref, lrc_ref, w_ref, b_ref, lower_ref, eye_ref,
                   vt_scr, fwd, bwd)
    for j in range(n_cblk):
        _gla_scan_block(block_rows(j), j, asc, fwd, vt_scr, sf_scr, False)
        jb = n_cblk - 1 - j
        _gla_scan_block(block_rows(jb), jb, desc, bwd, vt_scr, sb_scr, False)

    def gates(j, carry):
        rows = GLA_GATE_BLOCKS * GLA_BLOCK
        rs = pl.ds(pl.multiple_of(j * rows, rows), rows)
        _gla_gates(rs, j * GLA_GATE_BLOCKS, q_ref, k_ref, v_ref, lr_ref, w_ref, b_ref, lower_ref, eye_ref,
                   vt_scr, fwd, bwd)
        return carry

    def scan(j, carry):
        _gla_scan_block(block_rows(j), j, asc, fwd, vt_scr, sf_scr, True)
        jb = n_blk - 1 - j
        _gla_scan_block(block_rows(jb), jb, desc, bwd, vt_scr, sb_scr, True)
        return carry

    def output(j, carry):
        _gla_output(block_rows(j), j, v_ref, r_ref, qe_scr, snap_scr, fwd, bwd, ng_ref, y_ref)
        return carry

    lax.fori_loop(0, n_blk // GLA_GATE_BLOCKS, gates, 0)
    lax.fori_loop(0, n_blk, scan, 0, unroll=GLA_UNROLL)
    lax.fori_loop(0, n_blk, output, 0, unroll=GLA_UNROLL)


def _gla(p_lat, lr_lat, p_ctx, lr_ctx, gate_w, gate_b, norm_g):
    bsz, l, _ = p_lat.shape
    lc = p_ctx.shape[1]
    assert l % GLA_BLOCK == 0 and lc % GLA_BLOCK == 0 and lc <= l
    gate_specs = [pl.BlockSpec((LANES, 2 * GLA_DK), lambda b, h: (0, h)),
                  pl.BlockSpec((1, 2 * GLA_DK), lambda b, h: (0, h))]
    masks = _gla_masks()
    mask_specs = [pl.BlockSpec(m.shape, lambda b, h: (0, 0)) for m in masks]
    n_chunks = l // GLA_CHUNK
    return pl.pallas_call(
        _gla_kernel,
        grid=(bsz, GLA_HEADS),
        in_specs=[pl.BlockSpec((None, l, P_HEAD), lambda b, h: (b, 0, h)),
                  pl.BlockSpec((None, l, LANES), lambda b, h: (b, 0, 0)),
                  pl.BlockSpec((None, lc, GLA_DK), lambda b, h: (b, 0, h)),
                  pl.BlockSpec((None, lc, GLA_DV), lambda b, h: (b, 0, GLA_KDIM // GLA_DV + h)),
                  pl.BlockSpec((None, lc, LANES), lambda b, h: (b, 0, 0))]
                 + gate_specs
                 + [pl.BlockSpec((1, GLA_DV), lambda b, h: (0, 0))] + mask_specs,
        out_specs=pl.BlockSpec((None, l, GLA_DV), lambda b, h: (b, 0, h)),
        out_shape=jax.ShapeDtypeStruct((bsz, l, MIX_B), BF16),
        scratch_shapes=[pltpu.VMEM((GLA_DV, GLA_DK), F32)] * 2
                       + [pltpu.VMEM((GLA_DV, l), BF16)]
                       + [pltpu.VMEM((l, 2 * GLA_DK), BF16)]
                       + [pltpu.VMEM((l, GLA_DK), BF16)] * 4
                       + [pltpu.VMEM((n_chunks * SUBLANES, GLA_DK), F32)] * 2
                       + [pltpu.VMEM((n_chunks * GLA_DV, 2 * GLA_DK), BF16)],
        compiler_params=_params(("parallel", "arbitrary")),
        name="gla",
    )(p_lat, lr_lat, p_ctx, p_ctx, lr_ctx, gate_w, gate_b, norm_g, *masks)


def _cast_specs(weights, n_steps, step_index=lambda i: i):
    ins, outs, shapes = [], [], []
    for w in weights:
        rows, cols = w.shape
        assert rows % (n_steps * BF16_TILE_ROWS) == 0, "slabs must be whole packed bf16 row tiles"
        spec = pl.BlockSpec((rows // n_steps, cols), lambda *idx: (step_index(*idx), 0))
        ins.append(spec)
        outs.append(spec)
        shapes.append(jax.ShapeDtypeStruct(w.shape, BF16))
    return ins, outs, shapes


def _cast_slabs(src_refs, dst_refs):
    for src, dst in zip(src_refs, dst_refs):
        dst[...] = src[...].astype(dst.dtype)


def _rms(x, g):
    return x * lax.rsqrt(jnp.mean(x * x, axis=-1, keepdims=True) + EPS) * g


def _outproj_kernel(ya_ref, yb_ref, wa_ref, wb_ref, x_ref, gt_ref, pg_ref, fg_ref, sc_ref, sh_ref, *rest):
    n_cast = (len(rest) - 2) // 2
    h_ref, f_ref = rest[n_cast:n_cast + 2]
    _cast_slabs(rest[:n_cast], rest[n_cast + 2:])
    gate_gain = gt_ref[0] * pg_ref[...]
    mod_gain = fg_ref[...] * (1.0 + sc_ref[0])
    for r0 in range(0, h_ref.shape[0], ROW_SUBTILE):
        rs = pl.ds(r0, ROW_SUBTILE)
        y = _dot(ya_ref[rs, :], wa_ref[...]) + _dot(yb_ref[rs, :], wb_ref[...])
        h = x_ref[rs, :] + _rms(y, gate_gain)
        h_ref[rs, :] = h
        f_ref[rs, :] = (_rms(h, mod_gain) + sh_ref[0]).astype(f_ref.dtype)


def _outproj(y_a, y_b, w_out, x2d, gate, post_g, ffn_g, scale_f, shift_f, cast_weights, rows_per_mod, tm=512):
    m, d = x2d.shape
    tiles_per_mod = rows_per_mod // tm
    mod_spec = pl.BlockSpec((1, 1, d), lambda i: (i // tiles_per_mod, 0, 0))
    vec_spec = pl.BlockSpec((1, d), lambda i: (0, 0))
    cast_in, cast_out, cast_shapes = _cast_specs(cast_weights, m // tm)
    return pl.pallas_call(
        _outproj_kernel,
        grid=(m // tm,),
        in_specs=[pl.BlockSpec((tm, MIX_A), lambda i: (i, 0)),
                  pl.BlockSpec((tm, MIX_B), lambda i: (i, 0)),
                  pl.BlockSpec((MIX_A, d), lambda i: (0, 0), pipeline_mode=pl.Buffered(1)),
                  pl.BlockSpec((MIX_B, d), lambda i: (MIX_A // MIX_B, 0), pipeline_mode=pl.Buffered(1)),
                  pl.BlockSpec((tm, d), lambda i: (i, 0)),
                  mod_spec, vec_spec, vec_spec, mod_spec, mod_spec] + cast_in,
        out_specs=[pl.BlockSpec((tm, d), lambda i: (i, 0)),
                   pl.BlockSpec((tm, d), lambda i: (i, 0))] + cast_out,
        out_shape=[jax.ShapeDtypeStruct((m, d), F32),
                   jax.ShapeDtypeStruct((m, d), BF16)] + cast_shapes,
        compiler_params=_params(("arbitrary",)),
        name="outproj",
    )(y_a, y_b, w_out, w_out, x2d, gate, post_g, ffn_g, scale_f, shift_f, *cast_weights)


def _conv_gelu(a, cw, cb):
    hw = GRID_W
    ext = a.shape[0]
    tm = ext - 2 * hw
    rows = [a[dr * hw:dr * hw + tm, :] for dr in range(3)]
    taps = [sum(cw[3 * dr + dc:3 * dr + dc + 1, :] * rows[dr] for dr in range(3)) for dc in range(3)]
    col = lax.broadcasted_iota(jnp.int32, (tm, 1), 0) % hw
    left = jnp.where(col > 0, pltpu.roll(taps[0], 1, 0), 0.0)
    right = jnp.where(col < hw - 1, pltpu.roll(taps[2], tm - 1, 0), 0.0)
    return _gelu(cb + taps[1] + left + right)


def _ffn_up_kernel(f_ref, fp_ref, fn_ref, wa_ref, wv_ref, cw_ref, cb_ref, *rest, tiles_per_image):
    n_cast = (len(rest) - 2) // 2
    g_ref, fext_scr = rest[n_cast], rest[-1]
    _cast_slabs(rest[:n_cast], rest[n_cast + 1:-1])
    i = pl.program_id(0)
    j = pl.program_id(1)
    tm = f_ref.shape[0]
    hw = GRID_W

    @pl.when(j == 0)
    def _():
        top = (i % tiles_per_image) == 0
        bottom = (i % tiles_per_image) == tiles_per_image - 1
        fext_scr[pl.ds(0, hw), :] = jnp.where(top, jnp.zeros_like(fp_ref), fp_ref[...])
        fext_scr[pl.ds(hw, tm), :] = f_ref[...]
        fext_scr[pl.ds(hw + tm, hw), :] = jnp.where(bottom, jnp.zeros_like(fn_ref), fn_ref[...])

    a = _dot(fext_scr[...], wa_ref[...])
    val = _dot(f_ref[...], wv_ref[...])
    g_ref[...] = (_conv_gelu(a, cw_ref[...], cb_ref[...]) * val).astype(g_ref.dtype)


def _ffn_up(f2d, w_up, conv_w, conv_b, cast_weights, rows_per_image, tm=1024, tf=512):
    m, d = f2d.shape
    d_ff = w_up.shape[1] // 2
    assert rows_per_image % tm == 0 and tm % GRID_W == 0 and d_ff % tf == 0
    hb = tm // GRID_W
    n_halo = m // GRID_W
    n_j = d_ff // tf
    cast_in, cast_out, cast_shapes = _cast_specs(cast_weights, (m // tm) * n_j, lambda i, j: i * n_j + j)
    return pl.pallas_call(
        functools.partial(_ffn_up_kernel, tiles_per_image=rows_per_image // tm),
        grid=(m // tm, n_j),
        in_specs=[pl.BlockSpec((tm, d), lambda i, j: (i, 0)),
                  pl.BlockSpec((GRID_W, d), lambda i, j: (jnp.maximum(i * hb - 1, 0), 0)),
                  pl.BlockSpec((GRID_W, d), lambda i, j: (jnp.minimum((i + 1) * hb, n_halo - 1), 0)),
                  pl.BlockSpec((d, tf), lambda i, j: (0, j)),
                  pl.BlockSpec((d, tf), lambda i, j: (0, n_j + j)),
                  pl.BlockSpec((9, tf), lambda i, j: (0, j)),
                  pl.BlockSpec((1, tf), lambda i, j: (0, j))] + cast_in,
        out_specs=[pl.BlockSpec((tm, tf), lambda i, j: (i, j))] + cast_out,
        out_shape=[jax.ShapeDtypeStruct((m, d_ff), BF16)] + cast_shapes,
        scratch_shapes=[pltpu.VMEM((tm + 2 * GRID_W, d), BF16)],
        compiler_params=_params(("arbitrary", "arbitrary")),
        name="ffn_up",
    )(f2d, f2d, f2d, w_up, w_up, conv_w, conv_b, *cast_weights)


def _ffn_down_kernel(g_ref, w_ref, h_ref, gt_ref, pg_ref, o_ref):
    gate_gain = gt_ref[0] * pg_ref[...]
    for r0 in range(0, o_ref.shape[0], ROW_SUBTILE):
        rs = pl.ds(r0, ROW_SUBTILE)
        y = _dot(g_ref[rs, :], w_ref[...])
        o_ref[rs, :] = h_ref[rs, :] + _rms(y, gate_gain)


def _ffn_down(g2d, w_down, h2d, gate, post_g, rows_per_mod, tm=512):
    m, d = h2d.shape
    d_ff = g2d.shape[1]
    assert m % tm == 0
    tiles_per_mod = rows_per_mod // tm
    return pl.pallas_call(
        _ffn_down_kernel,
        grid=(m // tm,),
        in_specs=[pl.BlockSpec((tm, d_ff), lambda i: (i, 0)),
                  pl.BlockSpec((d_ff, d), lambda i: (0, 0), pipeline_mode=pl.Buffered(1)),
                  pl.BlockSpec((tm, d), lambda i: (i, 0)),
                  pl.BlockSpec((1, 1, d), lambda i: (i // tiles_per_mod, 0, 0)),
                  pl.BlockSpec((1, d), lambda i: (0, 0))],
        out_specs=pl.BlockSpec((tm, d), lambda i: (i, 0)),
        out_shape=jax.ShapeDtypeStruct((m, d), F32),
        compiler_params=pltpu.CompilerParams(dimension_semantics=("arbitrary",),
                                             vmem_limit_bytes=FFN_DOWN_VMEM_LIMIT),
        name="ffn_down",
    )(g2d, w_down, h2d, gate, post_g)


def _gate_operands(w_f, b_f, w_b, b_b):
    r, kdim = w_f.shape
    wp = jnp.zeros((LANES, GLA_HEADS, 2, GLA_DK), BF16)
    wp = wp.at[:r, :, 0].set(w_f.astype(BF16).reshape(r, GLA_HEADS, GLA_DK))
    wp = wp.at[r:2 * r, :, 1].set(w_b.astype(BF16).reshape(r, GLA_HEADS, GLA_DK))
    bias = jnp.stack([b_f.reshape(GLA_HEADS, GLA_DK), b_b.reshape(GLA_HEADS, GLA_DK)], axis=1)
    return wp.reshape(LANES, 2 * kdim), bias.reshape(1, 2 * kdim)


def kernel(x, c, ctx, c_ctx, ada_w, ada_b, pre_mix_g, post_mix_g, pre_ffn_g, post_ffn_g, w_in, sgu_ln_g, sgu_ln_b, sgu_w, sgu_b, gla_gate_w_f, gla_gate_b_f, gla_gate_w_b, gla_gate_b_b, gla_norm_g, w_out, ffn_w_up, ffn_conv_w, ffn_conv_b, ffn_w_down):
    bsz, l, d = x.shape
    lc = ctx.shape[1]
    assert ada_w.shape[0] == 1, "single-layer kernel"
    d_ff = ffn_w_down.shape[1]

    assert bsz < SUBLANES
    c_rows = jnp.zeros((SUBLANES, d), F32).at[:bsz].set(c).at[bsz].set(c_ctx)
    mod = _modulation(c_rows, ada_w[0], ada_b)
    sh_m, sc_m, gt_m, sh_f, sc_f, gt_f = [mod[:bsz, t * d:(t + 1) * d].reshape(bsz, 1, d) for t in range(N_MOD)]
    csh_m = mod[bsz:bsz + 1, 0:d].reshape(1, 1, d)
    csc_m = mod[bsz:bsz + 1, d:2 * d].reshape(1, 1, d)

    w_in_t = jnp.swapaxes(w_in[0], 0, 1).astype(BF16)
    w_lr_t = jnp.pad(w_in_t[COL_LR:], ((0, LANES - 2 * GLA_LOWRANK), (0, 0)))

    p_ctx, lr_ctx = _inproj(ctx.reshape(bsz * lc, d), csh_m, csc_m, pre_mix_g, w_in_t, w_lr_t,
                            ("none",) * 3, tm=2 * lc, tn=512, first_col=COL_K, rows_per_mod=bsz * lc)
    x2d = x.reshape(bsz * l, d)
    p_lat, lr_lat, y_a, w_out_bf = _inproj_rows(
        x2d, sh_m, sc_m, pre_mix_g, w_in_t, w_lr_t, sgu_ln_g, sgu_ln_b, sgu_w[0].astype(BF16),
        sgu_b[0][:, :, None], cast_weights=(w_out[0],), tm=512, rows_per_mod=l)

    y_b = _gla(p_lat.reshape(bsz, l, P_COLS), lr_lat.reshape(bsz, l, LANES),
               p_ctx.reshape(bsz, lc, -1), lr_ctx.reshape(bsz, lc, LANES),
               *_gate_operands(gla_gate_w_f[0], gla_gate_b_f[0], gla_gate_w_b[0], gla_gate_b_b[0]), gla_norm_g)
    h, f, w_up_bf = _outproj(y_a, y_b.reshape(bsz * l, MIX_B), w_out_bf, x2d, gt_m,
                             post_mix_g, pre_ffn_g, sc_f, sh_f, cast_weights=(ffn_w_up[0],), rows_per_mod=l)
    g, w_down_bf = _ffn_up(f, w_up_bf, ffn_conv_w[0].reshape(9, d_ff), ffn_conv_b,
                           cast_weights=(ffn_w_down[0],), rows_per_image=l)
    out = _ffn_down(g, w_down_bf, h, gt_f, post_ffn_g, rows_per_mod=l)
    return out.reshape(bsz, l, d)
```

```python
import functools

import jax
import jax.numpy as jnp
from jax import lax
from jax.experimental import pallas as pl
from jax.experimental.pallas import tpu as pltpu

F32 = jnp.float32
BF16 = jnp.bfloat16

EPS = 1e-6
GRID_W = 64
MIX_A = 1024
MIX_B = 1024
SGU_GROUPS = 4
SGU_CHUNK = 128
SGU_GW = MIX_A // SGU_GROUPS
GLA_HEADS = 4
GLA_DK = 128
GLA_DV = 256
GLA_KDIM = GLA_HEADS * GLA_DK
GLA_LOWRANK = 16
GLA_NORMALIZER = 16.0
GLA_CHUNK = 64
N_MOD = 6
LANES = 128
SUBLANES = 8
BF16_TILE_ROWS = 16
ROW_SUBTILE = 256
VMEM_LIMIT = 56 * 1024 * 1024
FFN_DOWN_VMEM_LIMIT = 60 * 1024 * 1024

COL_U = 0
COL_VS = COL_U + MIX_A
COL_Q = COL_VS + MIX_A
COL_K = COL_Q + GLA_KDIM
COL_V = COL_K + GLA_KDIM
COL_R = COL_V + MIX_B
COL_LR = COL_R + MIX_B
N_MAIN = COL_LR
P_Q = 0
P_K = P_Q + GLA_DK
P_V = P_K + GLA_DK
P_R = P_V + GLA_DV
P_HEAD = P_R + GLA_DV
P_COLS = GLA_HEADS * P_HEAD


def _dot(a, b):
    return jnp.dot(a, b, preferred_element_type=F32)


def _nt_dot(a, b):
    return lax.dot_general(a, b, (((1,), (1,)), ((), ())), preferred_element_type=F32)


def _silu(x):
    return x / (1.0 + jnp.exp(-x))


_GELU_A = -2.0 * 0.7978845608028654 * 1.4426950408889634
_GELU_B = _GELU_A * 0.044715


def _gelu(x):
    return x / (1.0 + jnp.exp2((_GELU_A + _GELU_B * (x * x)) * x))


def _params(sem):
    return pltpu.CompilerParams(dimension_semantics=sem, vmem_limit_bytes=VMEM_LIMIT)


def _mod_kernel(c_ref, w_ref, b_ref, o_ref):
    s = _silu(c_ref[...]).astype(BF16)
    o_ref[...] = _dot(s, w_ref[...].astype(BF16)) + b_ref[...]


def _modulation(c_rows, ada_w, ada_b, tn=1024):
    rows, d = c_rows.shape
    n = ada_w.shape[1]
    return pl.pallas_call(
        _mod_kernel,
        grid=(n // tn,),
        in_specs=[pl.BlockSpec((rows, d), lambda j: (0, 0)),
                  pl.BlockSpec((d, tn), lambda j: (0, j)),
                  pl.BlockSpec((1, tn), lambda j: (0, j))],
        out_specs=pl.BlockSpec((rows, tn), lambda j: (0, j)),
        out_shape=jax.ShapeDtypeStruct((rows, n), F32),
        compiler_params=_params(("arbitrary",)),
        name="mod",
    )(c_rows, ada_w, ada_b)


def _norm_modulate(x, g_ref, sc_ref, sh_ref):
    gain = g_ref[...] * (1.0 + sc_ref[0])
    return (x * lax.rsqrt(jnp.mean(x * x, axis=-1, keepdims=True) + EPS) * gain + sh_ref[0]).astype(BF16)


def _inproj_kernel(x_ref, sh_ref, sc_ref, g_ref, w_ref, wlr_ref, p_ref, lr_ref, a_scr):
    @pl.when(pl.program_id(1) == 0)
    def _():
        a = _norm_modulate(x_ref[...], g_ref, sc_ref, sh_ref)
        a_scr[...] = a
        lr_ref[...] = _nt_dot(a, wlr_ref[...])

    p_ref[...] = _nt_dot(a_scr[...], w_ref[...]).astype(p_ref.dtype)


def _inproj(x2d, shift, scale, gain, w_t, w_lr_t, tm, tn, first_col, n_cols, rows_per_mod):
    m, d = x2d.shape
    n = n_cols
    assert m % tm == 0 and first_col % tn == 0 and n_cols % tn == 0 and rows_per_mod % tm == 0
    tiles_per_mod = rows_per_mod // tm
    blk0 = first_col // tn
    return pl.pallas_call(
        _inproj_kernel,
        grid=(m // tm, n_cols // tn),
        in_specs=[pl.BlockSpec((tm, d), lambda i, j: (i, 0)),
                  pl.BlockSpec((1, 1, d), lambda i, j: (i // tiles_per_mod, 0, 0)),
                  pl.BlockSpec((1, 1, d), lambda i, j: (i // tiles_per_mod, 0, 0)),
                  pl.BlockSpec((1, d), lambda i, j: (0, 0)),
                  pl.BlockSpec((tn, d), lambda i, j: (blk0 + j, 0)),
                  pl.BlockSpec((LANES, d), lambda i, j: (0, 0))],
        out_specs=[pl.BlockSpec((tm, tn), lambda i, j: (i, j)),
                   pl.BlockSpec((tm, LANES), lambda i, j: (i, 0))],
        out_shape=[jax.ShapeDtypeStruct((m, n), BF16),
                   jax.ShapeDtypeStruct((m, LANES), F32)],
        scratch_shapes=[pltpu.VMEM((tm, d), BF16)],
        compiler_params=_params(("parallel", "arbitrary")),
        name="inproj",
    )(x2d, shift, scale, gain, w_t, w_lr_t)


def _spatial_gating(u, v, lg_ref, lb_ref, ws_ref, bs_ref, ya_ref):
    for r0 in range(0, u.shape[0], SGU_CHUNK):
        for g in range(SGU_GROUPS):
            c0 = g * SGU_GW
            vg = v[r0:r0 + SGU_CHUNK, c0:c0 + SGU_GW]
            vc = vg - jnp.mean(vg, axis=-1, keepdims=True)
            var = jnp.mean(vc * vc, axis=-1, keepdims=True)
            cs = pl.ds(c0, SGU_GW)
            vn = vc * lax.rsqrt(var + EPS) * lg_ref[:, cs] + lb_ref[:, cs]
            s = _dot(ws_ref[g], vn.astype(BF16)) + bs_ref[g]
            ya_ref[pl.ds(r0, SGU_CHUNK), cs] = (u[r0:r0 + SGU_CHUNK, c0:c0 + SGU_GW] * s).astype(ya_ref.dtype)


def _inproj_rows_kernel(x_ref, sh_ref, sc_ref, g_ref, w_ref, wlr_ref, lg_ref, lb_ref, ws_ref, bs_ref, *rest):
    n_cast = (len(rest) - 3) // 2
    p_ref, lr_ref, ya_ref = rest[n_cast:n_cast + 3]
    _cast_slabs(rest[:n_cast], rest[n_cast + 3:])
    a = _norm_modulate(x_ref[...], g_ref, sc_ref, sh_ref)
    lr_ref[...] = _nt_dot(a, wlr_ref[...])
    u = _gelu(_nt_dot(a, w_ref[pl.ds(COL_U, MIX_A), :]))
    v = _gelu(_nt_dot(a, w_ref[pl.ds(COL_VS, MIX_A), :]))
    _spatial_gating(u, v, lg_ref, lb_ref, ws_ref, bs_ref, ya_ref)
    qk = _nt_dot(a, w_ref[pl.ds(COL_Q, 2 * GLA_KDIM), :]).astype(p_ref.dtype)
    vv = _nt_dot(a, w_ref[pl.ds(COL_V, MIX_B), :]).astype(p_ref.dtype)
    rr = _silu(_nt_dot(a, w_ref[pl.ds(COL_R, MIX_B), :])).astype(p_ref.dtype)
    for h in range(GLA_HEADS):
        base = h * P_HEAD
        p_ref[:, pl.ds(base + P_Q, GLA_DK)] = qk[:, h * GLA_DK:(h + 1) * GLA_DK]
        p_ref[:, pl.ds(base + P_K, GLA_DK)] = qk[:, GLA_KDIM + h * GLA_DK:GLA_KDIM + (h + 1) * GLA_DK]
        p_ref[:, pl.ds(base + P_V, GLA_DV)] = vv[:, h * GLA_DV:(h + 1) * GLA_DV]
        p_ref[:, pl.ds(base + P_R, GLA_DV)] = rr[:, h * GLA_DV:(h + 1) * GLA_DV]


def _inproj_rows(x2d, shift, scale, gain, w_t, w_lr_t, ln_g, ln_b, w_s, b_s, cast_weights, tm, rows_per_mod):
    m, d = x2d.shape
    assert m % tm == 0 and rows_per_mod % tm == 0 and tm % SGU_CHUNK == 0
    tiles_per_mod = rows_per_mod // tm
    cast_in, cast_out, cast_shapes = _cast_specs(cast_weights, m // tm)
    whole = lambda shape: pl.BlockSpec(shape, lambda i: (0,) * len(shape))
    return pl.pallas_call(
        _inproj_rows_kernel,
        grid=(m // tm,),
        in_specs=[pl.BlockSpec((tm, d), lambda i: (i, 0)),
                  pl.BlockSpec((1, 1, d), lambda i: (i // tiles_per_mod, 0, 0)),
                  pl.BlockSpec((1, 1, d), lambda i: (i // tiles_per_mod, 0, 0)),
                  whole((1, d)),
                  pl.BlockSpec((N_MAIN, d), lambda i: (0, 0), pipeline_mode=pl.Buffered(1)),
                  pl.BlockSpec((LANES, d), lambda i: (0, 0), pipeline_mode=pl.Buffered(1)),
                  whole((1, MIX_A)), whole((1, MIX_A)),
                  whole((SGU_GROUPS, SGU_CHUNK, SGU_CHUNK)), whole((SGU_GROUPS, SGU_CHUNK, 1))] + cast_in,
        out_specs=[pl.BlockSpec((tm, P_COLS), lambda i: (i, 0)),
                   pl.BlockSpec((tm, LANES), lambda i: (i, 0)),
                   pl.BlockSpec((tm, MIX_A), lambda i: (i, 0))] + cast_out,
        out_shape=[jax.ShapeDtypeStruct((m, P_COLS), BF16),
                   jax.ShapeDtypeStruct((m, LANES), F32),
                   jax.ShapeDtypeStruct((m, MIX_A), BF16)] + cast_shapes,
        compiler_params=_params(("arbitrary",)),
        name="inproj_rows",
    )(x2d, shift, scale, gain, w_t, w_lr_t, ln_g, ln_b, w_s, b_s, *cast_weights)


GLA_BLOCK = 4 * GLA_CHUNK
GLA_PAIR = 2 * GLA_CHUNK
GLA_GATE_BLOCKS = 2
GLA_UNROLL = 4


LOG2E = 1.4426950408889634


def _log2_decay(z):
    soft = jnp.log2(1.0 + jnp.exp2(jnp.abs(z) * (-LOG2E)))
    return jnp.minimum(z, 0.0) * (LOG2E / GLA_NORMALIZER) - soft * (1.0 / GLA_NORMALIZER)


def _split(x):
    hi = x.astype(BF16)
    return hi, (x - hi.astype(F32)).astype(BF16)


def _gla_masks():
    n, c = GLA_BLOCK, GLA_CHUNK
    r = jnp.arange(n)[:, None]
    s = jnp.arange(n)[None, :]
    same = (r // c) == (s // c)
    return ((same & (s <= r)).astype(BF16), (same & (s >= r)).astype(BF16), jnp.eye(GLA_DV, dtype=BF16))


def _chunk_rows(blk, cidx, rows):
    return pl.ds(pl.multiple_of((blk * (GLA_BLOCK // GLA_CHUNK) + cidx) * rows, rows), rows)


def _gla_gates(rs, blk, q_ref, k_ref, v_ref, lr_ref, w_ref, b_ref, lower_ref, eye_ref, vt_scr, fwd, bwd):
    c, dk = GLA_CHUNK, GLA_DK
    vt_scr[:, rs] = _nt_dot(eye_ref[...], v_ref[rs, :]).astype(BF16)
    g = _log2_decay(_dot(lr_ref[rs, :].astype(BF16), w_ref[...]) + b_ref[...])
    g_hi, g_lo = _split(g)
    nb = g.shape[0]
    p = jnp.concatenate([_dot(lower_ref[...], g_hi[r0:r0 + GLA_BLOCK]) + _dot(lower_ref[...], g_lo[r0:r0 + GLA_BLOCK])
                         for r0 in range(0, nb, GLA_BLOCK)], axis=0)
    last = [p[i * c + c - 1:(i + 1) * c, :] for i in range(nb // c)]
    tot = jnp.concatenate([jnp.broadcast_to(t, (c, 2 * dk)) for t in last], axis=0)
    k32 = k_ref[rs, :].astype(F32)
    q32 = None if q_ref is None else q_ref[rs, :].astype(F32) * (GLA_DK ** -0.5)
    plans = ((fwd, slice(0, dk), p, tot - p), (bwd, slice(dk, 2 * dk), tot - p + g, p - g))
    for d, lanes, b_cum, to_end in plans:
        d["kd"][rs, :] = (k32 * jnp.exp2(to_end[:, lanes])).astype(BF16)
        for cidx in range(nb // c):
            dec = jnp.exp2(last[cidx][:, lanes])
            d["dec"][_chunk_rows(blk, cidx, SUBLANES), :] = jnp.broadcast_to(dec, (SUBLANES, dk))
        if q32 is not None:
            d["qe"][rs, :] = (q32 * jnp.exp2(b_cum[:, lanes])).astype(BF16)
            d["ke"][rs, :] = (k32 * jnp.exp2(-b_cum[:, lanes])).astype(BF16)


def _gla_scan_block(rs, blk, order, d, vt_scr, s_scr, record):
    c = GLA_CHUNK
    v_t = vt_scr[:, rs]
    kd = d["kd"][rs, :]
    zeros = jnp.zeros((c, kd.shape[1]), kd.dtype)
    s = s_scr[...]
    for cidx in order:
        pair, half = divmod(cidx, 2)
        slab = v_t[:, pair * GLA_PAIR:(pair + 1) * GLA_PAIR]
        kc = kd[cidx * c:(cidx + 1) * c]
        k_only = jnp.concatenate([kc, zeros] if half == 0 else [zeros, kc], axis=0)
        kv_t = _dot(slab, k_only)
        if record:
            d["snap"][_chunk_rows(blk, cidx, GLA_DV), :] = s.astype(BF16)
        dec = d["dec"][_chunk_rows(blk, cidx, SUBLANES), :]
        s = s * dec[0:1, :] + kv_t
    s_scr[...] = s


def _gla_output(rs, blk, v_ref, r_ref, qe_scr, snap_scr, fwd, bwd, ng_ref, y_ref):
    c, dk = GLA_CHUNK, GLA_DK
    qe = qe_scr[rs, :]
    s_f = _nt_dot(qe[:, :dk], fwd["ke"][rs, :])
    s_b = _nt_dot(qe[:, dk:], bwd["ke"][rs, :])
    att = jnp.where(fwd["tri"][...] > 0, s_f, 0.0) + jnp.where(bwd["tri"][...] > 0, s_b, 0.0)
    inter = [_nt_dot(qe[cidx * c:(cidx + 1) * c], snap_scr[_chunk_rows(blk, cidx, GLA_DV), :])
             for cidx in range(GLA_BLOCK // c)]
    o = _dot(att.astype(BF16), v_ref[rs, :]) + jnp.concatenate(inter, axis=0)
    o = o * lax.rsqrt(jnp.mean(o * o, axis=-1, keepdims=True) + EPS) * ng_ref[...]
    y_ref[rs, :] = (o * r_ref[rs, :].astype(F32)).astype(y_ref.dtype)


def _gla_kernel(p_ref, lr_ref, kc_ref, vc_ref, lrc_ref, w_ref, b_ref, ng_ref,
                lower_ref, upper_ref, eye_ref,
                y_ref, sf_scr, sb_scr, vt_scr, qe_scr, kef_scr, keb_scr, kdf_scr, kdb_scr,
                decf_scr, decb_scr, snap_scr):
    lanes_f, lanes_b = pl.ds(0, GLA_DK), pl.ds(GLA_DK, GLA_DK)
    fwd = dict(tri=lower_ref, qe=qe_scr.at[:, lanes_f], ke=kef_scr, kd=kdf_scr, dec=decf_scr,
               snap=snap_scr.at[:, lanes_f])
    bwd = dict(tri=upper_ref, qe=qe_scr.at[:, lanes_b], ke=keb_scr, kd=kdb_scr, dec=decb_scr,
               snap=snap_scr.at[:, lanes_b])
    q_ref, k_ref = p_ref.at[:, pl.ds(P_Q, GLA_DK)], p_ref.at[:, pl.ds(P_K, GLA_DK)]
    v_ref, r_ref = p_ref.at[:, pl.ds(P_V, GLA_DV)], p_ref.at[:, pl.ds(P_R, GLA_DV)]
    l, lc = p_ref.shape[0], kc_ref.shape[0]
    n_blk, n_cblk = l // GLA_BLOCK, lc // GLA_BLOCK
    asc = tuple(range(GLA_BLOCK // GLA_CHUNK))
    desc = asc[::-1]

    def block_rows(j):
        return pl.ds(pl.multiple_of(j * GLA_BLOCK, GLA_BLOCK), GLA_BLOCK)

    sf_scr[...] = jnp.zeros_like(sf_scr)
    sb_scr[...] = jnp.zeros_like(sb_scr)

    for j in range(n_cblk):
        _gla_gates(block_rows(j), j, None, kc_ref, vc_ref, lrc_ref, w_ref, b_ref, lower_ref, eye_ref,
                   vt_scr, fwd, bwd)
    for j in range(n_cblk):
        _gla_scan_block(block_rows(j), j, asc, fwd, vt_scr, sf_scr, False)
        jb = n_cblk - 1 - j
        _gla_scan_block(block_rows(jb), jb, desc, bwd, vt_scr, sb_scr, False)

    def gates(j, carry):
        rows = GLA_GATE_BLOCKS * GLA_BLOCK
        rs = pl.ds(pl.multiple_of(j * rows, rows), rows)
        _gla_gates(rs, j * GLA_GATE_BLOCKS, q_ref, k_ref, v_ref, lr_ref, w_ref, b_ref, lower_ref, eye_ref,
                   vt_scr, fwd, bwd)
        return carry

    def scan(j, carry):
        _gla_scan_block(block_rows(j), j, asc, fwd, vt_scr, sf_scr, True)
        jb = n_blk - 1 - j
        _gla_scan_block(block_rows(jb), jb, desc, bwd, vt_scr, sb_scr, True)
        return carry

    def output(j, carry):
        _gla_output(block_rows(j), j, v_ref, r_ref, qe_scr, snap_scr, fwd, bwd, ng_ref, y_ref)
        return carry

    lax.fori_loop(0, n_blk // GLA_GATE_BLOCKS, gates, 0)
    lax.fori_loop(0, n_blk, scan, 0, unroll=GLA_UNROLL)
    lax.fori_loop(0, n_blk, output, 0, unroll=GLA_UNROLL)


def _gla(p_lat, lr_lat, p_ctx, lr_ctx, gate_w, gate_b, norm_g):
    bsz, l, _ = p_lat.shape
    lc = p_ctx.shape[1]
    assert l % GLA_BLOCK == 0 and lc % GLA_BLOCK == 0 and lc <= l
    gate_specs = [pl.BlockSpec((LANES, 2 * GLA_DK), lambda b, h: (0, h)),
                  pl.BlockSpec((1, 2 * GLA_DK), lambda b, h: (0, h))]
    masks = _gla_masks()
    mask_specs = [pl.BlockSpec(m.shape, lambda b, h: (0, 0)) for m in masks]
    n_chunks = l // GLA_CHUNK
    return pl.pallas_call(
        _gla_kernel,
        grid=(bsz, GLA_HEADS),
        in_specs=[pl.BlockSpec((None, l, P_HEAD), lambda b, h: (b, 0, h)),
                  pl.BlockSpec((None, l, LANES), lambda b, h: (b, 0, 0)),
                  pl.BlockSpec((None, lc, GLA_DK), lambda b, h: (b, 0, h)),
                  pl.BlockSpec((None, lc, GLA_DV), lambda b, h: (b, 0, GLA_KDIM // GLA_DV + h)),
                  pl.BlockSpec((None, lc, LANES), lambda b, h: (b, 0, 0))]
                 + gate_specs
                 + [pl.BlockSpec((1, GLA_DV), lambda b, h: (0, 0))] + mask_specs,
        out_specs=pl.BlockSpec((None, l, GLA_DV), lambda b, h: (b, 0, h)),
        out_shape=jax.ShapeDtypeStruct((bsz, l, MIX_B), BF16),
        scratch_shapes=[pltpu.VMEM((GLA_DV, GLA_DK), F32)] * 2
                       + [pltpu.VMEM((GLA_DV, l), BF16)]
                       + [pltpu.VMEM((l, 2 * GLA_DK), BF16)]
                       + [pltpu.VMEM((l, GLA_DK), BF16)] * 4
                       + [pltpu.VMEM((n_chunks * SUBLANES, GLA_DK), F32)] * 2
                       + [pltpu.VMEM((n_chunks * GLA_DV, 2 * GLA_DK), BF16)],
        compiler_params=_params(("parallel", "arbitrary")),
        name="gla",
    )(p_lat, lr_lat, p_ctx, p_ctx, lr_ctx, gate_w, gate_b, norm_g, *masks)


def _cast_specs(weights, n_steps, step_index=lambda i: i):
    ins, outs, shapes = [], [], []
    for w in weights:
        rows, cols = w.shape
        assert rows % (n_steps * BF16_TILE_ROWS) == 0, "slabs must be whole packed bf16 row tiles"
        spec = pl.BlockSpec((rows // n_steps, cols), lambda *idx: (step_index(*idx), 0))
        ins.append(spec)
        outs.append(spec)
        shapes.append(jax.ShapeDtypeStruct(w.shape, BF16))
    return ins, outs, shapes


def _cast_slabs(src_refs, dst_refs):
    for src, dst in zip(src_refs, dst_refs):
        dst[...] = src[...].astype(dst.dtype)


def _rms(x, g):
    return x * lax.rsqrt(jnp.mean(x * x, axis=-1, keepdims=True) + EPS) * g


def _outproj_kernel(ya_ref, yb_ref, wa_ref, wb_ref, x_ref, gt_ref, pg_ref, fg_ref, sc_ref, sh_ref, *rest):
    n_cast = (len(rest) - 2) // 2
    h_ref, f_ref = rest[n_cast:n_cast + 2]
    _cast_slabs(rest[:n_cast], rest[n_cast + 2:])
    gate_gain = gt_ref[0] * pg_ref[...]
    mod_gain = fg_ref[...] * (1.0 + sc_ref[0])
    for r0 in range(0, h_ref.shape[0], ROW_SUBTILE):
        rs = pl.ds(r0, ROW_SUBTILE)
        y = _dot(ya_ref[rs, :], wa_ref[...]) + _dot(yb_ref[rs, :], wb_ref[...])
        h = x_ref[rs, :] + _rms(y, gate_gain)
        h_ref[rs, :] = h
        f_ref[rs, :] = (_rms(h, mod_gain) + sh_ref[0]).astype(f_ref.dtype)


def _outproj(y_a, y_b, w_out, x2d, gate, post_g, ffn_g, scale_f, shift_f, cast_weights, rows_per_mod, tm=512):
    m, d = x2d.shape
    tiles_per_mod = rows_per_mod // tm
    mod_spec = pl.BlockSpec((1, 1, d), lambda i: (i // tiles_per_mod, 0, 0))
    vec_spec = pl.BlockSpec((1, d), lambda i: (0, 0))
    cast_in, cast_out, cast_shapes = _cast_specs(cast_weights, m // tm)
    return pl.pallas_call(
        _outproj_kernel,
        grid=(m // tm,),
        in_specs=[pl.BlockSpec((tm, MIX_A), lambda i: (i, 0)),
                  pl.BlockSpec((tm, MIX_B), lambda i: (i, 0)),
                  pl.BlockSpec((MIX_A, d), lambda i: (0, 0), pipeline_mode=pl.Buffered(1)),
                  pl.BlockSpec((MIX_B, d), lambda i: (MIX_A // MIX_B, 0), pipeline_mode=pl.Buffered(1)),
                  pl.BlockSpec((tm, d), lambda i: (i, 0)),
                  mod_spec, vec_spec, vec_spec, mod_spec, mod_spec] + cast_in,
        out_specs=[pl.BlockSpec((tm, d), lambda i: (i, 0)),
                   pl.BlockSpec((tm, d), lambda i: (i, 0))] + cast_out,
        out_shape=[jax.ShapeDtypeStruct((m, d), F32),
                   jax.ShapeDtypeStruct((m, d), BF16)] + cast_shapes,
        compiler_params=_params(("arbitrary",)),
        name="outproj",
    )(y_a, y_b, w_out, w_out, x2d, gate, post_g, ffn_g, scale_f, shift_f, *cast_weights)


def _conv_gelu(a, cw, cb):
    hw = GRID_W
    ext = a.shape[0]
    tm = ext - 2 * hw
    rows = [a[dr * hw:dr * hw + tm, :] for dr in range(3)]
    taps = [sum(cw[3 * dr + dc:3 * dr + dc + 1, :] * rows[dr] for dr in range(3)) for dc in range(3)]
    col = lax.broadcasted_iota(jnp.int32, (tm, 1), 0) % hw
    left = jnp.where(col > 0, pltpu.roll(taps[0], 1, 0), 0.0)
    right = jnp.where(col < hw - 1, pltpu.roll(taps[2], tm - 1, 0), 0.0)
    return _gelu(cb + taps[1] + left + right)


def _ffn_up_kernel(f_ref, fp_ref, fn_ref, wa_ref, wv_ref, cw_ref, cb_ref, *rest, tiles_per_image):
    n_cast = (len(rest) - 2) // 2
    g_ref, fext_scr = rest[n_cast], rest[-1]
    _cast_slabs(rest[:n_cast], rest[n_cast + 1:-1])
    i = pl.program_id(0)
    j = pl.program_id(1)
    tm = f_ref.shape[0]
    hw = GRID_W

    @pl.when(j == 0)
    def _():
        top = (i % tiles_per_image) == 0
        bottom = (i % tiles_per_image) == tiles_per_image - 1
        fext_scr[pl.ds(0, hw), :] = jnp.where(top, jnp.zeros_like(fp_ref), fp_ref[...])
        fext_scr[pl.ds(hw, tm), :] = f_ref[...]
        fext_scr[pl.ds(hw + tm, hw), :] = jnp.where(bottom, jnp.zeros_like(fn_ref), fn_ref[...])

    a = _dot(fext_scr[...], wa_ref[...])
    val = _dot(f_ref[...], wv_ref[...])
    g_ref[...] = (_conv_gelu(a, cw_ref[...], cb_ref[...]) * val).astype(g_ref.dtype)


def _ffn_up(f2d, w_up, conv_w, conv_b, cast_weights, rows_per_image, tm=1024, tf=512):
    m, d = f2d.shape
    d_ff = w_up.shape[1] // 2
    assert rows_per_image % tm == 0 and tm % GRID_W == 0 and d_ff % tf == 0
    hb = tm // GRID_W
    n_halo = m // GRID_W
    n_j = d_ff // tf
    cast_in, cast_out, cast_shapes = _cast_specs(cast_weights, (m // tm) * n_j, lambda i, j: i * n_j + j)
    return pl.pallas_call(
        functools.partial(_ffn_up_kernel, tiles_per_image=rows_per_image // tm),
        grid=(m // tm, n_j),
        in_specs=[pl.BlockSpec((tm, d), lambda i, j: (i, 0)),
                  pl.BlockSpec((GRID_W, d), lambda i, j: (jnp.maximum(i * hb - 1, 0), 0)),
                  pl.BlockSpec((GRID_W, d), lambda i, j: (jnp.minimum((i + 1) * hb, n_halo - 1), 0)),
                  pl.BlockSpec((d, tf), lambda i, j: (0, j)),
                  pl.BlockSpec((d, tf), lambda i, j: (0, n_j + j)),
                  pl.BlockSpec((9, tf), lambda i, j: (0, j)),
                  pl.BlockSpec((1, tf), lambda i, j: (0, j))] + cast_in,
        out_specs=[pl.BlockSpec((tm, tf), lambda i, j: (i, j))] + cast_out,
        out_shape=[jax.ShapeDtypeStruct((m, d_ff), BF16)] + cast_shapes,
        scratch_shapes=[pltpu.VMEM((tm + 2 * GRID_W, d), BF16)],
        compiler_params=_params(("arbitrary", "arbitrary")),
        name="ffn_up",
    )(f2d, f2d, f2d, w_up, w_up, conv_w, conv_b, *cast_weights)


def _ffn_down_kernel(g_ref, w_ref, h_ref, gt_ref, pg_ref, o_ref):
    gate_gain = gt_ref[0] * pg_ref[...]
    for r0 in range(0, o_ref.shape[0], ROW_SUBTILE):
        rs = pl.ds(r0, ROW_SUBTILE)
        y = _dot(g_ref[rs, :], w_ref[...])
        o_ref[rs, :] = h_ref[rs, :] + _rms(y, gate_gain)


def _ffn_down(g2d, w_down, h2d, gate, post_g, rows_per_mod, tm=512):
    m, d = h2d.shape
    d_ff = g2d.shape[1]
    assert m % tm == 0
    tiles_per_mod = rows_per_mod // tm
    return pl.pallas_call(
        _ffn_down_kernel,
        grid=(m // tm,),
        in_specs=[pl.BlockSpec((tm, d_ff), lambda i: (i, 0)),
                  pl.BlockSpec((d_ff, d), lambda i: (0, 0), pipeline_mode=pl.Buffered(1)),
                  pl.BlockSpec((tm, d), lambda i: (i, 0)),
                  pl.BlockSpec((1, 1, d), lambda i: (i // tiles_per_mod, 0, 0)),
                  pl.BlockSpec((1, d), lambda i: (0, 0))],
        out_specs=pl.BlockSpec((tm, d), lambda i: (i, 0)),
        out_shape=jax.ShapeDtypeStruct((m, d), F32),
        compiler_params=pltpu.CompilerParams(dimension_semantics=("arbitrary",),
                                             vmem_limit_bytes=FFN_DOWN_VMEM_LIMIT),
        name="ffn_down",
    )(g2d, w_down, h2d, gate, post_g)


def _gate_operands(w_f, b_f, w_b, b_b):
    r, kdim = w_f.shape
    wp = jnp.zeros((LANES, GLA_HEADS, 2, GLA_DK), BF16)
    wp = wp.at[:r, :, 0].set(w_f.astype(BF16).reshape(r, GLA_HEADS, GLA_DK))
    wp = wp.at[r:2 * r, :, 1].set(w_b.astype(BF16).reshape(r, GLA_HEADS, GLA_DK))
    bias = jnp.stack([b_f.reshape(GLA_HEADS, GLA_DK), b_b.reshape(GLA_HEADS, GLA_DK)], axis=1)
    return wp.reshape(LANES, 2 * kdim), bias.reshape(1, 2 * kdim)


def kernel(x, c, ctx, c_ctx, ada_w, ada_b, pre_mix_g, post_mix_g, pre_ffn_g, post_ffn_g, w_in, sgu_ln_g, sgu_ln_b, sgu_w, sgu_b, gla_gate_w_f, gla_gate_b_f, gla_gate_w_b, gla_gate_b_b, gla_norm_g, w_out, ffn_w_up, ffn_conv_w, ffn_conv_b, ffn_w_down):
    bsz, l, d = x.shape
    lc = ctx.shape[1]
    assert ada_w.shape[0] == 1, "single-layer kernel"
    d_ff = ffn_w_down.shape[1]

    assert bsz < SUBLANES
    c_rows = jnp.zeros((SUBLANES, d), F32).at[:bsz].set(c).at[bsz].set(c_ctx)
    mod = _modulation(c_rows, ada_w[0], ada_b)
    sh_m, sc_m, gt_m, sh_f, sc_f, gt_f = [mod[:bsz, t * d:(t + 1) * d].reshape(bsz, 1, d) for t in range(N_MOD)]
    csh_m = mod[bsz:bsz + 1, 0:d].reshape(1, 1, d)
    csc_m = mod[bsz:bsz + 1, d:2 * d].reshape(1, 1, d)

    w_in_t = jnp.swapaxes(w_in[0], 0, 1).astype(BF16)
    w_lr_t = jnp.pad(w_in_t[COL_LR:], ((0, LANES - 2 * GLA_LOWRANK), (0, 0)))

    p_ctx, lr_ctx = _inproj(ctx.reshape(bsz * lc, d), csh_m, csc_m, pre_mix_g, w_in_t, w_lr_t,
                            tm=2 * lc, tn=512, first_col=COL_K, n_cols=COL_R - COL_K, rows_per_mod=bsz * lc)
    x2d = x.reshape(bsz * l, d)
    p_lat, lr_lat, y_a, w_out_bf = _inproj_rows(
        x2d, sh_m, sc_m, pre_mix_g, w_in_t, w_lr_t, sgu_ln_g, sgu_ln_b, sgu_w[0].astype(BF16),
        sgu_b[0][:, :, None], cast_weights=(w_out[0],), tm=512, rows_per_mod=l)

    y_b = _gla(p_lat.reshape(bsz, l, P_COLS), lr_lat.reshape(bsz, l, LANES),
               p_ctx.reshape(bsz, lc, -1), lr_ctx.reshape(bsz, lc, LANES),
               *_gate_operands(gla_gate_w_f[0], gla_gate_b_f[0], gla_gate_w_b[0], gla_gate_b_b[0]), gla_norm_g)
    h, f, w_up_bf = _outproj(y_a, y_b.reshape(bsz * l, MIX_B), w_out_bf, x2d, gt_m,
                             post_mix_g, pre_ffn_g, sc_f, sh_f, cast_weights=(ffn_w_up[0],), rows_per_mod=l)
    g, w_down_bf = _ffn_up(f, w_up_bf, ffn_conv_w[0].reshape(9, d_ff), ffn_conv_b,
                           cast_weights=(ffn_w_down[0],), rows_per_image=l)
    out = _ffn_down(g, w_down_bf, h, gt_f, post_ffn_g, rows_per_mod=l)
    return out.reshape(bsz, l, d)
```

```python
import functools

import jax
import jax.numpy as jnp
from jax import lax
from jax.experimental import pallas as pl
from jax.experimental.pallas import tpu as pltpu

F32 = jnp.float32
BF16 = jnp.bfloat16

EPS = 1e-6
GRID_W = 64
MIX_A = 1024
MIX_B = 1024
SGU_GROUPS = 4
SGU_CHUNK = 128
SGU_GW = MIX_A // SGU_GROUPS
GLA_HEADS = 4
GLA_DK = 128
GLA_DV = 256
GLA_KDIM = GLA_HEADS * GLA_DK
GLA_LOWRANK = 16
GLA_NORMALIZER = 16.0
GLA_CHUNK = 64
N_MOD = 6
LANES = 128
SUBLANES = 8
BF16_TILE_ROWS = 16
ROW_SUBTILE = 256
VMEM_LIMIT = 56 * 1024 * 1024
FFN_DOWN_VMEM_LIMIT = 60 * 1024 * 1024

COL_U = 0
COL_VS = COL_U + MIX_A
COL_Q = COL_VS + MIX_A
COL_K = COL_Q + GLA_KDIM
COL_V = COL_K + GLA_KDIM
COL_R = COL_V + MIX_B
COL_LR = COL_R + MIX_B
N_MAIN = COL_LR
P_Q = 0
P_K = P_Q + GLA_DK
P_V = P_K + GLA_DK
P_R = P_V + GLA_DV
P_HEAD = P_R + GLA_DV
P_COLS = GLA_HEADS * P_HEAD


def _dot(a, b):
    return jnp.dot(a, b, preferred_element_type=F32)


def _nt_dot(a, b):
    return lax.dot_general(a, b, (((1,), (1,)), ((), ())), preferred_element_type=F32)


def _silu(x):
    return x / (1.0 + jnp.exp(-x))


_GELU_A = -2.0 * 0.7978845608028654 * 1.4426950408889634
_GELU_B = _GELU_A * 0.044715


def _gelu(x):
    return x / (1.0 + jnp.exp2((_GELU_A + _GELU_B * (x * x)) * x))


def _params(sem):
    return pltpu.CompilerParams(dimension_semantics=sem, vmem_limit_bytes=VMEM_LIMIT)


def _mod_kernel(c_ref, w_ref, b_ref, o_ref):
    s = _silu(c_ref[...]).astype(BF16)
    o_ref[...] = _dot(s, w_ref[...].astype(BF16)) + b_ref[...]


def _modulation(c_rows, ada_w, ada_b, tn=1024):
    rows, d = c_rows.shape
    n = ada_w.shape[1]
    return pl.pallas_call(
        _mod_kernel,
        grid=(n // tn,),
        in_specs=[pl.BlockSpec((rows, d), lambda j: (0, 0)),
                  pl.BlockSpec((d, tn), lambda j: (0, j)),
                  pl.BlockSpec((1, tn), lambda j: (0, j))],
        out_specs=pl.BlockSpec((rows, tn), lambda j: (0, j)),
        out_shape=jax.ShapeDtypeStruct((rows, n), F32),
        compiler_params=_params(("arbitrary",)),
        name="mod",
    )(c_rows, ada_w, ada_b)


def _norm_modulate(x, g_ref, sc_ref, sh_ref):
    gain = g_ref[...] * (1.0 + sc_ref[0])
    return (x * lax.rsqrt(jnp.mean(x * x, axis=-1, keepdims=True) + EPS) * gain + sh_ref[0]).astype(BF16)


def _inproj_kernel(x_ref, sh_ref, sc_ref, g_ref, w_ref, wlr_ref, p_ref, lr_ref, a_scr):
    @pl.when(pl.program_id(1) == 0)
    def _():
        a = _norm_modulate(x_ref[...], g_ref, sc_ref, sh_ref)
        a_scr[...] = a
        lr_ref[...] = _nt_dot(a, wlr_ref[...])

    p_ref[...] = _nt_dot(a_scr[...], w_ref[...]).astype(p_ref.dtype)


def _inproj(x2d, shift, scale, gain, w_t, w_lr_t, tm, tn, first_col, n_cols, rows_per_mod):
    m, d = x2d.shape
    n = n_cols
    assert m % tm == 0 and first_col % tn == 0 and n_cols % tn == 0 and rows_per_mod % tm == 0
    tiles_per_mod = rows_per_mod // tm
    blk0 = first_col // tn
    return pl.pallas_call(
        _inproj_kernel,
        grid=(m // tm, n_cols // tn),
        in_specs=[pl.BlockSpec((tm, d), lambda i, j: (i, 0)),
                  pl.BlockSpec((1, 1, d), lambda i, j: (i // tiles_per_mod, 0, 0)),
                  pl.BlockSpec((1, 1, d), lambda i, j: (i // tiles_per_mod, 0, 0)),
                  pl.BlockSpec((1, d), lambda i, j: (0, 0)),
                  pl.BlockSpec((tn, d), lambda i, j: (blk0 + j, 0)),
                  pl.BlockSpec((LANES, d), lambda i, j: (0, 0))],
        out_specs=[pl.BlockSpec((tm, tn), lambda i, j: (i, j)),
                   pl.BlockSpec((tm, LANES), lambda i, j: (i, 0))],
        out_shape=[jax.ShapeDtypeStruct((m, n), BF16),
                   jax.ShapeDtypeStruct((m, LANES), F32)],
        scratch_shapes=[pltpu.VMEM((tm, d), BF16)],
        compiler_params=_params(("parallel", "arbitrary")),
        name="inproj",
    )(x2d, shift, scale, gain, w_t, w_lr_t)


def _spatial_gating(u, v, lg_ref, lb_ref, ws_ref, bs_ref, ya_ref):
    for r0 in range(0, u.shape[0], SGU_CHUNK):
        for g in range(SGU_GROUPS):
            c0 = g * SGU_GW
            vg = v[r0:r0 + SGU_CHUNK, c0:c0 + SGU_GW]
            vc = vg - jnp.mean(vg, axis=-1, keepdims=True)
            var = jnp.mean(vc * vc, axis=-1, keepdims=True)
            cs = pl.ds(c0, SGU_GW)
            vn = vc * lax.rsqrt(var + EPS) * lg_ref[:, cs] + lb_ref[:, cs]
            s = _dot(ws_ref[g], vn.astype(BF16)) + bs_ref[g]
            ya_ref[pl.ds(r0, SGU_CHUNK), cs] = (u[r0:r0 + SGU_CHUNK, c0:c0 + SGU_GW] * s).astype(ya_ref.dtype)


def _inproj_rows_kernel(x_ref, sh_ref, sc_ref, g_ref, w_ref, wlr_ref, lg_ref, lb_ref, ws_ref, bs_ref, *rest):
    n_cast = (len(rest) - 3) // 2
    p_ref, lr_ref, ya_ref = rest[n_cast:n_cast + 3]
    _cast_slabs(rest[:n_cast], rest[n_cast + 3:])
    a = _norm_modulate(x_ref[...], g_ref, sc_ref, sh_ref)
    lr_ref[...] = _nt_dot(a, wlr_ref[...])
    u = _gelu(_nt_dot(a, w_ref[pl.ds(COL_U, MIX_A), :]))
    v = _gelu(_nt_dot(a, w_ref[pl.ds(COL_VS, MIX_A), :]))
    _spatial_gating(u, v, lg_ref, lb_ref, ws_ref, bs_ref, ya_ref)
    qk = _nt_dot(a, w_ref[pl.ds(COL_Q, 2 * GLA_KDIM), :]).astype(p_ref.dtype)
    vv = _nt_dot(a, w_ref[pl.ds(COL_V, MIX_B), :]).astype(p_ref.dtype)
    rr = _silu(_nt_dot(a, w_ref[pl.ds(COL_R, MIX_B), :])).astype(p_ref.dtype)
    for h in range(GLA_HEADS):
        base = h * P_HEAD
        p_ref[:, pl.ds(base + P_Q, GLA_DK)] = qk[:, h * GLA_DK:(h + 1) * GLA_DK]
        p_ref[:, pl.ds(base + P_K, GLA_DK)] = qk[:, GLA_KDIM + h * GLA_DK:GLA_KDIM + (h + 1) * GLA_DK]
        p_ref[:, pl.ds(base + P_V, GLA_DV)] = vv[:, h * GLA_DV:(h + 1) * GLA_DV]
        p_ref[:, pl.ds(base + P_R, GLA_DV)] = rr[:, h * GLA_DV:(h + 1) * GLA_DV]


def _inproj_rows(x2d, shift, scale, gain, w_t, w_lr_t, ln_g, ln_b, w_s, b_s, cast_weights, tm, rows_per_mod):
    m, d = x2d.shape
    assert m % tm == 0 and rows_per_mod % tm == 0 and tm % SGU_CHUNK == 0
    tiles_per_mod = rows_per_mod // tm
    cast_in, cast_out, cast_shapes = _cast_specs(cast_weights, m // tm)
    whole = lambda shape: pl.BlockSpec(shape, lambda i: (0,) * len(shape))
    return pl.pallas_call(
        _inproj_rows_kernel,
        grid=(m // tm,),
        in_specs=[pl.BlockSpec((tm, d), lambda i: (i, 0)),
                  pl.BlockSpec((1, 1, d), lambda i: (i // tiles_per_mod, 0, 0)),
                  pl.BlockSpec((1, 1, d), lambda i: (i // tiles_per_mod, 0, 0)),
                  whole((1, d)),
                  pl.BlockSpec((N_MAIN, d), lambda i: (0, 0), pipeline_mode=pl.Buffered(1)),
                  pl.BlockSpec((LANES, d), lambda i: (0, 0), pipeline_mode=pl.Buffered(1)),
                  whole((1, MIX_A)), whole((1, MIX_A)),
                  whole((SGU_GROUPS, SGU_CHUNK, SGU_CHUNK)), whole((SGU_GROUPS, SGU_CHUNK, 1))] + cast_in,
        out_specs=[pl.BlockSpec((tm, P_COLS), lambda i: (i, 0)),
                   pl.BlockSpec((tm, LANES), lambda i: (i, 0)),
                   pl.BlockSpec((tm, MIX_A), lambda i: (i, 0))] + cast_out,
        out_shape=[jax.ShapeDtypeStruct((m, P_COLS), BF16),
                   jax.ShapeDtypeStruct((m, LANES), F32),
                   jax.ShapeDtypeStruct((m, MIX_A), BF16)] + cast_shapes,
        compiler_params=_params(("arbitrary",)),
        name="inproj_rows",
    )(x2d, shift, scale, gain, w_t, w_lr_t, ln_g, ln_b, w_s, b_s, *cast_weights)


GLA_BLOCK = 4 * GLA_CHUNK
GLA_PAIR = 2 * GLA_CHUNK
GLA_GATE_BLOCKS = 2
GLA_UNROLL = 8


LOG2E = 1.4426950408889634


def _log2_decay(z):
    soft = jnp.log2(1.0 + jnp.exp2(jnp.abs(z) * (-LOG2E)))
    return jnp.minimum(z, 0.0) * (LOG2E / GLA_NORMALIZER) - soft * (1.0 / GLA_NORMALIZER)


def _split(x):
    hi = x.astype(BF16)
    return hi, (x - hi.astype(F32)).astype(BF16)


def _gla_masks():
    n, c = GLA_BLOCK, GLA_CHUNK
    r = jnp.arange(n)[:, None]
    s = jnp.arange(n)[None, :]
    same = (r // c) == (s // c)
    return ((same & (s <= r)).astype(BF16), (same & (s >= r)).astype(BF16), jnp.eye(GLA_DV, dtype=BF16))


def _chunk_rows(blk, cidx, rows):
    return pl.ds(pl.multiple_of((blk * (GLA_BLOCK // GLA_CHUNK) + cidx) * rows, rows), rows)


def _gla_gates(rs, blk, q_ref, k_ref, v_ref, lr_ref, w_ref, b_ref, lower_ref, eye_ref, vt_scr, fwd, bwd):
    c, dk = GLA_CHUNK, GLA_DK
    vt_scr[:, rs] = _nt_dot(eye_ref[...], v_ref[rs, :]).astype(BF16)
    g = _log2_decay(_dot(lr_ref[rs, :].astype(BF16), w_ref[...]) + b_ref[...])
    g_hi, g_lo = _split(g)
    nb = g.shape[0]
    p = jnp.concatenate([_dot(lower_ref[...], g_hi[r0:r0 + GLA_BLOCK]) + _dot(lower_ref[...], g_lo[r0:r0 + GLA_BLOCK])
                         for r0 in range(0, nb, GLA_BLOCK)], axis=0)
    last = [p[i * c + c - 1:(i + 1) * c, :] for i in range(nb // c)]
    tot = jnp.concatenate([jnp.broadcast_to(t, (c, 2 * dk)) for t in last], axis=0)
    k32 = k_ref[rs, :].astype(F32)
    q32 = None if q_ref is None else q_ref[rs, :].astype(F32) * (GLA_DK ** -0.5)
    plans = ((fwd, slice(0, dk), p, tot - p), (bwd, slice(dk, 2 * dk), tot - p + g, p - g))
    for d, lanes, b_cum, to_end in plans:
        d["kd"][rs, :] = (k32 * jnp.exp2(to_end[:, lanes])).astype(BF16)
        for cidx in range(nb // c):
            dec = jnp.exp2(last[cidx][:, lanes])
            d["dec"][_chunk_rows(blk, cidx, SUBLANES), :] = jnp.broadcast_to(dec, (SUBLANES, dk))
        if q32 is not None:
            d["qe"][rs, :] = (q32 * jnp.exp2(b_cum[:, lanes])).astype(BF16)
            d["ke"][rs, :] = (k32 * jnp.exp2(-b_cum[:, lanes])).astype(BF16)


def _gla_scan_block(rs, blk, order, d, vt_scr, s_scr, record):
    c = GLA_CHUNK
    v_t = vt_scr[:, rs]
    kd = d["kd"][rs, :]
    zeros = jnp.zeros((c, kd.shape[1]), kd.dtype)
    s = s_scr[...]
    for cidx in order:
        pair, half = divmod(cidx, 2)
        slab = v_t[:, pair * GLA_PAIR:(pair + 1) * GLA_PAIR]
        kc = kd[cidx * c:(cidx + 1) * c]
        k_only = jnp.concatenate([kc, zeros] if half == 0 else [zeros, kc], axis=0)
        kv_t = _dot(slab, k_only)
        if record:
            d["snap"][_chunk_rows(blk, cidx, GLA_DV), :] = s.astype(BF16)
        dec = d["dec"][_chunk_rows(blk, cidx, SUBLANES), :]
        s = s * dec[0:1, :] + kv_t
    s_scr[...] = s


def _gla_output(rs, blk, v_ref, r_ref, qe_scr, snap_scr, fwd, bwd, ng_ref, y_ref):
    c, dk = GLA_CHUNK, GLA_DK
    qe = qe_scr[rs, :]
    s_f = _nt_dot(qe[:, :dk], fwd["ke"][rs, :])
    s_b = _nt_dot(qe[:, dk:], bwd["ke"][rs, :])
    att = jnp.where(fwd["tri"][...] > 0, s_f, 0.0) + jnp.where(bwd["tri"][...] > 0, s_b, 0.0)
    inter = [_nt_dot(qe[cidx * c:(cidx + 1) * c], snap_scr[_chunk_rows(blk, cidx, GLA_DV), :])
             for cidx in range(GLA_BLOCK // c)]
    o = _dot(att.astype(BF16), v_ref[rs, :]) + jnp.concatenate(inter, axis=0)
    o = o * lax.rsqrt(jnp.mean(o * o, axis=-1, keepdims=True) + EPS) * ng_ref[...]
    y_ref[rs, :] = (o * r_ref[rs, :].astype(F32)).astype(y_ref.dtype)


def _gla_kernel(p_ref, lr_ref, kc_ref, vc_ref, lrc_ref, w_ref, b_ref, ng_ref,
                lower_ref, upper_ref, eye_ref,
                y_ref, sf_scr, sb_scr, vt_scr, qe_scr, kef_scr, keb_scr, kdf_scr, kdb_scr,
                decf_scr, decb_scr, snap_scr):
    lanes_f, lanes_b = pl.ds(0, GLA_DK), pl.ds(GLA_DK, GLA_DK)
    fwd = dict(tri=lower_ref, qe=qe_scr.at[:, lanes_f], ke=kef_scr, kd=kdf_scr, dec=decf_scr,
               snap=snap_scr.at[:, lanes_f])
    bwd = dict(tri=upper_ref, qe=qe_scr.at[:, lanes_b], ke=keb_scr, kd=kdb_scr, dec=decb_scr,
               snap=snap_scr.at[:, lanes_b])
    q_ref, k_ref = p_ref.at[:, pl.ds(P_Q, GLA_DK)], p_ref.at[:, pl.ds(P_K, GLA_DK)]
    v_ref, r_ref = p_ref.at[:, pl.ds(P_V, GLA_DV)], p_ref.at[:, pl.ds(P_R, GLA_DV)]
    l, lc = p_ref.shape[0], kc_ref.shape[0]
    n_blk, n_cblk = l // GLA_BLOCK, lc // GLA_BLOCK
    asc = tuple(range(GLA_BLOCK // GLA_CHUNK))
    desc = asc[::-1]

    def block_rows(j):
        return pl.ds(pl.multiple_of(j * GLA_BLOCK, GLA_BLOCK), GLA_BLOCK)

    sf_scr[...] = jnp.zeros_like(sf_scr)
    sb_scr[...] = jnp.zeros_like(sb_scr)

    for j in range(n_cblk):
        _gla_gates(block_rows(j), j, None, kc_ref, vc_ref, lrc_ref, w_ref, b_ref, lower_ref, eye_ref,
                   vt_scr, fwd, bwd)
    for j in range(n_cblk):
        _gla_scan_block(block_rows(j), j, asc, fwd, vt_scr, sf_scr, False)
        jb = n_cblk - 1 - j
        _gla_scan_block(block_rows(jb), jb, desc, bwd, vt_scr, sb_scr, False)

    def gates(j, carry):
        rows = GLA_GATE_BLOCKS * GLA_BLOCK
        rs = pl.ds(pl.multiple_of(j * rows, rows), rows)
        _gla_gates(rs, j * GLA_GATE_BLOCKS, q_ref, k_ref, v_ref, lr_ref, w_ref, b_ref, lower_ref, eye_ref,
                   vt_scr, fwd, bwd)
        return carry

    def scan(j, carry):
        _gla_scan_block(block_rows(j), j, asc, fwd, vt_scr, sf_scr, True)
        jb = n_blk - 1 - j
        _gla_scan_block(block_rows(jb), jb, desc, bwd, vt_scr, sb_scr, True)
        return carry

    def output(j, carry):
        _gla_output(block_rows(j), j, v_ref, r_ref, qe_scr, snap_scr, fwd, bwd, ng_ref, y_ref)
        return carry

    lax.fori_loop(0, n_blk // GLA_GATE_BLOCKS, gates, 0)
    lax.fori_loop(0, n_blk, scan, 0, unroll=GLA_UNROLL)
    lax.fori_loop(0, n_blk, output, 0, unroll=GLA_UNROLL)


def _gla(p_lat, lr_lat, p_ctx, lr_ctx, gate_w, gate_b, norm_g):
    bsz, l, _ = p_lat.shape
    lc = p_ctx.shape[1]
    assert l % GLA_BLOCK == 0 and lc % GLA_BLOCK == 0 and lc <= l
    gate_specs = [pl.BlockSpec((LANES, 2 * GLA_DK), lambda b, h: (0, h)),
                  pl.BlockSpec((1, 2 * GLA_DK), lambda b, h: (0, h))]
    masks = _gla_masks()
    mask_specs = [pl.BlockSpec(m.shape, lambda b, h: (0, 0)) for m in masks]
    n_chunks = l // GLA_CHUNK
    return pl.pallas_call(
        _gla_kernel,
        grid=(bsz, GLA_HEADS),
        in_specs=[pl.BlockSpec((None, l, P_HEAD), lambda b, h: (b, 0, h)),
                  pl.BlockSpec((None, l, LANES), lambda b, h: (b, 0, 0)),
                  pl.BlockSpec((None, lc, GLA_DK), lambda b, h: (b, 0, h)),
                  pl.BlockSpec((None, lc, GLA_DV), lambda b, h: (b, 0, GLA_KDIM // GLA_DV + h)),
                  pl.BlockSpec((None, lc, LANES), lambda b, h: (b, 0, 0))]
                 + gate_specs
                 + [pl.BlockSpec((1, GLA_DV), lambda b, h: (0, 0))] + mask_specs,
        out_specs=pl.BlockSpec((None, l, GLA_DV), lambda b, h: (b, 0, h)),
        out_shape=jax.ShapeDtypeStruct((bsz, l, MIX_B), BF16),
        scratch_shapes=[pltpu.VMEM((GLA_DV, GLA_DK), F32)] * 2
                       + [pltpu.VMEM((GLA_DV, l), BF16)]
                       + [pltpu.VMEM((l, 2 * GLA_DK), BF16)]
                       + [pltpu.VMEM((l, GLA_DK), BF16)] * 4
                       + [pltpu.VMEM((n_chunks * SUBLANES, GLA_DK), F32)] * 2
                       + [pltpu.VMEM((n_chunks * GLA_DV, 2 * GLA_DK), BF16)],
        compiler_params=_params(("parallel", "arbitrary")),
        name="gla",
    )(p_lat, lr_lat, p_ctx, p_ctx, lr_ctx, gate_w, gate_b, norm_g, *masks)


def _cast_specs(weights, n_steps, step_index=lambda i: i):
    ins, outs, shapes = [], [], []
    for w in weights:
        rows, cols = w.shape
        assert rows % (n_steps * BF16_TILE_ROWS) == 0, "slabs must be whole packed bf16 row tiles"
        spec = pl.BlockSpec((rows // n_steps, cols), lambda *idx: (step_index(*idx), 0))
        ins.append(spec)
        outs.append(spec)
        shapes.append(jax.ShapeDtypeStruct(w.shape, BF16))
    return ins, outs, shapes


def _cast_slabs(src_refs, dst_refs):
    for src, dst in zip(src_refs, dst_refs):
        dst[...] = src[...].astype(dst.dtype)


def _rms(x, g):
    return x * lax.rsqrt(jnp.mean(x * x, axis=-1, keepdims=True) + EPS) * g


def _outproj_kernel(ya_ref, yb_ref, wa_ref, wb_ref, x_ref, gt_ref, pg_ref, fg_ref, sc_ref, sh_ref, *rest):
    n_cast = (len(rest) - 2) // 2
    h_ref, f_ref = rest[n_cast:n_cast + 2]
    _cast_slabs(rest[:n_cast], rest[n_cast + 2:])
    gate_gain = gt_ref[0] * pg_ref[...]
    mod_gain = fg_ref[...] * (1.0 + sc_ref[0])
    for r0 in range(0, h_ref.shape[0], ROW_SUBTILE):
        rs = pl.ds(r0, ROW_SUBTILE)
        y = _dot(ya_ref[rs, :], wa_ref[...]) + _dot(yb_ref[rs, :], wb_ref[...])
        h = x_ref[rs, :] + _rms(y, gate_gain)
        h_ref[rs, :] = h
        f_ref[rs, :] = (_rms(h, mod_gain) + sh_ref[0]).astype(f_ref.dtype)


def _outproj(y_a, y_b, w_out, x2d, gate, post_g, ffn_g, scale_f, shift_f, cast_weights, rows_per_mod, tm=512):
    m, d = x2d.shape
    tiles_per_mod = rows_per_mod // tm
    mod_spec = pl.BlockSpec((1, 1, d), lambda i: (i // tiles_per_mod, 0, 0))
    vec_spec = pl.BlockSpec((1, d), lambda i: (0, 0))
    cast_in, cast_out, cast_shapes = _cast_specs(cast_weights, m // tm)
    return pl.pallas_call(
        _outproj_kernel,
        grid=(m // tm,),
        in_specs=[pl.BlockSpec((tm, MIX_A), lambda i: (i, 0)),
                  pl.BlockSpec((tm, MIX_B), lambda i: (i, 0)),
                  pl.BlockSpec((MIX_A, d), lambda i: (0, 0), pipeline_mode=pl.Buffered(1)),
                  pl.BlockSpec((MIX_B, d), lambda i: (MIX_A // MIX_B, 0), pipeline_mode=pl.Buffered(1)),
                  pl.BlockSpec((tm, d), lambda i: (i, 0)),
                  mod_spec, vec_spec, vec_spec, mod_spec, mod_spec] + cast_in,
        out_specs=[pl.BlockSpec((tm, d), lambda i: (i, 0)),
                   pl.BlockSpec((tm, d), lambda i: (i, 0))] + cast_out,
        out_shape=[jax.ShapeDtypeStruct((m, d), F32),
                   jax.ShapeDtypeStruct((m, d), BF16)] + cast_shapes,
        compiler_params=_params(("arbitrary",)),
        name="outproj",
    )(y_a, y_b, w_out, w_out, x2d, gate, post_g, ffn_g, scale_f, shift_f, *cast_weights)


def _conv_gelu(a, cw, cb):
    hw = GRID_W
    ext = a.shape[0]
    tm = ext - 2 * hw
    rows = [a[dr * hw:dr * hw + tm, :] for dr in range(3)]
    taps = [sum(cw[3 * dr + dc:3 * dr + dc + 1, :] * rows[dr] for dr in range(3)) for dc in range(3)]
    col = lax.broadcasted_iota(jnp.int32, (tm, 1), 0) % hw
    left = jnp.where(col > 0, pltpu.roll(taps[0], 1, 0), 0.0)
    right = jnp.where(col < hw - 1, pltpu.roll(taps[2], tm - 1, 0), 0.0)
    return _gelu(cb + taps[1] + left + right)


def _ffn_up_kernel(f_ref, fp_ref, fn_ref, wa_ref, wv_ref, cw_ref, cb_ref, *rest, tiles_per_image):
    n_cast = (len(rest) - 2) // 2
    g_ref, fext_scr = rest[n_cast], rest[-1]
    _cast_slabs(rest[:n_cast], rest[n_cast + 1:-1])
    i = pl.program_id(0)
    j = pl.program_id(1)
    tm = f_ref.shape[0]
    hw = GRID_W

    @pl.when(j == 0)
    def _():
        top = (i % tiles_per_image) == 0
        bottom = (i % tiles_per_image) == tiles_per_image - 1
        fext_scr[pl.ds(0, hw), :] = jnp.where(top, jnp.zeros_like(fp_ref), fp_ref[...])
        fext_scr[pl.ds(hw, tm), :] = f_ref[...]
        fext_scr[pl.ds(hw + tm, hw), :] = jnp.where(bottom, jnp.zeros_like(fn_ref), fn_ref[...])

    a = _dot(fext_scr[...], wa_ref[...])
    val = _dot(f_ref[...], wv_ref[...])
    g_ref[...] = (_conv_gelu(a, cw_ref[...], cb_ref[...]) * val).astype(g_ref.dtype)


def _ffn_up(f2d, w_up, conv_w, conv_b, cast_weights, rows_per_image, tm=1024, tf=512):
    m, d = f2d.shape
    d_ff = w_up.shape[1] // 2
    assert rows_per_image % tm == 0 and tm % GRID_W == 0 and d_ff % tf == 0
    hb = tm // GRID_W
    n_halo = m // GRID_W
    n_j = d_ff // tf
    cast_in, cast_out, cast_shapes = _cast_specs(cast_weights, (m // tm) * n_j, lambda i, j: i * n_j + j)
    return pl.pallas_call(
        functools.partial(_ffn_up_kernel, tiles_per_image=rows_per_image // tm),
        grid=(m // tm, n_j),
        in_specs=[pl.BlockSpec((tm, d), lambda i, j: (i, 0)),
                  pl.BlockSpec((GRID_W, d), lambda i, j: (jnp.maximum(i * hb - 1, 0), 0)),
                  pl.BlockSpec((GRID_W, d), lambda i, j: (jnp.minimum((i + 1) * hb, n_halo - 1), 0)),
                  pl.BlockSpec((d, tf), lambda i, j: (0, j)),
                  pl.BlockSpec((d, tf), lambda i, j: (0, n_j + j)),
                  pl.BlockSpec((9, tf), lambda i, j: (0, j)),
                  pl.BlockSpec((1, tf), lambda i, j: (0, j))] + cast_in,
        out_specs=[pl.BlockSpec((tm, tf), lambda i, j: (i, j))] + cast_out,
        out_shape=[jax.ShapeDtypeStruct((m, d_ff), BF16)] + cast_shapes,
        scratch_shapes=[pltpu.VMEM((tm + 2 * GRID_W, d), BF16)],
        compiler_params=_params(("arbitrary", "arbitrary")),
        name="ffn_up",
    )(f2d, f2d, f2d, w_up, w_up, conv_w, conv_b, *cast_weights)


def _ffn_down_kernel(g_ref, w_ref, h_ref, gt_ref, pg_ref, o_ref):
    gate_gain = gt_ref[0] * pg_ref[...]
    for r0 in range(0, o_ref.shape[0], ROW_SUBTILE):
        rs = pl.ds(r0, ROW_SUBTILE)
        y = _dot(g_ref[rs, :], w_ref[...])
        o_ref[rs, :] = h_ref[rs, :] + _rms(y, gate_gain)


def _ffn_down(g2d, w_down, h2d, gate, post_g, rows_per_mod, tm=512):
    m, d = h2d.shape
    d_ff = g2d.shape[1]
    assert m % tm == 0
    tiles_per_mod = rows_per_mod // tm
    return pl.pallas_call(
        _ffn_down_kernel,
        grid=(m // tm,),
        in_specs=[pl.BlockSpec((tm, d_ff), lambda i: (i, 0)),
                  pl.BlockSpec((d_ff, d), lambda i: (0, 0), pipeline_mode=pl.Buffered(1)),
                  pl.BlockSpec((tm, d), lambda i: (i, 0)),
                  pl.BlockSpec((1, 1, d), lambda i: (i // tiles_per_mod, 0, 0)),
                  pl.BlockSpec((1, d), lambda i: (0, 0))],
        out_specs=pl.BlockSpec((tm, d), lambda i: (i, 0)),
        out_shape=jax.ShapeDtypeStruct((m, d), F32),
        compiler_params=pltpu.CompilerParams(dimension_semantics=("arbitrary",),
                                             vmem_limit_bytes=FFN_DOWN_VMEM_LIMIT),
        name="ffn_down",
    )(g2d, w_down, h2d, gate, post_g)


def _gate_operands(w_f, b_f, w_b, b_b):
    r, kdim = w_f.shape
    wp = jnp.zeros((LANES, GLA_HEADS, 2, GLA_DK), BF16)
    wp = wp.at[:r, :, 0].set(w_f.astype(BF16).reshape(r, GLA_HEADS, GLA_DK))
    wp = wp.at[r:2 * r, :, 1].set(w_b.astype(BF16).reshape(r, GLA_HEADS, GLA_DK))
    bias = jnp.stack([b_f.reshape(GLA_HEADS, GLA_DK), b_b.reshape(GLA_HEADS, GLA_DK)], axis=1)
    return wp.reshape(LANES, 2 * kdim), bias.reshape(1, 2 * kdim)


def kernel(x, c, ctx, c_ctx, ada_w, ada_b, pre_mix_g, post_mix_g, pre_ffn_g, post_ffn_g, w_in, sgu_ln_g, sgu_ln_b, sgu_w, sgu_b, gla_gate_w_f, gla_gate_b_f, gla_gate_w_b, gla_gate_b_b, gla_norm_g, w_out, ffn_w_up, ffn_conv_w, ffn_conv_b, ffn_w_down):
    bsz, l, d = x.shape
    lc = ctx.shape[1]
    assert ada_w.shape[0] == 1, "single-layer kernel"
    d_ff = ffn_w_down.shape[1]

    assert bsz < SUBLANES
    c_rows = jnp.zeros((SUBLANES, d), F32).at[:bsz].set(c).at[bsz].set(c_ctx)
    mod = _modulation(c_rows, ada_w[0], ada_b)
    sh_m, sc_m, gt_m, sh_f, sc_f, gt_f = [mod[:bsz, t * d:(t + 1) * d].reshape(bsz, 1, d) for t in range(N_MOD)]
    csh_m = mod[bsz:bsz + 1, 0:d].reshape(1, 1, d)
    csc_m = mod[bsz:bsz + 1, d:2 * d].reshape(1, 1, d)

    w_in_t = jnp.swapaxes(w_in[0], 0, 1).astype(BF16)
    w_lr_t = jnp.pad(w_in_t[COL_LR:], ((0, LANES - 2 * GLA_LOWRANK), (0, 0)))

    p_ctx, lr_ctx = _inproj(ctx.reshape(bsz * lc, d), csh_m, csc_m, pre_mix_g, w_in_t, w_lr_t,
                            tm=2 * lc, tn=512, first_col=COL_K, n_cols=COL_R - COL_K, rows_per_mod=bsz * lc)
    x2d = x.reshape(bsz * l, d)
    p_lat, lr_lat, y_a, w_out_bf = _inproj_rows(
        x2d, sh_m, sc_m, pre_mix_g, w_in_t, w_lr_t, sgu_ln_g, sgu_ln_b, sgu_w[0].astype(BF16),
        sgu_b[0][:, :, None], cast_weights=(w_out[0],), tm=512, rows_per_mod=l)

    y_b = _gla(p_lat.reshape(bsz, l, P_COLS), lr_lat.reshape(bsz, l, LANES),
               p_ctx.reshape(bsz, lc, -1), lr_ctx.reshape(bsz, lc, LANES),
               *_gate_operands(gla_gate_w_f[0], gla_gate_b_f[0], gla_gate_w_b[0], gla_gate_b_b[0]), gla_norm_g)
    h, f, w_up_bf = _outproj(y_a, y_b.reshape(bsz * l, MIX_B), w_out_bf, x2d, gt_m,
                             post_mix_g, pre_ffn_g, sc_f, sh_f, cast_weights=(ffn_w_up[0],), rows_per_mod=l)
    g, w_down_bf = _ffn_up(f, w_up_bf, ffn_conv_w[0].reshape(9, d_ff), ffn_conv_b,
                           cast_weights=(ffn_w_down[0],), rows_per_image=l)
    out = _ffn_down(g, w_down_bf, h, gt_f, post_ffn_g, rows_per_mod=l)
    return out.reshape(bsz, l, d)
```

```python
import functools

import jax
import jax.numpy as jnp
from jax import lax
from jax.experimental import pallas as pl
from jax.experimental.pallas import tpu as pltpu

F32 = jnp.float32
BF16 = jnp.bfloat16

EPS = 1e-6
GRID_W = 64
MIX_A = 1024
MIX_B = 1024
SGU_GROUPS = 4
SGU_CHUNK = 128
SGU_GW = MIX_A // SGU_GROUPS
GLA_HEADS = 4
GLA_DK = 128
GLA_DV = 256
GLA_KDIM = GLA_HEADS * GLA_DK
GLA_LOWRANK = 16
GLA_NORMALIZER = 16.0
GLA_CHUNK = 64
N_MOD = 6
LANES = 128
SUBLANES = 8
BF16_TILE_ROWS = 16
ROW_SUBTILE = 256
VMEM_LIMIT = 56 * 1024 * 1024
FFN_DOWN_VMEM_LIMIT = 60 * 1024 * 1024

COL_U = 0
COL_VS = COL_U + MIX_A
COL_Q = COL_VS + MIX_A
COL_K = COL_Q + GLA_KDIM
COL_V = COL_K + GLA_KDIM
COL_R = COL_V + MIX_B
COL_LR = COL_R + MIX_B
N_MAIN = COL_LR
P_Q = 0
P_K = P_Q + GLA_DK
P_V = P_K + GLA_DK
P_R = P_V + GLA_DV
P_HEAD = P_R + GLA_DV
P_COLS = GLA_HEADS * P_HEAD


def _dot(a, b):
    return jnp.dot(a, b, preferred_element_type=F32)


def _nt_dot(a, b):
    return lax.dot_general(a, b, (((1,), (1,)), ((), ())), preferred_element_type=F32)


def _silu(x):
    return x / (1.0 + jnp.exp(-x))


_GELU_A = -2.0 * 0.7978845608028654 * 1.4426950408889634
_GELU_B = _GELU_A * 0.044715


def _gelu(x):
    return x / (1.0 + jnp.exp2((_GELU_A + _GELU_B * (x * x)) * x))


def _params(sem):
    return pltpu.CompilerParams(dimension_semantics=sem, vmem_limit_bytes=VMEM_LIMIT)


def _mod_kernel(c_ref, w_ref, b_ref, src_ref, o_ref, dst_ref):
    dst_ref[...] = src_ref[...].astype(dst_ref.dtype)
    s = _silu(c_ref[...]).astype(BF16)
    o_ref[...] = _dot(s, w_ref[...].astype(BF16)) + b_ref[...]


def _modulation(c_rows, ada_w, ada_b, cast_weight, tn=768):
    rows, d = c_rows.shape
    n = ada_w.shape[1]
    steps = n // tn
    r, c = cast_weight.shape
    assert n % tn == 0 and c % (steps * LANES) == 0
    slab = pl.BlockSpec((r, c // steps), lambda j: (0, j))
    return pl.pallas_call(
        _mod_kernel,
        grid=(steps,),
        in_specs=[pl.BlockSpec((rows, d), lambda j: (0, 0)),
                  pl.BlockSpec((d, tn), lambda j: (0, j)),
                  pl.BlockSpec((1, tn), lambda j: (0, j)),
                  slab],
        out_specs=[pl.BlockSpec((rows, tn), lambda j: (0, j)), slab],
        out_shape=[jax.ShapeDtypeStruct((rows, n), F32), jax.ShapeDtypeStruct((r, c), BF16)],
        compiler_params=_params(("arbitrary",)),
        name="mod",
    )(c_rows, ada_w, ada_b, cast_weight)


def _norm_modulate(x, g_ref, sc_ref, sh_ref):
    gain = g_ref[...] * (1.0 + sc_ref[0])
    return (x * lax.rsqrt(jnp.mean(x * x, axis=-1, keepdims=True) + EPS) * gain + sh_ref[0]).astype(BF16)


def _inproj_kernel(x_ref, sh_ref, sc_ref, g_ref, w_ref, wlr_ref, p_ref, lr_ref, a_scr):
    @pl.when(pl.program_id(1) == 0)
    def _():
        a = _norm_modulate(x_ref[...], g_ref, sc_ref, sh_ref)
        a_scr[...] = a
        lr_ref[...] = _nt_dot(a, wlr_ref[...])

    p_ref[...] = _nt_dot(a_scr[...], w_ref[...]).astype(p_ref.dtype)


def _inproj(x2d, shift, scale, gain, w_t, w_lr_t, tm, tn, first_col, n_cols, rows_per_mod):
    m, d = x2d.shape
    n = n_cols
    assert m % tm == 0 and first_col % tn == 0 and n_cols % tn == 0 and rows_per_mod % tm == 0
    tiles_per_mod = rows_per_mod // tm
    blk0 = first_col // tn
    return pl.pallas_call(
        _inproj_kernel,
        grid=(m // tm, n_cols // tn),
        in_specs=[pl.BlockSpec((tm, d), lambda i, j: (i, 0)),
                  pl.BlockSpec((1, 1, d), lambda i, j: (i // tiles_per_mod, 0, 0)),
                  pl.BlockSpec((1, 1, d), lambda i, j: (i // tiles_per_mod, 0, 0)),
                  pl.BlockSpec((1, d), lambda i, j: (0, 0)),
                  pl.BlockSpec((tn, d), lambda i, j: (blk0 + j, 0)),
                  pl.BlockSpec((LANES, d), lambda i, j: (0, 0))],
        out_specs=[pl.BlockSpec((tm, tn), lambda i, j: (i, j)),
                   pl.BlockSpec((tm, LANES), lambda i, j: (i, 0))],
        out_shape=[jax.ShapeDtypeStruct((m, n), BF16),
                   jax.ShapeDtypeStruct((m, LANES), F32)],
        scratch_shapes=[pltpu.VMEM((tm, d), BF16)],
        compiler_params=_params(("parallel", "arbitrary")),
        name="inproj",
    )(x2d, shift, scale, gain, w_t, w_lr_t)


def _spatial_gating(u, v, lg_ref, lb_ref, ws_ref, bs_ref, ya_ref):
    for r0 in range(0, u.shape[0], SGU_CHUNK):
        for g in range(SGU_GROUPS):
            c0 = g * SGU_GW
            vg = v[r0:r0 + SGU_CHUNK, c0:c0 + SGU_GW]
            vc = vg - jnp.mean(vg, axis=-1, keepdims=True)
            var = jnp.mean(vc * vc, axis=-1, keepdims=True)
            cs = pl.ds(c0, SGU_GW)
            vn = vc * lax.rsqrt(var + EPS) * lg_ref[:, cs] + lb_ref[:, cs]
            s = _dot(ws_ref[g], vn.astype(BF16)) + bs_ref[g]
            ya_ref[pl.ds(r0, SGU_CHUNK), cs] = (u[r0:r0 + SGU_CHUNK, c0:c0 + SGU_GW] * s).astype(ya_ref.dtype)


def _inproj_rows_kernel(x_ref, sh_ref, sc_ref, g_ref, w_ref, wlr_ref, lg_ref, lb_ref, ws_ref, bs_ref, *rest):
    n_cast = (len(rest) - 3) // 2
    p_ref, lr_ref, ya_ref = rest[n_cast:n_cast + 3]
    _cast_slabs(rest[:n_cast], rest[n_cast + 3:])
    a = _norm_modulate(x_ref[...], g_ref, sc_ref, sh_ref)
    lr_ref[...] = _nt_dot(a, wlr_ref[...])
    u = _gelu(_nt_dot(a, w_ref[pl.ds(COL_U, MIX_A), :]))
    v = _gelu(_nt_dot(a, w_ref[pl.ds(COL_VS, MIX_A), :]))
    _spatial_gating(u, v, lg_ref, lb_ref, ws_ref, bs_ref, ya_ref)
    qk = _nt_dot(a, w_ref[pl.ds(COL_Q, 2 * GLA_KDIM), :]).astype(p_ref.dtype)
    vv = _nt_dot(a, w_ref[pl.ds(COL_V, MIX_B), :]).astype(p_ref.dtype)
    rr = _silu(_nt_dot(a, w_ref[pl.ds(COL_R, MIX_B), :])).astype(p_ref.dtype)
    for h in range(GLA_HEADS):
        base = h * P_HEAD
        p_ref[:, pl.ds(base + P_Q, GLA_DK)] = qk[:, h * GLA_DK:(h + 1) * GLA_DK]
        p_ref[:, pl.ds(base + P_K, GLA_DK)] = qk[:, GLA_KDIM + h * GLA_DK:GLA_KDIM + (h + 1) * GLA_DK]
        p_ref[:, pl.ds(base + P_V, GLA_DV)] = vv[:, h * GLA_DV:(h + 1) * GLA_DV]
        p_ref[:, pl.ds(base + P_R, GLA_DV)] = rr[:, h * GLA_DV:(h + 1) * GLA_DV]


def _inproj_rows(x2d, shift, scale, gain, w_t, w_lr_t, ln_g, ln_b, w_s, b_s, cast_weights, tm, rows_per_mod):
    m, d = x2d.shape
    assert m % tm == 0 and rows_per_mod % tm == 0 and tm % SGU_CHUNK == 0
    tiles_per_mod = rows_per_mod // tm
    cast_in, cast_out, cast_shapes = _cast_specs(cast_weights, m // tm)
    whole = lambda shape: pl.BlockSpec(shape, lambda i: (0,) * len(shape))
    return pl.pallas_call(
        _inproj_rows_kernel,
        grid=(m // tm,),
        in_specs=[pl.BlockSpec((tm, d), lambda i: (i, 0)),
                  pl.BlockSpec((1, 1, d), lambda i: (i // tiles_per_mod, 0, 0)),
                  pl.BlockSpec((1, 1, d), lambda i: (i // tiles_per_mod, 0, 0)),
                  whole((1, d)),
                  pl.BlockSpec((N_MAIN, d), lambda i: (0, 0), pipeline_mode=pl.Buffered(1)),
                  pl.BlockSpec((LANES, d), lambda i: (0, 0), pipeline_mode=pl.Buffered(1)),
                  whole((1, MIX_A)), whole((1, MIX_A)),
                  whole((SGU_GROUPS, SGU_CHUNK, SGU_CHUNK)), whole((SGU_GROUPS, SGU_CHUNK, 1))] + cast_in,
        out_specs=[pl.BlockSpec((tm, P_COLS), lambda i: (i, 0)),
                   pl.BlockSpec((tm, LANES), lambda i: (i, 0)),
                   pl.BlockSpec((tm, MIX_A), lambda i: (i, 0))] + cast_out,
        out_shape=[jax.ShapeDtypeStruct((m, P_COLS), BF16),
                   jax.ShapeDtypeStruct((m, LANES), F32),
                   jax.ShapeDtypeStruct((m, MIX_A), BF16)] + cast_shapes,
        compiler_params=_params(("arbitrary",)),
        name="inproj_rows",
    )(x2d, shift, scale, gain, w_t, w_lr_t, ln_g, ln_b, w_s, b_s, *cast_weights)


GLA_BLOCK = 4 * GLA_CHUNK
GLA_PAIR = 2 * GLA_CHUNK
GLA_GATE_BLOCKS = 2
GLA_UNROLL = 8


LOG2E = 1.4426950408889634


def _log2_decay(z):
    soft = jnp.log2(1.0 + jnp.exp2(jnp.abs(z) * (-LOG2E)))
    return jnp.minimum(z, 0.0) * (LOG2E / GLA_NORMALIZER) - soft * (1.0 / GLA_NORMALIZER)


def _split(x):
    hi = x.astype(BF16)
    return hi, (x - hi.astype(F32)).astype(BF16)


def _gla_masks():
    n, c = GLA_BLOCK, GLA_CHUNK
    r = jnp.arange(n)[:, None]
    s = jnp.arange(n)[None, :]
    same = (r // c) == (s // c)
    return ((same & (s <= r)).astype(BF16), (same & (s >= r)).astype(BF16), jnp.eye(GLA_DV, dtype=BF16))


def _chunk_rows(blk, cidx, rows):
    return pl.ds(pl.multiple_of((blk * (GLA_BLOCK // GLA_CHUNK) + cidx) * rows, rows), rows)


def _gla_gates(rs, blk, q_ref, k_ref, v_ref, lr_ref, w_ref, b_ref, lower_ref, eye_ref, vt_scr, fwd, bwd):
    c, dk = GLA_CHUNK, GLA_DK
    vt_scr[:, rs] = _nt_dot(eye_ref[...], v_ref[rs, :]).astype(BF16)
    g = _log2_decay(_dot(lr_ref[rs, :].astype(BF16), w_ref[...]) + b_ref[...])
    g_hi, g_lo = _split(g)
    nb = g.shape[0]
    p = jnp.concatenate([_dot(lower_ref[...], g_hi[r0:r0 + GLA_BLOCK]) + _dot(lower_ref[...], g_lo[r0:r0 + GLA_BLOCK])
                         for r0 in range(0, nb, GLA_BLOCK)], axis=0)
    last = [p[i * c + c - 1:(i + 1) * c, :] for i in range(nb // c)]
    tot = jnp.concatenate([jnp.broadcast_to(t, (c, 2 * dk)) for t in last], axis=0)
    k32 = k_ref[rs, :].astype(F32)
    q32 = None if q_ref is None else q_ref[rs, :].astype(F32) * (GLA_DK ** -0.5)
    plans = ((fwd, slice(0, dk), p, tot - p), (bwd, slice(dk, 2 * dk), tot - p + g, p - g))
    for d, lanes, b_cum, to_end in plans:
        d["kd"][rs, :] = (k32 * jnp.exp2(to_end[:, lanes])).astype(BF16)
        for cidx in range(nb // c):
            dec = jnp.exp2(last[cidx][:, lanes])
            d["dec"][_chunk_rows(blk, cidx, SUBLANES), :] = jnp.broadcast_to(dec, (SUBLANES, dk))
        if q32 is not None:
            d["qe"][rs, :] = (q32 * jnp.exp2(b_cum[:, lanes])).astype(BF16)
            d["ke"][rs, :] = (k32 * jnp.exp2(-b_cum[:, lanes])).astype(BF16)


def _gla_scan_block(rs, blk, order, d, vt_scr, s_scr, record):
    c = GLA_CHUNK
    v_t = vt_scr[:, rs]
    kd = d["kd"][rs, :]
    zeros = jnp.zeros((c, kd.shape[1]), kd.dtype)
    s = s_scr[...]
    for cidx in order:
        pair, half = divmod(cidx, 2)
        slab = v_t[:, pair * GLA_PAIR:(pair + 1) * GLA_PAIR]
        kc = kd[cidx * c:(cidx + 1) * c]
        k_only = jnp.concatenate([kc, zeros] if half == 0 else [zeros, kc], axis=0)
        kv_t = _dot(slab, k_only)
        if record:
            d["snap"][_chunk_rows(blk, cidx, GLA_DV), :] = s.astype(BF16)
        dec = d["dec"][_chunk_rows(blk, cidx, SUBLANES), :]
        s = s * dec[0:1, :] + kv_t
    s_scr[...] = s


def _gla_output(rs, blk, v_ref, r_ref, qe_scr, snap_scr, fwd, bwd, ng_ref, y_ref):
    c, dk = GLA_CHUNK, GLA_DK
    qe = qe_scr[rs, :]
    s_f = _nt_dot(qe[:, :dk], fwd["ke"][rs, :])
    s_b = _nt_dot(qe[:, dk:], bwd["ke"][rs, :])
    att = jnp.where(fwd["tri"][...] > 0, s_f, 0.0) + jnp.where(bwd["tri"][...] > 0, s_b, 0.0)
    inter = [_nt_dot(qe[cidx * c:(cidx + 1) * c], snap_scr[_chunk_rows(blk, cidx, GLA_DV), :])
             for cidx in range(GLA_BLOCK // c)]
    o = _dot(att.astype(BF16), v_ref[rs, :]) + jnp.concatenate(inter, axis=0)
    o = o * lax.rsqrt(jnp.mean(o * o, axis=-1, keepdims=True) + EPS) * ng_ref[...]
    y_ref[rs, :] = (o * r_ref[rs, :].astype(F32)).astype(y_ref.dtype)


def _gla_kernel(p_ref, lr_ref, kc_ref, vc_ref, lrc_ref, w_ref, b_ref, ng_ref,
                lower_ref, upper_ref, eye_ref,
                y_ref, sf_scr, sb_scr, vt_scr, qe_scr, kef_scr, keb_scr, kdf_scr, kdb_scr,
                decf_scr, decb_scr, snap_scr):
    lanes_f, lanes_b = pl.ds(0, GLA_DK), pl.ds(GLA_DK, GLA_DK)
    fwd = dict(tri=lower_ref, qe=qe_scr.at[:, lanes_f], ke=kef_scr, kd=kdf_scr, dec=decf_scr,
               snap=snap_scr.at[:, lanes_f])
    bwd = dict(tri=upper_ref, qe=qe_scr.at[:, lanes_b], ke=keb_scr, kd=kdb_scr, dec=decb_scr,
               snap=snap_scr.at[:, lanes_b])
    q_ref, k_ref = p_ref.at[:, pl.ds(P_Q, GLA_DK)], p_ref.at[:, pl.ds(P_K, GLA_DK)]
    v_ref, r_ref = p_ref.at[:, pl.ds(P_V, GLA_DV)], p_ref.at[:, pl.ds(P_R, GLA_DV)]
    l, lc = p_ref.shape[0], kc_ref.shape[0]
    n_blk, n_cblk = l // GLA_BLOCK, lc // GLA_BLOCK
    asc = tuple(range(GLA_BLOCK // GLA_CHUNK))
    desc = asc[::-1]

    def block_rows(j):
        return pl.ds(pl.multiple_of(j * GLA_BLOCK, GLA_BLOCK), GLA_BLOCK)

    sf_scr[...] = jnp.zeros_like(sf_scr)
    sb_scr[...] = jnp.zeros_like(sb_scr)

    for j in range(n_cblk):
        _gla_gates(block_rows(j), j, None, kc_ref, vc_ref, lrc_ref, w_ref, b_ref, lower_ref, eye_ref,
                   vt_scr, fwd, bwd)
    for j in range(n_cblk):
        _gla_scan_block(block_rows(j), j, asc, fwd, vt_scr, sf_scr, False)
        jb = n_cblk - 1 - j
        _gla_scan_block(block_rows(jb), jb, desc, bwd, vt_scr, sb_scr, False)

    def gates(j, carry):
        rows = GLA_GATE_BLOCKS * GLA_BLOCK
        rs = pl.ds(pl.multiple_of(j * rows, rows), rows)
        _gla_gates(rs, j * GLA_GATE_BLOCKS, q_ref, k_ref, v_ref, lr_ref, w_ref, b_ref, lower_ref, eye_ref,
                   vt_scr, fwd, bwd)
        return carry

    def scan(j, carry):
        _gla_scan_block(block_rows(j), j, asc, fwd, vt_scr, sf_scr, True)
        jb = n_blk - 1 - j
        _gla_scan_block(block_rows(jb), jb, desc, bwd, vt_scr, sb_scr, True)
        return carry

    def output(j, carry):
        _gla_output(block_rows(j), j, v_ref, r_ref, qe_scr, snap_scr, fwd, bwd, ng_ref, y_ref)
        return carry

    lax.fori_loop(0, n_blk // GLA_GATE_BLOCKS, gates, 0)
    lax.fori_loop(0, n_blk, scan, 0, unroll=GLA_UNROLL)
    lax.fori_loop(0, n_blk, output, 0, unroll=GLA_UNROLL)


def _gla(p_lat, lr_lat, p_ctx, lr_ctx, gate_w, gate_b, norm_g):
    bsz, l, _ = p_lat.shape
    lc = p_ctx.shape[1]
    assert l % GLA_BLOCK == 0 and lc % GLA_BLOCK == 0 and lc <= l
    gate_specs = [pl.BlockSpec((LANES, 2 * GLA_DK), lambda b, h: (0, h)),
                  pl.BlockSpec((1, 2 * GLA_DK), lambda b, h: (0, h))]
    masks = _gla_masks()
    mask_specs = [pl.BlockSpec(m.shape, lambda b, h: (0, 0)) for m in masks]
    n_chunks = l // GLA_CHUNK
    return pl.pallas_call(
        _gla_kernel,
        grid=(bsz, GLA_HEADS),
        in_specs=[pl.BlockSpec((None, l, P_HEAD), lambda b, h: (b, 0, h)),
                  pl.BlockSpec((None, l, LANES), lambda b, h: (b, 0, 0)),
                  pl.BlockSpec((None, lc, GLA_DK), lambda b, h: (b, 0, h)),
                  pl.BlockSpec((None, lc, GLA_DV), lambda b, h: (b, 0, GLA_KDIM // GLA_DV + h)),
                  pl.BlockSpec((None, lc, LANES), lambda b, h: (b, 0, 0))]
                 + gate_specs
                 + [pl.BlockSpec((1, GLA_DV), lambda b, h: (0, 0))] + mask_specs,
        out_specs=pl.BlockSpec((None, l, GLA_DV), lambda b, h: (b, 0, h)),
        out_shape=jax.ShapeDtypeStruct((bsz, l, MIX_B), BF16),
        scratch_shapes=[pltpu.VMEM((GLA_DV, GLA_DK), F32)] * 2
                       + [pltpu.VMEM((GLA_DV, l), BF16)]
                       + [pltpu.VMEM((l, 2 * GLA_DK), BF16)]
                       + [pltpu.VMEM((l, GLA_DK), BF16)] * 4
                       + [pltpu.VMEM((n_chunks * SUBLANES, GLA_DK), F32)] * 2
                       + [pltpu.VMEM((n_chunks * GLA_DV, 2 * GLA_DK), BF16)],
        compiler_params=_params(("parallel", "arbitrary")),
        name="gla",
    )(p_lat, lr_lat, p_ctx, p_ctx, lr_ctx, gate_w, gate_b, norm_g, *masks)


def _cast_specs(weights, n_steps, step_index=lambda i: i):
    ins, outs, shapes = [], [], []
    for w in weights:
        rows, cols = w.shape
        assert rows % (n_steps * BF16_TILE_ROWS) == 0, "slabs must be whole packed bf16 row tiles"
        spec = pl.BlockSpec((rows // n_steps, cols), lambda *idx: (step_index(*idx), 0))
        ins.append(spec)
        outs.append(spec)
        shapes.append(jax.ShapeDtypeStruct(w.shape, BF16))
    return ins, outs, shapes


def _cast_slabs(src_refs, dst_refs):
    for src, dst in zip(src_refs, dst_refs):
        dst[...] = src[...].astype(dst.dtype)


def _rms(x, g):
    return x * lax.rsqrt(jnp.mean(x * x, axis=-1, keepdims=True) + EPS) * g


def _outproj_kernel(ya_ref, yb_ref, wa_ref, wb_ref, x_ref, gt_ref, pg_ref, fg_ref, sc_ref, sh_ref, *rest):
    n_cast = (len(rest) - 2) // 2
    h_ref, f_ref = rest[n_cast:n_cast + 2]
    _cast_slabs(rest[:n_cast], rest[n_cast + 2:])
    gate_gain = gt_ref[0] * pg_ref[...]
    mod_gain = fg_ref[...] * (1.0 + sc_ref[0])
    for r0 in range(0, h_ref.shape[0], ROW_SUBTILE):
        rs = pl.ds(r0, ROW_SUBTILE)
        y = _dot(ya_ref[rs, :], wa_ref[...]) + _dot(yb_ref[rs, :], wb_ref[...])
        h = x_ref[rs, :] + _rms(y, gate_gain)
        h_ref[rs, :] = h
        f_ref[rs, :] = (_rms(h, mod_gain) + sh_ref[0]).astype(f_ref.dtype)


def _outproj(y_a, y_b, w_out, x2d, gate, post_g, ffn_g, scale_f, shift_f, cast_weights, rows_per_mod, tm=512):
    m, d = x2d.shape
    tiles_per_mod = rows_per_mod // tm
    mod_spec = pl.BlockSpec((1, 1, d), lambda i: (i // tiles_per_mod, 0, 0))
    vec_spec = pl.BlockSpec((1, d), lambda i: (0, 0))
    cast_in, cast_out, cast_shapes = _cast_specs(cast_weights, m // tm)
    return pl.pallas_call(
        _outproj_kernel,
        grid=(m // tm,),
        in_specs=[pl.BlockSpec((tm, MIX_A), lambda i: (i, 0)),
                  pl.BlockSpec((tm, MIX_B), lambda i: (i, 0)),
                  pl.BlockSpec((MIX_A, d), lambda i: (0, 0), pipeline_mode=pl.Buffered(1)),
                  pl.BlockSpec((MIX_B, d), lambda i: (MIX_A // MIX_B, 0), pipeline_mode=pl.Buffered(1)),
                  pl.BlockSpec((tm, d), lambda i: (i, 0)),
                  mod_spec, vec_spec, vec_spec, mod_spec, mod_spec] + cast_in,
        out_specs=[pl.BlockSpec((tm, d), lambda i: (i, 0)),
                   pl.BlockSpec((tm, d), lambda i: (i, 0))] + cast_out,
        out_shape=[jax.ShapeDtypeStruct((m, d), F32),
                   jax.ShapeDtypeStruct((m, d), BF16)] + cast_shapes,
        compiler_params=_params(("arbitrary",)),
        name="outproj",
    )(y_a, y_b, w_out, w_out, x2d, gate, post_g, ffn_g, scale_f, shift_f, *cast_weights)


def _conv_gelu(a, cw, cb):
    hw = GRID_W
    ext = a.shape[0]
    tm = ext - 2 * hw
    rows = [a[dr * hw:dr * hw + tm, :] for dr in range(3)]
    taps = [sum(cw[3 * dr + dc:3 * dr + dc + 1, :] * rows[dr] for dr in range(3)) for dc in range(3)]
    col = lax.broadcasted_iota(jnp.int32, (tm, 1), 0) % hw
    left = jnp.where(col > 0, pltpu.roll(taps[0], 1, 0), 0.0)
    right = jnp.where(col < hw - 1, pltpu.roll(taps[2], tm - 1, 0), 0.0)
    return _gelu(cb + taps[1] + left + right)


def _ffn_up_kernel(f_ref, fp_ref, fn_ref, wa_ref, wv_ref, cw_ref, cb_ref, *rest, tiles_per_image):
    n_cast = (len(rest) - 2) // 2
    g_ref, fext_scr = rest[n_cast], rest[-1]
    _cast_slabs(rest[:n_cast], rest[n_cast + 1:-1])
    i = pl.program_id(0)
    j = pl.program_id(1)
    tm = f_ref.shape[0]
    hw = GRID_W

    @pl.when(j == 0)
    def _():
        top = (i % tiles_per_image) == 0
        bottom = (i % tiles_per_image) == tiles_per_image - 1
        fext_scr[pl.ds(0, hw), :] = jnp.where(top, jnp.zeros_like(fp_ref), fp_ref[...])
        fext_scr[pl.ds(hw, tm), :] = f_ref[...]
        fext_scr[pl.ds(hw + tm, hw), :] = jnp.where(bottom, jnp.zeros_like(fn_ref), fn_ref[...])

    a = _dot(fext_scr[...], wa_ref[...])
    val = _dot(f_ref[...], wv_ref[...])
    g_ref[...] = (_conv_gelu(a, cw_ref[...], cb_ref[...]) * val).astype(g_ref.dtype)


def _ffn_up(f2d, w_up, conv_w, conv_b, cast_weights, rows_per_image, tm=1024, tf=512):
    m, d = f2d.shape
    d_ff = w_up.shape[1] // 2
    assert rows_per_image % tm == 0 and tm % GRID_W == 0 and d_ff % tf == 0
    hb = tm // GRID_W
    n_halo = m // GRID_W
    n_j = d_ff // tf
    cast_in, cast_out, cast_shapes = _cast_specs(cast_weights, (m // tm) * n_j, lambda i, j: i * n_j + j)
    return pl.pallas_call(
        functools.partial(_ffn_up_kernel, tiles_per_image=rows_per_image // tm),
        grid=(m // tm, n_j),
        in_specs=[pl.BlockSpec((tm, d), lambda i, j: (i, 0)),
                  pl.BlockSpec((GRID_W, d), lambda i, j: (jnp.maximum(i * hb - 1, 0), 0)),
                  pl.BlockSpec((GRID_W, d), lambda i, j: (jnp.minimum((i + 1) * hb, n_halo - 1), 0)),
                  pl.BlockSpec((d, tf), lambda i, j: (0, j)),
                  pl.BlockSpec((d, tf), lambda i, j: (0, n_j + j)),
                  pl.BlockSpec((9, tf), lambda i, j: (0, j)),
                  pl.BlockSpec((1, tf), lambda i, j: (0, j))] + cast_in,
        out_specs=[pl.BlockSpec((tm, tf), lambda i, j: (i, j))] + cast_out,
        out_shape=[jax.ShapeDtypeStruct((m, d_ff), BF16)] + cast_shapes,
        scratch_shapes=[pltpu.VMEM((tm + 2 * GRID_W, d), BF16)],
        compiler_params=_params(("arbitrary", "arbitrary")),
        name="ffn_up",
    )(f2d, f2d, f2d, w_up, w_up, conv_w, conv_b, *cast_weights)


def _ffn_down_kernel(g_ref, w_ref, h_ref, gt_ref, pg_ref, o_ref):
    gate_gain = gt_ref[0] * pg_ref[...]
    for r0 in range(0, o_ref.shape[0], ROW_SUBTILE):
        rs = pl.ds(r0, ROW_SUBTILE)
        y = _dot(g_ref[rs, :], w_ref[...])
        o_ref[rs, :] = h_ref[rs, :] + _rms(y, gate_gain)


def _ffn_down(g2d, w_down, h2d, gate, post_g, rows_per_mod, tm=512):
    m, d = h2d.shape
    d_ff = g2d.shape[1]
    assert m % tm == 0
    tiles_per_mod = rows_per_mod // tm
    return pl.pallas_call(
        _ffn_down_kernel,
        grid=(m // tm,),
        in_specs=[pl.BlockSpec((tm, d_ff), lambda i: (i, 0)),
                  pl.BlockSpec((d_ff, d), lambda i: (0, 0), pipeline_mode=pl.Buffered(1)),
                  pl.BlockSpec((tm, d), lambda i: (i, 0)),
                  pl.BlockSpec((1, 1, d), lambda i: (i // tiles_per_mod, 0, 0)),
                  pl.BlockSpec((1, d), lambda i: (0, 0))],
        out_specs=pl.BlockSpec((tm, d), lambda i: (i, 0)),
        out_shape=jax.ShapeDtypeStruct((m, d), F32),
        compiler_params=pltpu.CompilerParams(dimension_semantics=("arbitrary",),
                                             vmem_limit_bytes=FFN_DOWN_VMEM_LIMIT),
        name="ffn_down",
    )(g2d, w_down, h2d, gate, post_g)


def _gate_operands(w_f, b_f, w_b, b_b):
    r, kdim = w_f.shape
    wp = jnp.zeros((LANES, GLA_HEADS, 2, GLA_DK), BF16)
    wp = wp.at[:r, :, 0].set(w_f.astype(BF16).reshape(r, GLA_HEADS, GLA_DK))
    wp = wp.at[r:2 * r, :, 1].set(w_b.astype(BF16).reshape(r, GLA_HEADS, GLA_DK))
    bias = jnp.stack([b_f.reshape(GLA_HEADS, GLA_DK), b_b.reshape(GLA_HEADS, GLA_DK)], axis=1)
    return wp.reshape(LANES, 2 * kdim), bias.reshape(1, 2 * kdim)


def kernel(x, c, ctx, c_ctx, ada_w, ada_b, pre_mix_g, post_mix_g, pre_ffn_g, post_ffn_g, w_in, sgu_ln_g, sgu_ln_b, sgu_w, sgu_b, gla_gate_w_f, gla_gate_b_f, gla_gate_w_b, gla_gate_b_b, gla_norm_g, w_out, ffn_w_up, ffn_conv_w, ffn_conv_b, ffn_w_down):
    bsz, l, d = x.shape
    lc = ctx.shape[1]
    assert ada_w.shape[0] == 1, "single-layer kernel"
    d_ff = ffn_w_down.shape[1]

    assert bsz < SUBLANES
    c_rows = jnp.zeros((SUBLANES, d), F32).at[:bsz].set(c).at[bsz].set(c_ctx)
    mod, w_in_t = _modulation(c_rows, ada_w[0], ada_b, jnp.swapaxes(w_in[0], 0, 1))
    sh_m, sc_m, gt_m, sh_f, sc_f, gt_f = [mod[:bsz, t * d:(t + 1) * d].reshape(bsz, 1, d) for t in range(N_MOD)]
    csh_m = mod[bsz:bsz + 1, 0:d].reshape(1, 1, d)
    csc_m = mod[bsz:bsz + 1, d:2 * d].reshape(1, 1, d)
    w_lr_t = jnp.pad(w_in_t[COL_LR:], ((0, LANES - 2 * GLA_LOWRANK), (0, 0)))

    p_ctx, lr_ctx = _inproj(ctx.reshape(bsz * lc, d), csh_m, csc_m, pre_mix_g, w_in_t, w_lr_t,
                            tm=2 * lc, tn=512, first_col=COL_K, n_cols=COL_R - COL_K, rows_per_mod=bsz * lc)
    x2d = x.reshape(bsz * l, d)
    p_lat, lr_lat, y_a, w_out_bf = _inproj_rows(
        x2d, sh_m, sc_m, pre_mix_g, w_in_t, w_lr_t, sgu_ln_g, sgu_ln_b, sgu_w[0].astype(BF16),
        sgu_b[0][:, :, None], cast_weights=(w_out[0],), tm=512, rows_per_mod=l)

    y_b = _gla(p_lat.reshape(bsz, l, P_COLS), lr_lat.reshape(bsz, l, LANES),
               p_ctx.reshape(bsz, lc, -1), lr_ctx.reshape(bsz, lc, LANES),
               *_gate_operands(gla_gate_w_f[0], gla_gate_b_f[0], gla_gate_w_b[0], gla_gate_b_b[0]), gla_norm_g)
    h, f, w_up_bf = _outproj(y_a, y_b.reshape(bsz * l, MIX_B), w_out_bf, x2d, gt_m,
                             post_mix_g, pre_ffn_g, sc_f, sh_f, cast_weights=(ffn_w_up[0],), rows_per_mod=l)
    g, w_down_bf = _ffn_up(f, w_up_bf, ffn_conv_w[0].reshape(9, d_ff), ffn_conv_b,
                           cast_weights=(ffn_w_down[0],), rows_per_image=l)
    out = _ffn_down(g, w_down_bf, h, gt_f, post_ffn_g, rows_per_mod=l)
    return out.reshape(bsz, l, d)
```

```python
import functools

import jax
import jax.numpy as jnp
from jax import lax
from jax.experimental import pallas as pl
from jax.experimental.pallas import tpu as pltpu

F32 = jnp.float32
BF16 = jnp.bfloat16

EPS = 1e-6
GRID_W = 64
MIX_A = 1024
MIX_B = 1024
SGU_GROUPS = 4
SGU_CHUNK = 128
SGU_GW = MIX_A // SGU_GROUPS
GLA_HEADS = 4
GLA_DK = 128
GLA_DV = 256
GLA_KDIM = GLA_HEADS * GLA_DK
GLA_LOWRANK = 16
GLA_NORMALIZER = 16.0
GLA_CHUNK = 64
N_MOD = 6
LANES = 128
SUBLANES = 8
BF16_TILE_ROWS = 16
ROW_SUBTILE = 256
VMEM_LIMIT = 56 * 1024 * 1024
FFN_DOWN_VMEM_LIMIT = 60 * 1024 * 1024

COL_U = 0
COL_VS = COL_U + MIX_A
COL_Q = COL_VS + MIX_A
COL_K = COL_Q + GLA_KDIM
COL_V = COL_K + GLA_KDIM
COL_R = COL_V + MIX_B
COL_LR = COL_R + MIX_B
N_MAIN = COL_LR
P_Q = 0
P_K = P_Q + GLA_DK
P_V = P_K + GLA_DK
P_R = P_V + GLA_DV
P_HEAD = P_R + GLA_DV
P_COLS = GLA_HEADS * P_HEAD


def _dot(a, b):
    return jnp.dot(a, b, preferred_element_type=F32)


def _nt_dot(a, b):
    return lax.dot_general(a, b, (((1,), (1,)), ((), ())), preferred_element_type=F32)


def _silu(x):
    return x / (1.0 + jnp.exp(-x))


_GELU_A = -2.0 * 0.7978845608028654 * 1.4426950408889634
_GELU_B = _GELU_A * 0.044715


def _gelu(x):
    return x / (1.0 + jnp.exp2((_GELU_A + _GELU_B * (x * x)) * x))


def _params(sem):
    return pltpu.CompilerParams(dimension_semantics=sem, vmem_limit_bytes=VMEM_LIMIT)


def _mod_kernel(c_ref, w_ref, b_ref, src_ref, o_ref, dst_ref):
    dst_ref[...] = src_ref[...].astype(dst_ref.dtype)
    s = _silu(c_ref[...]).astype(BF16)
    o_ref[...] = _dot(s, w_ref[...].astype(BF16)) + b_ref[...]


def _modulation(c_rows, ada_w, ada_b, cast_weight, tn=1536):
    rows, d = c_rows.shape
    n = ada_w.shape[1]
    steps = n // tn
    r, c = cast_weight.shape
    assert n % tn == 0 and c % (steps * LANES) == 0
    slab = pl.BlockSpec((r, c // steps), lambda j: (0, j))
    return pl.pallas_call(
        _mod_kernel,
        grid=(steps,),
        in_specs=[pl.BlockSpec((rows, d), lambda j: (0, 0)),
                  pl.BlockSpec((d, tn), lambda j: (0, j)),
                  pl.BlockSpec((1, tn), lambda j: (0, j)),
                  slab],
        out_specs=[pl.BlockSpec((rows, tn), lambda j: (0, j)), slab],
        out_shape=[jax.ShapeDtypeStruct((rows, n), F32), jax.ShapeDtypeStruct((r, c), BF16)],
        compiler_params=_params(("arbitrary",)),
        name="mod",
    )(c_rows, ada_w, ada_b, cast_weight)


def _norm_modulate(x, g_ref, sc_ref, sh_ref):
    gain = g_ref[...] * (1.0 + sc_ref[0])
    return (x * lax.rsqrt(jnp.mean(x * x, axis=-1, keepdims=True) + EPS) * gain + sh_ref[0]).astype(BF16)


def _inproj_kernel(x_ref, sh_ref, sc_ref, g_ref, w_ref, wlr_ref, p_ref, lr_ref, a_scr):
    @pl.when(pl.program_id(1) == 0)
    def _():
        a = _norm_modulate(x_ref[...], g_ref, sc_ref, sh_ref)
        a_scr[...] = a
        lr_ref[...] = _nt_dot(a, wlr_ref[...])

    p_ref[...] = _nt_dot(a_scr[...], w_ref[...]).astype(p_ref.dtype)


def _inproj(x2d, shift, scale, gain, w_t, w_lr_t, tm, tn, first_col, n_cols, rows_per_mod):
    m, d = x2d.shape
    n = n_cols
    assert m % tm == 0 and first_col % tn == 0 and n_cols % tn == 0 and rows_per_mod % tm == 0
    tiles_per_mod = rows_per_mod // tm
    blk0 = first_col // tn
    return pl.pallas_call(
        _inproj_kernel,
        grid=(m // tm, n_cols // tn),
        in_specs=[pl.BlockSpec((tm, d), lambda i, j: (i, 0)),
                  pl.BlockSpec((1, 1, d), lambda i, j: (i // tiles_per_mod, 0, 0)),
                  pl.BlockSpec((1, 1, d), lambda i, j: (i // tiles_per_mod, 0, 0)),
                  pl.BlockSpec((1, d), lambda i, j: (0, 0)),
                  pl.BlockSpec((tn, d), lambda i, j: (blk0 + j, 0)),
                  pl.BlockSpec((LANES, d), lambda i, j: (0, 0))],
        out_specs=[pl.BlockSpec((tm, tn), lambda i, j: (i, j)),
                   pl.BlockSpec((tm, LANES), lambda i, j: (i, 0))],
        out_shape=[jax.ShapeDtypeStruct((m, n), BF16),
                   jax.ShapeDtypeStruct((m, LANES), F32)],
        scratch_shapes=[pltpu.VMEM((tm, d), BF16)],
        compiler_params=_params(("parallel", "arbitrary")),
        name="inproj",
    )(x2d, shift, scale, gain, w_t, w_lr_t)


def _spatial_gating(u, v, lg_ref, lb_ref, ws_ref, bs_ref, ya_ref):
    for r0 in range(0, u.shape[0], SGU_CHUNK):
        for g in range(SGU_GROUPS):
            c0 = g * SGU_GW
            vg = v[r0:r0 + SGU_CHUNK, c0:c0 + SGU_GW]
            vc = vg - jnp.mean(vg, axis=-1, keepdims=True)
            var = jnp.mean(vc * vc, axis=-1, keepdims=True)
            cs = pl.ds(c0, SGU_GW)
            vn = vc * lax.rsqrt(var + EPS) * lg_ref[:, cs] + lb_ref[:, cs]
            s = _dot(ws_ref[g], vn.astype(BF16)) + bs_ref[g]
            ya_ref[pl.ds(r0, SGU_CHUNK), cs] = (u[r0:r0 + SGU_CHUNK, c0:c0 + SGU_GW] * s).astype(ya_ref.dtype)


def _inproj_rows_kernel(x_ref, sh_ref, sc_ref, g_ref, w_ref, wlr_ref, lg_ref, lb_ref, ws_ref, bs_ref, *rest):
    n_cast = (len(rest) - 3) // 2
    p_ref, lr_ref, ya_ref = rest[n_cast:n_cast + 3]
    _cast_slabs(rest[:n_cast], rest[n_cast + 3:])
    a = _norm_modulate(x_ref[...], g_ref, sc_ref, sh_ref)
    lr_ref[...] = _nt_dot(a, wlr_ref[...])
    u = _gelu(_nt_dot(a, w_ref[pl.ds(COL_U, MIX_A), :]))
    v = _gelu(_nt_dot(a, w_ref[pl.ds(COL_VS, MIX_A), :]))
    _spatial_gating(u, v, lg_ref, lb_ref, ws_ref, bs_ref, ya_ref)
    qk = _nt_dot(a, w_ref[pl.ds(COL_Q, 2 * GLA_KDIM), :]).astype(p_ref.dtype)
    vv = _nt_dot(a, w_ref[pl.ds(COL_V, MIX_B), :]).astype(p_ref.dtype)
    rr = _silu(_nt_dot(a, w_ref[pl.ds(COL_R, MIX_B), :])).astype(p_ref.dtype)
    for h in range(GLA_HEADS):
        base = h * P_HEAD
        p_ref[:, pl.ds(base + P_Q, GLA_DK)] = qk[:, h * GLA_DK:(h + 1) * GLA_DK]
        p_ref[:, pl.ds(base + P_K, GLA_DK)] = qk[:, GLA_KDIM + h * GLA_DK:GLA_KDIM + (h + 1) * GLA_DK]
        p_ref[:, pl.ds(base + P_V, GLA_DV)] = vv[:, h * GLA_DV:(h + 1) * GLA_DV]
        p_ref[:, pl.ds(base + P_R, GLA_DV)] = rr[:, h * GLA_DV:(h + 1) * GLA_DV]


def _inproj_rows(x2d, shift, scale, gain, w_t, w_lr_t, ln_g, ln_b, w_s, b_s, cast_weights, tm, rows_per_mod):
    m, d = x2d.shape
    assert m % tm == 0 and rows_per_mod % tm == 0 and tm % SGU_CHUNK == 0
    tiles_per_mod = rows_per_mod // tm
    cast_in, cast_out, cast_shapes = _cast_specs(cast_weights, m // tm)
    whole = lambda shape: pl.BlockSpec(shape, lambda i: (0,) * len(shape))
    return pl.pallas_call(
        _inproj_rows_kernel,
        grid=(m // tm,),
        in_specs=[pl.BlockSpec((tm, d), lambda i: (i, 0)),
                  pl.BlockSpec((1, 1, d), lambda i: (i // tiles_per_mod, 0, 0)),
                  pl.BlockSpec((1, 1, d), lambda i: (i // tiles_per_mod, 0, 0)),
                  whole((1, d)),
                  pl.BlockSpec((N_MAIN, d), lambda i: (0, 0), pipeline_mode=pl.Buffered(1)),
                  pl.BlockSpec((LANES, d), lambda i: (0, 0), pipeline_mode=pl.Buffered(1)),
                  whole((1, MIX_A)), whole((1, MIX_A)),
                  whole((SGU_GROUPS, SGU_CHUNK, SGU_CHUNK)), whole((SGU_GROUPS, SGU_CHUNK, 1))] + cast_in,
        out_specs=[pl.BlockSpec((tm, P_COLS), lambda i: (i, 0)),
                   pl.BlockSpec((tm, LANES), lambda i: (i, 0)),
                   pl.BlockSpec((tm, MIX_A), lambda i: (i, 0))] + cast_out,
        out_shape=[jax.ShapeDtypeStruct((m, P_COLS), BF16),
                   jax.ShapeDtypeStruct((m, LANES), F32),
                   jax.ShapeDtypeStruct((m, MIX_A), BF16)] + cast_shapes,
        compiler_params=_params(("arbitrary",)),
        name="inproj_rows",
    )(x2d, shift, scale, gain, w_t, w_lr_t, ln_g, ln_b, w_s, b_s, *cast_weights)


GLA_BLOCK = 4 * GLA_CHUNK
GLA_PAIR = 2 * GLA_CHUNK
GLA_GATE_BLOCKS = 2
GLA_UNROLL = 8


LOG2E = 1.4426950408889634


def _log2_decay(z):
    soft = jnp.log2(1.0 + jnp.exp2(jnp.abs(z) * (-LOG2E)))
    return jnp.minimum(z, 0.0) * (LOG2E / GLA_NORMALIZER) - soft * (1.0 / GLA_NORMALIZER)


def _split(x):
    hi = x.astype(BF16)
    return hi, (x - hi.astype(F32)).astype(BF16)


def _gla_masks():
    n, c = GLA_BLOCK, GLA_CHUNK
    r = jnp.arange(n)[:, None]
    s = jnp.arange(n)[None, :]
    same = (r // c) == (s // c)
    return ((same & (s <= r)).astype(BF16), (same & (s >= r)).astype(BF16), jnp.eye(GLA_DV, dtype=BF16))


def _chunk_rows(blk, cidx, rows):
    return pl.ds(pl.multiple_of((blk * (GLA_BLOCK // GLA_CHUNK) + cidx) * rows, rows), rows)


def _gla_gates(rs, blk, q_ref, k_ref, v_ref, lr_ref, w_ref, b_ref, lower_ref, eye_ref, vt_scr, fwd, bwd):
    c, dk = GLA_CHUNK, GLA_DK
    vt_scr[:, rs] = _nt_dot(eye_ref[...], v_ref[rs, :]).astype(BF16)
    g = _log2_decay(_dot(lr_ref[rs, :].astype(BF16), w_ref[...]) + b_ref[...])
    g_hi, g_lo = _split(g)
    nb = g.shape[0]
    p = jnp.concatenate([_dot(lower_ref[...], g_hi[r0:r0 + GLA_BLOCK]) + _dot(lower_ref[...], g_lo[r0:r0 + GLA_BLOCK])
                         for r0 in range(0, nb, GLA_BLOCK)], axis=0)
    last = [p[i * c + c - 1:(i + 1) * c, :] for i in range(nb // c)]
    tot = jnp.concatenate([jnp.broadcast_to(t, (c, 2 * dk)) for t in last], axis=0)
    k32 = k_ref[rs, :].astype(F32)
    q32 = None if q_ref is None else q_ref[rs, :].astype(F32) * (GLA_DK ** -0.5)
    plans = ((fwd, slice(0, dk), p, tot - p), (bwd, slice(dk, 2 * dk), tot - p + g, p - g))
    for d, lanes, b_cum, to_end in plans:
        d["kd"][rs, :] = (k32 * jnp.exp2(to_end[:, lanes])).astype(BF16)
        for cidx in range(nb // c):
            dec = jnp.exp2(last[cidx][:, lanes])
            d["dec"][_chunk_rows(blk, cidx, SUBLANES), :] = jnp.broadcast_to(dec, (SUBLANES, dk))
        if q32 is not None:
            d["qe"][rs, :] = (q32 * jnp.exp2(b_cum[:, lanes])).astype(BF16)
            d["ke"][rs, :] = (k32 * jnp.exp2(-b_cum[:, lanes])).astype(BF16)


def _gla_scan_block(rs, blk, order, d, vt_scr, s_scr, record):
    c = GLA_CHUNK
    v_t = vt_scr[:, rs]
    kd = d["kd"][rs, :]
    zeros = jnp.zeros((c, kd.shape[1]), kd.dtype)
    s = s_scr[...]
    for cidx in order:
        pair, half = divmod(cidx, 2)
        slab = v_t[:, pair * GLA_PAIR:(pair + 1) * GLA_PAIR]
        kc = kd[cidx * c:(cidx + 1) * c]
        k_only = jnp.concatenate([kc, zeros] if half == 0 else [zeros, kc], axis=0)
        kv_t = _dot(slab, k_only)
        if record:
            d["snap"][_chunk_rows(blk, cidx, GLA_DV), :] = s.astype(BF16)
        dec = d["dec"][_chunk_rows(blk, cidx, SUBLANES), :]
        s = s * dec[0:1, :] + kv_t
    s_scr[...] = s


def _gla_output(rs, blk, v_ref, r_ref, qe_scr, snap_scr, fwd, bwd, ng_ref, y_ref):
    c, dk = GLA_CHUNK, GLA_DK
    qe = qe_scr[rs, :]
    s_f = _nt_dot(qe[:, :dk], fwd["ke"][rs, :])
    s_b = _nt_dot(qe[:, dk:], bwd["ke"][rs, :])
    att = jnp.where(fwd["tri"][...] > 0, s_f, 0.0) + jnp.where(bwd["tri"][...] > 0, s_b, 0.0)
    inter = [_nt_dot(qe[cidx * c:(cidx + 1) * c], snap_scr[_chunk_rows(blk, cidx, GLA_DV), :])
             for cidx in range(GLA_BLOCK // c)]
    o = _dot(att.astype(BF16), v_ref[rs, :]) + jnp.concatenate(inter, axis=0)
    o = o * lax.rsqrt(jnp.mean(o * o, axis=-1, keepdims=True) + EPS) * ng_ref[...]
    y_ref[rs, :] = (o * r_ref[rs, :].astype(F32)).astype(y_ref.dtype)


def _gla_kernel(p_ref, lr_ref, kc_ref, vc_ref, lrc_ref, w_ref, b_ref, ng_ref,
                lower_ref, upper_ref, eye_ref,
                y_ref, sf_scr, sb_scr, vt_scr, qe_scr, kef_scr, keb_scr, kdf_scr, kdb_scr,
                decf_scr, decb_scr, snap_scr):
    lanes_f, lanes_b = pl.ds(0, GLA_DK), pl.ds(GLA_DK, GLA_DK)
    fwd = dict(tri=lower_ref, qe=qe_scr.at[:, lanes_f], ke=kef_scr, kd=kdf_scr, dec=decf_scr,
               snap=snap_scr.at[:, lanes_f])
    bwd = dict(tri=upper_ref, qe=qe_scr.at[:, lanes_b], ke=keb_scr, kd=kdb_scr, dec=decb_scr,
               snap=snap_scr.at[:, lanes_b])
    q_ref, k_ref = p_ref.at[:, pl.ds(P_Q, GLA_DK)], p_ref.at[:, pl.ds(P_K, GLA_DK)]
    v_ref, r_ref = p_ref.at[:, pl.ds(P_V, GLA_DV)], p_ref.at[:, pl.ds(P_R, GLA_DV)]
    l, lc = p_ref.shape[0], kc_ref.shape[0]
    n_blk, n_cblk = l // GLA_BLOCK, lc // GLA_BLOCK
    asc = tuple(range(GLA_BLOCK // GLA_CHUNK))
    desc = asc[::-1]

    def block_rows(j):
        return pl.ds(pl.multiple_of(j * GLA_BLOCK, GLA_BLOCK), GLA_BLOCK)

    sf_scr[...] = jnp.zeros_like(sf_scr)
    sb_scr[...] = jnp.zeros_like(sb_scr)

    for j in range(n_cblk):
        _gla_gates(block_rows(j), j, None, kc_ref, vc_ref, lrc_ref, w_ref, b_ref, lower_ref, eye_ref,
                   vt_scr, fwd, bwd)
    for j in range(n_cblk):
        _gla_scan_block(block_rows(j), j, asc, fwd, vt_scr, sf_scr, False)
        jb = n_cblk - 1 - j
        _gla_scan_block(block_rows(jb), jb, desc, bwd, vt_scr, sb_scr, False)

    def gates(j, carry):
        rows = GLA_GATE_BLOCKS * GLA_BLOCK
        rs = pl.ds(pl.multiple_of(j * rows, rows), rows)
        _gla_gates(rs, j * GLA_GATE_BLOCKS, q_ref, k_ref, v_ref, lr_ref, w_ref, b_ref, lower_ref, eye_ref,
                   vt_scr, fwd, bwd)
        return carry

    def scan(j, carry):
        _gla_scan_block(block_rows(j), j, asc, fwd, vt_scr, sf_scr, True)
        jb = n_blk - 1 - j
        _gla_scan_block(block_rows(jb), jb, desc, bwd, vt_scr, sb_scr, True)
        return carry

    def output(j, carry):
        _gla_output(block_rows(j), j, v_ref, r_ref, qe_scr, snap_scr, fwd, bwd, ng_ref, y_ref)
        return carry

    lax.fori_loop(0, n_blk // GLA_GATE_BLOCKS, gates, 0)
    lax.fori_loop(0, n_blk, scan, 0, unroll=GLA_UNROLL)
    lax.fori_loop(0, n_blk, output, 0, unroll=GLA_UNROLL)


def _gla(p_lat, lr_lat, p_ctx, lr_ctx, gate_w, gate_b, norm_g):
    bsz, l, _ = p_lat.shape
    lc = p_ctx.shape[1]
    assert l % GLA_BLOCK == 0 and lc % GLA_BLOCK == 0 and lc <= l
    gate_specs = [pl.BlockSpec((LANES, 2 * GLA_DK), lambda b, h: (0, h)),
                  pl.BlockSpec((1, 2 * GLA_DK), lambda b, h: (0, h))]
    masks = _gla_masks()
    mask_specs = [pl.BlockSpec(m.shape, lambda b, h: (0, 0)) for m in masks]
    n_chunks = l // GLA_CHUNK
    return pl.pallas_call(
        _gla_kernel,
        grid=(bsz, GLA_HEADS),
        in_specs=[pl.BlockSpec((None, l, P_HEAD), lambda b, h: (b, 0, h)),
                  pl.BlockSpec((None, l, LANES), lambda b, h: (b, 0, 0)),
                  pl.BlockSpec((None, lc, GLA_DK), lambda b, h: (b, 0, h)),
                  pl.BlockSpec((None, lc, GLA_DV), lambda b, h: (b, 0, GLA_KDIM // GLA_DV + h)),
                  pl.BlockSpec((None, lc, LANES), lambda b, h: (b, 0, 0))]
                 + gate_specs
                 + [pl.BlockSpec((1, GLA_DV), lambda b, h: (0, 0))] + mask_specs,
        out_specs=pl.BlockSpec((None, l, GLA_DV), lambda b, h: (b, 0, h)),
        out_shape=jax.ShapeDtypeStruct((bsz, l, MIX_B), BF16),
        scratch_shapes=[pltpu.VMEM((GLA_DV, GLA_DK), F32)] * 2
                       + [pltpu.VMEM((GLA_DV, l), BF16)]
                       + [pltpu.VMEM((l, 2 * GLA_DK), BF16)]
                       + [pltpu.VMEM((l, GLA_DK), BF16)] * 4
                       + [pltpu.VMEM((n_chunks * SUBLANES, GLA_DK), F32)] * 2
                       + [pltpu.VMEM((n_chunks * GLA_DV, 2 * GLA_DK), BF16)],
        compiler_params=_params(("parallel", "arbitrary")),
        name="gla",
    )(p_lat, lr_lat, p_ctx, p_ctx, lr_ctx, gate_w, gate_b, norm_g, *masks)


def _cast_specs(weights, n_steps, step_index=lambda i: i):
    ins, outs, shapes = [], [], []
    for w in weights:
        rows, cols = w.shape
        assert rows % (n_steps * BF16_TILE_ROWS) == 0, "slabs must be whole packed bf16 row tiles"
        spec = pl.BlockSpec((rows // n_steps, cols), lambda *idx: (step_index(*idx), 0))
        ins.append(spec)
        outs.append(spec)
        shapes.append(jax.ShapeDtypeStruct(w.shape, BF16))
    return ins, outs, shapes


def _cast_slabs(src_refs, dst_refs):
    for src, dst in zip(src_refs, dst_refs):
        dst[...] = src[...].astype(dst.dtype)


def _rms(x, g):
    return x * lax.rsqrt(jnp.mean(x * x, axis=-1, keepdims=True) + EPS) * g


def _outproj_kernel(ya_ref, yb_ref, wa_ref, wb_ref, x_ref, gt_ref, pg_ref, fg_ref, sc_ref, sh_ref, *rest):
    n_cast = (len(rest) - 2) // 2
    h_ref, f_ref = rest[n_cast:n_cast + 2]
    _cast_slabs(rest[:n_cast], rest[n_cast + 2:])
    gate_gain = gt_ref[0] * pg_ref[...]
    mod_gain = fg_ref[...] * (1.0 + sc_ref[0])
    for r0 in range(0, h_ref.shape[0], ROW_SUBTILE):
        rs = pl.ds(r0, ROW_SUBTILE)
        y = _dot(ya_ref[rs, :], wa_ref[...]) + _dot(yb_ref[rs, :], wb_ref[...])
        h = x_ref[rs, :] + _rms(y, gate_gain)
        h_ref[rs, :] = h
        f_ref[rs, :] = (_rms(h, mod_gain) + sh_ref[0]).astype(f_ref.dtype)


def _outproj(y_a, y_b, w_out, x2d, gate, post_g, ffn_g, scale_f, shift_f, cast_weights, rows_per_mod, tm=512):
    m, d = x2d.shape
    tiles_per_mod = rows_per_mod // tm
    mod_spec = pl.BlockSpec((1, 1, d), lambda i: (i // tiles_per_mod, 0, 0))
    vec_spec = pl.BlockSpec((1, d), lambda i: (0, 0))
    cast_in, cast_out, cast_shapes = _cast_specs(cast_weights, m // tm)
    return pl.pallas_call(
        _outproj_kernel,
        grid=(m // tm,),
        in_specs=[pl.BlockSpec((tm, MIX_A), lambda i: (i, 0)),
                  pl.BlockSpec((tm, MIX_B), lambda i: (i, 0)),
                  pl.BlockSpec((MIX_A, d), lambda i: (0, 0), pipeline_mode=pl.Buffered(1)),
                  pl.BlockSpec((MIX_B, d), lambda i: (MIX_A // MIX_B, 0), pipeline_mode=pl.Buffered(1)),
                  pl.BlockSpec((tm, d), lambda i: (i, 0)),
                  mod_spec, vec_spec, vec_spec, mod_spec, mod_spec] + cast_in,
        out_specs=[pl.BlockSpec((tm, d), lambda i: (i, 0)),
                   pl.BlockSpec((tm, d), lambda i: (i, 0))] + cast_out,
        out_shape=[jax.ShapeDtypeStruct((m, d), F32),
                   jax.ShapeDtypeStruct((m, d), BF16)] + cast_shapes,
        compiler_params=_params(("arbitrary",)),
        name="outproj",
    )(y_a, y_b, w_out, w_out, x2d, gate, post_g, ffn_g, scale_f, shift_f, *cast_weights)


def _conv_gelu(a, cw, cb):
    hw = GRID_W
    ext = a.shape[0]
    tm = ext - 2 * hw
    rows = [a[dr * hw:dr * hw + tm, :] for dr in range(3)]
    taps = [sum(cw[3 * dr + dc:3 * dr + dc + 1, :] * rows[dr] for dr in range(3)) for dc in range(3)]
    col = lax.broadcasted_iota(jnp.int32, (tm, 1), 0) % hw
    left = jnp.where(col > 0, pltpu.roll(taps[0], 1, 0), 0.0)
    right = jnp.where(col < hw - 1, pltpu.roll(taps[2], tm - 1, 0), 0.0)
    return _gelu(cb + taps[1] + left + right)


def _ffn_up_kernel(f_ref, fp_ref, fn_ref, wa_ref, wv_ref, cw_ref, cb_ref, *rest, tiles_per_image):
    n_cast = (len(rest) - 2) // 2
    g_ref, fext_scr = rest[n_cast], rest[-1]
    _cast_slabs(rest[:n_cast], rest[n_cast + 1:-1])
    i = pl.program_id(0)
    j = pl.program_id(1)
    tm = f_ref.shape[0]
    hw = GRID_W

    @pl.when(j == 0)
    def _():
        top = (i % tiles_per_image) == 0
        bottom = (i % tiles_per_image) == tiles_per_image - 1
        fext_scr[pl.ds(0, hw), :] = jnp.where(top, jnp.zeros_like(fp_ref), fp_ref[...])
        fext_scr[pl.ds(hw, tm), :] = f_ref[...]
        fext_scr[pl.ds(hw + tm, hw), :] = jnp.where(bottom, jnp.zeros_like(fn_ref), fn_ref[...])

    a = _dot(fext_scr[...], wa_ref[...])
    val = _dot(f_ref[...], wv_ref[...])
    g_ref[...] = (_conv_gelu(a, cw_ref[...], cb_ref[...]) * val).astype(g_ref.dtype)


def _ffn_up(f2d, w_up, conv_w, conv_b, cast_weights, rows_per_image, tm=1024, tf=512):
    m, d = f2d.shape
    d_ff = w_up.shape[1] // 2
    assert rows_per_image % tm == 0 and tm % GRID_W == 0 and d_ff % tf == 0
    hb = tm // GRID_W
    n_halo = m // GRID_W
    n_j = d_ff // tf
    cast_in, cast_out, cast_shapes = _cast_specs(cast_weights, (m // tm) * n_j, lambda i, j: i * n_j + j)
    return pl.pallas_call(
        functools.partial(_ffn_up_kernel, tiles_per_image=rows_per_image // tm),
        grid=(m // tm, n_j),
        in_specs=[pl.BlockSpec((tm, d), lambda i, j: (i, 0)),
                  pl.BlockSpec((GRID_W, d), lambda i, j: (jnp.maximum(i * hb - 1, 0), 0)),
                  pl.BlockSpec((GRID_W, d), lambda i, j: (jnp.minimum((i + 1) * hb, n_halo - 1), 0)),
                  pl.BlockSpec((d, tf), lambda i, j: (0, j)),
                  pl.BlockSpec((d, tf), lambda i, j: (0, n_j + j)),
                  pl.BlockSpec((9, tf), lambda i, j: (0, j)),
                  pl.BlockSpec((1, tf), lambda i, j: (0, j))] + cast_in,
        out_specs=[pl.BlockSpec((tm, tf), lambda i, j: (i, j))] + cast_out,
        out_shape=[jax.ShapeDtypeStruct((m, d_ff), BF16)] + cast_shapes,
        scratch_shapes=[pltpu.VMEM((tm + 2 * GRID_W, d), BF16)],
        compiler_params=_params(("arbitrary", "arbitrary")),
        name="ffn_up",
    )(f2d, f2d, f2d, w_up, w_up, conv_w, conv_b, *cast_weights)


def _ffn_down_kernel(g_ref, w_ref, h_ref, gt_ref, pg_ref, o_ref):
    gate_gain = gt_ref[0] * pg_ref[...]
    for r0 in range(0, o_ref.shape[0], ROW_SUBTILE):
        rs = pl.ds(r0, ROW_SUBTILE)
        y = _dot(g_ref[rs, :], w_ref[...])
        o_ref[rs, :] = h_ref[rs, :] + _rms(y, gate_gain)


def _ffn_down(g2d, w_down, h2d, gate, post_g, rows_per_mod, tm=512):
    m, d = h2d.shape
    d_ff = g2d.shape[1]
    assert m % tm == 0
    tiles_per_mod = rows_per_mod // tm
    return pl.pallas_call(
        _ffn_down_kernel,
        grid=(m // tm,),
        in_specs=[pl.BlockSpec((tm, d_ff), lambda i: (i, 0)),
                  pl.BlockSpec((d_ff, d), lambda i: (0, 0), pipeline_mode=pl.Buffered(1)),
                  pl.BlockSpec((tm, d), lambda i: (i, 0)),
                  pl.BlockSpec((1, 1, d), lambda i: (i // tiles_per_mod, 0, 0)),
                  pl.BlockSpec((1, d), lambda i: (0, 0))],
        out_specs=pl.BlockSpec((tm, d), lambda i: (i, 0)),
        out_shape=jax.ShapeDtypeStruct((m, d), F32),
        compiler_params=pltpu.CompilerParams(dimension_semantics=("arbitrary",),
                                             vmem_limit_bytes=FFN_DOWN_VMEM_LIMIT),
        name="ffn_down",
    )(g2d, w_down, h2d, gate, post_g)


def _gate_operands(w_f, b_f, w_b, b_b):
    r, kdim = w_f.shape
    wp = jnp.zeros((LANES, GLA_HEADS, 2, GLA_DK), BF16)
    wp = wp.at[:r, :, 0].set(w_f.astype(BF16).reshape(r, GLA_HEADS, GLA_DK))
    wp = wp.at[r:2 * r, :, 1].set(w_b.astype(BF16).reshape(r, GLA_HEADS, GLA_DK))
    bias = jnp.stack([b_f.reshape(GLA_HEADS, GLA_DK), b_b.reshape(GLA_HEADS, GLA_DK)], axis=1)
    return wp.reshape(LANES, 2 * kdim), bias.reshape(1, 2 * kdim)


def kernel(x, c, ctx, c_ctx, ada_w, ada_b, pre_mix_g, post_mix_g, pre_ffn_g, post_ffn_g, w_in, sgu_ln_g, sgu_ln_b, sgu_w, sgu_b, gla_gate_w_f, gla_gate_b_f, gla_gate_w_b, gla_gate_b_b, gla_norm_g, w_out, ffn_w_up, ffn_conv_w, ffn_conv_b, ffn_w_down):
    bsz, l, d = x.shape
    lc = ctx.shape[1]
    assert ada_w.shape[0] == 1, "single-layer kernel"
    d_ff = ffn_w_down.shape[1]

    assert bsz < SUBLANES
    c_rows = jnp.zeros((SUBLANES, d), F32).at[:bsz].set(c).at[bsz].set(c_ctx)
    mod, w_in_t = _modulation(c_rows, ada_w[0], ada_b, jnp.swapaxes(w_in[0], 0, 1))
    sh_m, sc_m, gt_m, sh_f, sc_f, gt_f = [mod[:bsz, t * d:(t + 1) * d].reshape(bsz, 1, d) for t in range(N_MOD)]
    csh_m = mod[bsz:bsz + 1, 0:d].reshape(1, 1, d)
    csc_m = mod[bsz:bsz + 1, d:2 * d].reshape(1, 1, d)
    w_lr_t = jnp.pad(w_in_t[COL_LR:], ((0, LANES - 2 * GLA_LOWRANK), (0, 0)))

    p_ctx, lr_ctx = _inproj(ctx.reshape(bsz * lc, d), csh_m, csc_m, pre_mix_g, w_in_t, w_lr_t,
                            tm=2 * lc, tn=512, first_col=COL_K, n_cols=COL_R - COL_K, rows_per_mod=bsz * lc)
    x2d = x.reshape(bsz * l, d)
    p_lat, lr_lat, y_a, w_out_bf = _inproj_rows(
        x2d, sh_m, sc_m, pre_mix_g, w_in_t, w_lr_t, sgu_ln_g, sgu_ln_b, sgu_w[0].astype(BF16),
        sgu_b[0][:, :, None], cast_weights=(w_out[0],), tm=512, rows_per_mod=l)

    y_b = _gla(p_lat.reshape(bsz, l, P_COLS), lr_lat.reshape(bsz, l, LANES),
               p_ctx.reshape(bsz, lc, -1), lr_ctx.reshape(bsz, lc, LANES),
               *_gate_operands(gla_gate_w_f[0], gla_gate_b_f[0], gla_gate_w_b[0], gla_gate_b_b[0]), gla_norm_g)
    h, f, w_up_bf = _outproj(y_a, y_b.reshape(bsz * l, MIX_B), w_out_bf, x2d, gt_m,
                             post_mix_g, pre_ffn_g, sc_f, sh_f, cast_weights=(ffn_w_up[0],), rows_per_mod=l)
    g, w_down_bf = _ffn_up(f, w_up_bf, ffn_conv_w[0].reshape(9, d_ff), ffn_conv_b,
                           cast_weights=(ffn_w_down[0],), rows_per_image=l)
    out = _ffn_down(g, w_down_bf, h, gt_f, post_ffn_g, rows_per_mod=l)
    return out.reshape(bsz, l, d)
```
